```python
import math
import jax, jax.numpy as jnp
from jax import lax
import numpy as np

D_MODEL = 1024
BATCH = 8
SEQ = 2048
DEPTH = 1
DEC_BATCH = 32
DEC_SEQ = 1
PAST_LEN = 16384
PAGE_SIZE = 128

N_ATTN_HEADS = 8
ATTN_HEAD_DIM = 64
ATTN_WIDTH = N_ATTN_HEADS * ATTN_HEAD_DIM
MOBA_BLOCK = 256
MOBA_TOPK = 3
Q_BLOCK = 128
N_BUCKETS = 32
MAX_DISTANCE = 128
N_M_HEADS = 4
M_HEAD_DIM = 128
M_WIDTH = N_M_HEADS * M_HEAD_DIM
CONV_WIDTH = 4
M_CHUNK = 128
MIX_WIDTH = ATTN_WIDTH + M_WIDTH
PROJ_SIZES = (ATTN_WIDTH,) * 3 + (M_WIDTH,) * 4 + (N_M_HEADS,) * 2
PROJ_WIDTH = sum(PROJ_SIZES)
PROJ_OFFSETS = tuple(sum(PROJ_SIZES[:i + 1]) for i in range(len(PROJ_SIZES) - 1))
N_KEYS = 128
N_EXPERTS = N_KEYS * N_KEYS
PEER_HEADS = 8
PEER_KEY_DIM = 256
PEER_TOPK = 16
PEER_ROWS = 256
LN_EPS = 1e-5
ALPHA = (2.0 * DEPTH) ** 0.25
BETA = (8.0 * DEPTH) ** -0.25

kernel_name = 'hymba_moba_mlstm_peer_step'


def standardize(x):
    xf = x.astype(jnp.float32)
    mu = jnp.mean(xf, -1, keepdims=True)
    var = jnp.mean(jnp.square(xf - mu), -1, keepdims=True)
    return (xf - mu) * lax.rsqrt(var + LN_EPS)


def layer_norm(x, gain, bias):
    return (standardize(x) * gain + bias).astype(x.dtype)


def t5_bucket(dist):
    max_exact = N_BUCKETS // 2
    d = jnp.maximum(dist, 0)
    far = max_exact + (jnp.log(jnp.maximum(d, 1).astype(jnp.float32) / max_exact)
                       / math.log(MAX_DISTANCE / max_exact) * (N_BUCKETS - max_exact)).astype(jnp.int32)
    return jnp.where(d < max_exact, d, jnp.minimum(far, N_BUCKETS - 1))


def head_bias(dist, head_idx, table):
    flat = jnp.transpose(table).reshape(-1)
    return flat[head_idx * N_BUCKETS + t5_bucket(dist)].astype(jnp.float32)


def moba_prompt(q, k, v, table):
    B, S, H, dh = q.shape
    scale = dh ** -0.5
    nb = -(-S // MOBA_BLOCK)
    nbs = max(nb, MOBA_TOPK)
    pad = nbs * MOBA_BLOCK - S
    kp = jnp.pad(k, ((0, 0), (0, pad), (0, 0), (0, 0)))
    vp = jnp.pad(v, ((0, 0), (0, pad), (0, 0), (0, 0)))
    kblk = kp.reshape(B, nbs, MOBA_BLOCK, H, dh).transpose(0, 3, 1, 2, 4)
    vblk = vp.reshape(B, nbs, MOBA_BLOCK, H, dh).transpose(0, 3, 1, 2, 4)
    kmean = jnp.mean(kblk.astype(jnp.float32), axis=3)
    qh = q.transpose(0, 2, 1, 3)
    nqb = S // Q_BLOCK
    heads = jnp.arange(H)
    offs = jnp.arange(MOBA_BLOCK)
    take = jax.vmap(lambda t, ix: t[ix])

    def one(i):
        b = i // nqb
        s0 = (i % nqb) * Q_BLOCK
        ob = s0 // MOBA_BLOCK
        qi = lax.dynamic_slice_in_dim(qh[b], s0, Q_BLOCK, axis=1).astype(jnp.float32)
        kb, vb = kblk[b], vblk[b]
        gate = jnp.einsum('hqd,hnd->hqn', qi, kmean[b])
        gate = jnp.where(jnp.arange(nbs) < ob, gate, -jnp.inf)
        _, sel = lax.top_k(gate, MOBA_TOPK)
        valid = sel < ob
        k_sel = take(kb, sel)
        v_sel = take(vb, sel)
        k_own = lax.dynamic_index_in_dim(kb, ob, axis=1, keepdims=False)
        v_own = lax.dynamic_index_in_dim(vb, ob, axis=1, keepdims=False)
        q_pos = s0 + jnp.arange(Q_BLOCK)
        sel_pos = sel[..., None] * MOBA_BLOCK + offs
        own_pos = ob * MOBA_BLOCK + offs
        l_sel = (jnp.einsum('hqd,hqjrd->hqjr', qi, k_sel) * scale
                 + head_bias(q_pos[:, None, None] - sel_pos, heads[:, None, None, None], table))
        l_sel = jnp.where(valid[..., None], l_sel, -jnp.inf)
        l_own = (jnp.einsum('hqd,hrd->hqr', qi, k_own) * scale
                 + head_bias(q_pos[:, None] - own_pos[None, :], heads[:, None, None], table))
        l_own = jnp.where(own_pos[None, :] <= q_pos[:, None], l_own, -jnp.inf)
        nsel = MOBA_TOPK * MOBA_BLOCK
        p = jax.nn.softmax(jnp.concatenate([l_sel.reshape(H, Q_BLOCK, nsel), l_own], -1), axis=-1)
        p_sel = p[..., :nsel].reshape(H, Q_BLOCK, MOBA_TOPK, MOBA_BLOCK)
        return (jnp.einsum('hqjr,hqjrd->hqd', p_sel, v_sel)
                + jnp.einsum('hqr,hrd->hqd', p[..., nsel:], v_own))

    out = lax.map(one, jnp.arange(B * nqb))
    return out.reshape(B, nqb, H, Q_BLOCK, dh).transpose(0, 1, 3, 2, 4).reshape(B, S, H * dh)


def moba_sample(q, k, v, cache_k, cache_v, page_table, table):
    DB, T, H, dh = q.shape
    scale = dh ** -0.5
    ppb = MOBA_BLOCK // PAGE_SIZE
    nfp = PAST_LEN // MOBA_BLOCK
    r = PAST_LEN - nfp * MOBA_BLOCK
    qf = q.astype(jnp.float32)
    heads = jnp.arange(H)
    q_pos = PAST_LEN + jnp.arange(T)
    logits, values, specs = [], [], []
    if nfp > 0:
        kt = min(MOBA_TOPK, nfp)
        page_sum = jnp.sum(cache_k.astype(jnp.float32), axis=1)
        kmean = page_sum[page_table[:, :nfp * ppb]].reshape(DB, nfp, ppb, H, dh).sum(2) / MOBA_BLOCK
        gate = jnp.einsum('bthd,bnhd->bhtn', qf, kmean)
        _, sel = lax.top_k(gate, kt)
        lpage = sel[..., None] * ppb + jnp.arange(ppb)
        phys = page_table[jnp.arange(DB)[:, None, None, None, None], lpage]
        rows = jnp.arange(PAGE_SIZE)
        hsel = heads[None, :, None, None, None, None]
        nsel = kt * MOBA_BLOCK
        k_sel = cache_k[phys[..., None], rows, hsel].reshape(DB, H, T, nsel, dh)
        v_sel = cache_v[phys[..., None], rows, hsel].reshape(DB, H, T, nsel, dh)
        sel_pos = (lpage[..., None] * PAGE_SIZE + rows).reshape(DB, H, T, nsel)
        logits.append(jnp.einsum('bthd,bhtkd->bhtk', qf, k_sel) * scale
                      + head_bias(q_pos[:, None] - sel_pos, heads[None, :, None, None], table))
        values.append(v_sel)
        specs.append('bhtk,bhtkd->bthd')
    if r > 0:
        own = page_table[:, nfp * ppb:]
        k_own = cache_k[own].reshape(DB, r, H, dh)
        v_own = cache_v[own].reshape(DB, r, H, dh)
        own_pos = nfp * MOBA_BLOCK + jnp.arange(r)
        logits.append(jnp.einsum('bthd,bkhd->bhtk', qf, k_own) * scale
                      + head_bias(q_pos[:, None] - own_pos[None, :], heads[:, None, None], table))
        values.append(v_own)
        specs.append('bhtk,bkhd->bthd')
    l_new = (jnp.einsum('bthd,bkhd->bhtk', qf, k) * scale
             + head_bias(q_pos[:, None] - q_pos[None, :], heads[:, None, None], table))
    logits.append(jnp.where(q_pos[None, :] <= q_pos[:, None], l_new, -jnp.inf))
    values.append(v)
    specs.append('bhtk,bkhd->bthd')
    p = jax.nn.softmax(jnp.concatenate(logits, -1), axis=-1)
    bounds = np.cumsum([0] + [t.shape[-1] for t in logits]).tolist()
    out = sum(jnp.einsum(spec, p[..., bounds[j]:bounds[j + 1]], val)
              for j, (spec, val) in enumerate(zip(specs, values)))
    return out.reshape(DB, T, H * dh)


def causal_conv(x, state, w, b):
    xp = jnp.concatenate([state.astype(x.dtype), x], axis=1)
    L = x.shape[1]
    y = sum(xp[:, j:j + L] * w[j] for j in range(CONV_WIDTH)) + b
    return jax.nn.silu(y), xp[:, -(CONV_WIDTH - 1):]


def mlstm_chunk(state, q, k, v, i_t, logf):
    C, n, m = state
    L = q.shape[2]
    b = jnp.cumsum(logf, axis=-1)
    causal = jnp.tril(jnp.ones((L, L), bool))
    D = jnp.where(causal, b[..., :, None] - b[..., None, :] + i_t[..., None, :], -jnp.inf)
    inter = b + m[..., None]
    m_t = jnp.maximum(inter, jnp.max(D, -1))
    w_inter = jnp.exp(inter - m_t)
    s = jnp.einsum('bhtd,bhsd->bhts', q, k) * jnp.exp(D - m_t[..., None])
    num = w_inter[..., None] * jnp.einsum('bhvk,bhtk->bhtv', C, q) + jnp.einsum('bhts,bhsv->bhtv', s, v)
    den = w_inter * jnp.einsum('bhk,bhtk->bht', n, q) + jnp.sum(s, -1)
    h = num / jnp.maximum(jnp.abs(den), jnp.exp(-m_t))[..., None]
    g = b[..., -1:] - b + i_t
    m_new = jnp.maximum(b[..., -1] + m, jnp.max(g, -1))
    wc = jnp.exp(b[..., -1] + m - m_new)
    ws = jnp.exp(g - m_new[..., None])
    C_new = wc[..., None, None] * C + jnp.einsum('bhs,bhsv,bhsk->bhvk', ws, v, k)
    n_new = wc[..., None] * n + jnp.einsum('bhs,bhsk->bhk', ws, k)
    return (C_new, n_new, m_new), h


def mlstm_mixer(qk_raw, vm, om, ig, fg, C, n, m, conv_state, conv_w, conv_b, b_gate):
    B, L, _ = vm.shape
    qk, new_conv = causal_conv(qk_raw, conv_state, conv_w, conv_b)
    heads = lambda t: t.reshape(B, L, N_M_HEADS, M_HEAD_DIM).transpose(0, 2, 1, 3).astype(jnp.float32)
    q = heads(qk[..., :M_WIDTH])
    k = heads(qk[..., M_WIDTH:]) * (M_HEAD_DIM ** -0.5)
    v = heads(vm)
    i_t = (ig + b_gate[:N_M_HEADS]).astype(jnp.float32).transpose(0, 2, 1)
    logf = jax.nn.log_sigmoid((fg + b_gate[N_M_HEADS:]).astype(jnp.float32)).transpose(0, 2, 1)
    chunk = M_CHUNK if L % M_CHUNK == 0 else L
    nc = L // chunk

    def to_chunks(t):
        return jnp.moveaxis(t.reshape(t.shape[:2] + (nc, chunk) + t.shape[3:]), 2, 0)

    init = (C.astype(jnp.float32), n.astype(jnp.float32), m.astype(jnp.float32))
    (C, n, m), h = lax.scan(lambda st, xs: mlstm_chunk(st, *xs), init,
                            tuple(to_chunks(t) for t in (q, k, v, i_t, logf)))
    h = jnp.moveaxis(h, 0, 2).reshape(B, N_M_HEADS, L, M_HEAD_DIM).transpose(0, 2, 1, 3).reshape(B, L, M_WIDTH)
    return jax.nn.sigmoid(om.astype(jnp.float32)) * h, C, n, m, new_conv


def peer_rows(h, w_query, sub_keys, expert_u, expert_v):
    R = h.shape[0]
    qry = (h @ w_query).astype(jnp.float32).reshape(R, PEER_HEADS, 2, PEER_KEY_DIM // 2)
    half = jnp.einsum('rpsd,snd->rpsn', qry, sub_keys.astype(jnp.float32))
    half_s, half_i = lax.top_k(half, PEER_TOPK)
    n_cand = PEER_TOPK * PEER_TOPK
    cand_s = (half_s[:, :, 0, :, None] + half_s[:, :, 1, None, :]).reshape(R, PEER_HEADS, n_cand)
    cand_i = (half_i[:, :, 0, :, None] * N_KEYS + half_i[:, :, 1, None, :]).reshape(R, PEER_HEADS, n_cand)
    best_s, best_j = lax.top_k(cand_s, PEER_TOPK)
    idx = jnp.take_along_axis(cand_i, best_j, axis=-1)
    g = jax.nn.softmax(best_s, axis=-1)
    act = jax.nn.gelu(jnp.einsum('rd,rpkd->rpk', h, expert_u[idx]).astype(jnp.float32))
    return jnp.einsum('rpk,rpkd->rd', (g * act).astype(h.dtype), expert_v[idx])


def peer(h, w_query, sub_keys, expert_u, expert_v):
    R, Dm = h.shape
    rb = min(PEER_ROWS, R)
    nrb = -(-R // rb)
    hp = jnp.pad(h, ((0, nrb * rb - R), (0, 0))).reshape(nrb, rb, Dm)
    out = lax.map(lambda t: peer_rows(t, w_query, sub_keys, expert_u, expert_v), hp)
    return out.reshape(nrb * rb, Dm)[:R]


def setup_inputs(seed: int = 0) -> dict:
    key = jax.random.key(seed)
    ks = iter(jax.random.split(key, 40))
    f32 = jnp.float32

    def nrm(shape, s=1.0):
        return s * jax.random.normal(next(ks), shape, f32)

    n_pages = PAST_LEN // PAGE_SIZE
    used = DEC_BATCH * n_pages
    n_phys = used + max(1, used // 4)
    page_table = jax.random.permutation(next(ks), n_phys)[:used].reshape(DEC_BATCH, n_pages).astype(jnp.int32)
    cache_shape = (DEPTH, n_phys, PAGE_SIZE, N_ATTN_HEADS, ATTN_HEAD_DIM)
    return {
        'x_prompt': nrm((BATCH, SEQ, D_MODEL)),
        'x_sample': nrm((DEC_BATCH, DEC_SEQ, D_MODEL)),
        'cache_k': nrm(cache_shape),
        'cache_v': nrm(cache_shape),
        'page_table': page_table,
        'state_C': nrm((DEPTH, DEC_BATCH, N_M_HEADS, M_HEAD_DIM, M_HEAD_DIM)),
        'state_n': nrm((DEPTH, DEC_BATCH, N_M_HEADS, M_HEAD_DIM)),
        'state_m': nrm((DEPTH, DEC_BATCH, N_M_HEADS), 0.5),
        'state_conv': nrm((DEPTH, DEC_BATCH, CONV_WIDTH - 1, 2 * M_WIDTH)),
        'c_prompt': nrm((BATCH, D_MODEL)),
        'c_sample': nrm((DEC_BATCH, D_MODEL)),
        'rel_bias_table': nrm((N_BUCKETS, N_ATTN_HEADS), 0.3),
        'w_ada': nrm((DEPTH, D_MODEL, 6 * D_MODEL), D_MODEL ** -0.5),
        'b_ada': nrm((DEPTH, 6 * D_MODEL), 0.02),
        'w_in': nrm((DEPTH, D_MODEL, PROJ_WIDTH), D_MODEL ** -0.5),
        'b_gate': jnp.concatenate([nrm((DEPTH, N_M_HEADS), 0.1), 3.0 + nrm((DEPTH, N_M_HEADS), 0.5)], -1),
        'conv_w': nrm((DEPTH, CONV_WIDTH, 2 * M_WIDTH), 0.5),
        'conv_b': nrm((DEPTH, 2 * M_WIDTH), 0.02),
        'beta_attn': 1.0 + nrm((DEPTH, ATTN_WIDTH), 0.05),
        'beta_mlstm': 1.0 + nrm((DEPTH, M_WIDTH), 0.05),
        'w_out': nrm((DEPTH, MIX_WIDTH, D_MODEL), BETA * MIX_WIDTH ** -0.5),
        'ln1_g': 1.0 + nrm((DEPTH, D_MODEL), 0.05),
        'ln1_b': nrm((DEPTH, D_MODEL), 0.02),
        'w_query': nrm((DEPTH, D_MODEL, PEER_HEADS * PEER_KEY_DIM), D_MODEL ** -0.5),
        'sub_keys': nrm((DEPTH, 2, N_KEYS, PEER_KEY_DIM // 2), (PEER_KEY_DIM // 2) ** -0.5),
        'expert_u': nrm((DEPTH, N_EXPERTS, D_MODEL), D_MODEL ** -0.5),
        'expert_v': nrm((DEPTH, N_EXPERTS, D_MODEL), BETA * PEER_HEADS ** -0.5),
        'ln2_g': 1.0 + nrm((DEPTH, D_MODEL), 0.05),
        'ln2_b': nrm((DEPTH, D_MODEL), 0.02),
    }


def reference(x_prompt, x_sample, cache_k, cache_v, page_table, state_C, state_n, state_m, state_conv,
              c_prompt, c_sample, rel_bias_table, w_ada, b_ada, w_in, b_gate, conv_w, conv_b,
              beta_attn, beta_mlstm, w_out, ln1_g, ln1_b, w_query, sub_keys, expert_u, expert_v,
              ln2_g, ln2_b):
    f32 = jnp.float32

    def layer(l, x, c, mixer):
        sh1, sc1, g1, sh2, sc2, g2 = jnp.split((jax.nn.silu(c) @ w_ada[l] + b_ada[l])[:, None, :], 6, axis=-1)
        h = standardize(x) * (1.0 + sc1) + sh1
        parts = jnp.split(h @ w_in[l], PROJ_OFFSETS, axis=-1)
        attn, mem, new_state = mixer(*parts)
        mixed = jnp.concatenate([attn * beta_attn[l], mem * beta_mlstm[l]], axis=-1).astype(x.dtype)
        x = layer_norm(ALPHA * x + g1 * (mixed @ w_out[l]), ln1_g[l], ln1_b[l])
        h2 = (standardize(x) * (1.0 + sc2) + sh2).astype(x.dtype)
        f = peer(h2.reshape(-1, D_MODEL), w_query[l], sub_keys[l], expert_u[l], expert_v[l]).reshape(x.shape)
        x = layer_norm(ALPHA * x + g2 * f, ln2_g[l], ln2_b[l])
        return x, new_state

    def prompt_mixer(l):
        def mix(aq, ak, av, mq, mk, mv, mo, mi, mf):
            B, S, _ = aq.shape
            split_heads = lambda t: t.reshape(B, S, N_ATTN_HEADS, ATTN_HEAD_DIM)
            k, v = split_heads(ak), split_heads(av)
            attn = moba_prompt(split_heads(aq), k, v, rel_bias_table)
            mem, C, n, m, conv = mlstm_mixer(
                jnp.concatenate([mq, mk], -1), mv, mo, mi, mf,
                jnp.zeros((B, N_M_HEADS, M_HEAD_DIM, M_HEAD_DIM), f32), jnp.zeros((B, N_M_HEADS, M_HEAD_DIM), f32),
                jnp.zeros((B, N_M_HEADS), f32), jnp.zeros((B, CONV_WIDTH - 1, 2 * M_WIDTH), mq.dtype),
                conv_w[l], conv_b[l], b_gate[l])
            return attn, mem, (k, v, C, n, m, conv)
        return mix

    def sample_mixer(l):
        def mix(aq, ak, av, mq, mk, mv, mo, mi, mf):
            DB, T, _ = aq.shape
            split_heads = lambda t: t.reshape(DB, T, N_ATTN_HEADS, ATTN_HEAD_DIM)
            k, v = split_heads(ak), split_heads(av)
            attn = moba_sample(split_heads(aq), k, v, cache_k[l], cache_v[l], page_table, rel_bias_table)
            mem, C, n, m, conv = mlstm_mixer(
                jnp.concatenate([mq, mk], -1), mv, mo, mi, mf,
                state_C[l], state_n[l], state_m[l], state_conv[l], conv_w[l], conv_b[l], b_gate[l])
            return attn, mem, (k, v, C, n, m, conv)
        return mix

    xp, xs = x_prompt, x_sample
    new_p, new_s = [], []
    for l in range(DEPTH):
        xp, st = layer(l, xp, c_prompt, prompt_mixer(l))
        new_p.append(st)
        xs, st = layer(l, xs, c_sample, sample_mixer(l))
        new_s.append(st)

    def stack(states, i):
        return jnp.stack([s[i] for s in states])

    return (xp, xs,
            stack(new_p, 0), stack(new_p, 1), stack(new_p, 2), stack(new_p, 3), stack(new_p, 4), stack(new_p, 5),
            stack(new_s, 0), stack(new_s, 1), stack(new_s, 2), stack(new_s, 3), stack(new_s, 4), stack(new_s, 5))
```

```python
import functools
import math

import numpy as np
import jax
import jax.numpy as jnp
from jax import lax
from jax.experimental import pallas as pl
from jax.experimental.pallas import tpu as pltpu

F32 = jnp.float32
BF16 = jnp.bfloat16
NEG_INF = float("-inf")
HIGHEST = lax.Precision.HIGHEST

N_ATTN_HEADS = 8
ATTN_HEAD_DIM = 64
ATTN_WIDTH = N_ATTN_HEADS * ATTN_HEAD_DIM
MOBA_BLOCK = 256
MOBA_TOPK = 3
PAGE_SIZE = 128
N_BUCKETS = 32
MAX_DISTANCE = 128
N_M_HEADS = 4
M_HEAD_DIM = 128
M_WIDTH = N_M_HEADS * M_HEAD_DIM
CONV_WIDTH = 4
M_CHUNK = 128
N_KEYS = 128
PEER_HEADS = 8
PEER_KEY_DIM = 256
PEER_TOPK = 16
LN_EPS = 1e-5
DEPTH = 1
ALPHA = (2.0 * DEPTH) ** 0.25

LANES = 128
SUBLANES = 8
VMEM_LIMIT = 56 * 1024 * 1024


def _cparams(sem):
    return pltpu.CompilerParams(dimension_semantics=sem, vmem_limit_bytes=VMEM_LIMIT)


def _bucket_thresholds():
    max_exact = N_BUCKETS // 2
    d = np.arange(0, MAX_DISTANCE + 1)
    far = max_exact + (np.log(np.maximum(d, 1) / max_exact) / math.log(MAX_DISTANCE / max_exact)
                       * (N_BUCKETS - max_exact)).astype(np.int64)
    bucket = np.where(d < max_exact, d, np.minimum(far, N_BUCKETS - 1))
    assert np.all(np.diff(bucket) >= 0) and bucket[-1] == N_BUCKETS - 1
    return tuple(int(np.argmax(bucket >= k)) for k in range(max_exact + 1, N_BUCKETS))


_BUCKET_THRESHOLDS = _bucket_thresholds()


def _t5_bucket(dist):
    max_exact = N_BUCKETS // 2
    far = jnp.full(dist.shape, max_exact, jnp.int32)
    for t in _BUCKET_THRESHOLDS:
        far = far + (dist >= t).astype(jnp.int32)
    return jnp.where(dist < max_exact, dist, far)


def _bias_from_bucket(bucket, tbl_ref, h):
    out = jnp.zeros(bucket.shape, F32)
    for j in range(N_BUCKETS):
        out = jnp.where(bucket == j, tbl_ref[j, h], out)
    return out


def _standardize(x):
    mu = jnp.mean(x, axis=-1, keepdims=True)
    xc = x - mu
    var = jnp.mean(xc * xc, axis=-1, keepdims=True)
    return xc * lax.rsqrt(var + LN_EPS)


def _sigmoid(x):
    return 1.0 / (1.0 + jnp.exp(-x))


def _log_sigmoid(x):
    return jnp.minimum(x, 0.0) - jnp.log1p(jnp.exp(-jnp.abs(x)))


def _gelu_tanh(x):
    c = math.sqrt(2.0 / math.pi)
    return 0.5 * x * (1.0 + jnp.tanh(c * (x + 0.044715 * (x * x * x))))


def _dot_nt(a, b, **kw):
    return lax.dot_general(a, b, (((1,), (1,)), ((), ())), preferred_element_type=F32, **kw)


def _dot(a, b, **kw):
    return jnp.dot(a, b, preferred_element_type=F32, **kw)


def _ada_kernel(c_ref, w_ref, b_ref, o_ref):
    c = c_ref[...]
    s = c * _sigmoid(c)
    o_ref[...] = _dot(s, w_ref[...], precision=HIGHEST) + b_ref[...]


def _ada(c_all, w_ada, b_ada):
    n, d = c_all.shape
    n_out = w_ada.shape[1]
    tn = 1024
    return pl.pallas_call(
        _ada_kernel,
        grid=(n_out // tn,),
        in_specs=[pl.BlockSpec((n, d), lambda j: (0, 0)),
                  pl.BlockSpec((d, tn), lambda j: (0, j)),
                  pl.BlockSpec((1, tn), lambda j: (0, j))],
        out_specs=pl.BlockSpec((n, tn), lambda j: (0, j)),
        out_shape=jax.ShapeDtypeStruct((n, n_out), F32),
        compiler_params=_cparams(("parallel",)),
        name="ada_mod",
    )(c_all, w_ada, b_ada.reshape(1, n_out))


_PROJ_GROUPS = (ATTN_WIDTH, ATTN_WIDTH, ATTN_WIDTH, 2 * M_WIDTH, M_WIDTH, M_WIDTH, LANES)
_PROJ_OFFS = tuple(int(v) for v in np.cumsum((0,) + _PROJ_GROUPS))


def _inproj_kernel(x_ref, sc_ref, sh_ref, w_ref, *o_refs):
    h = _standardize(x_ref[...]) * sc_ref[0] + sh_ref[0]
    hb = h.astype(BF16)
    for g, o_ref in enumerate(o_refs):
        o_ref[...] = _dot(hb, w_ref[:, _PROJ_OFFS[g]:_PROJ_OFFS[g + 1]])


def _inproj(x2d, sc3, sh3, w_in_b, tm, rows_per_mod):
    r, d = x2d.shape
    m = sc3.shape[1]
    if m == 1:
        mod_map = lambda i: ((i * tm) // rows_per_mod, 0, 0)
    else:
        mod_map = lambda i: (i, 0, 0)
    return pl.pallas_call(
        _inproj_kernel,
        grid=(r // tm,),
        in_specs=[pl.BlockSpec((tm, d), lambda i: (i, 0)),
                  pl.BlockSpec((1, m, d), mod_map),
                  pl.BlockSpec((1, m, d), mod_map),
                  pl.BlockSpec(w_in_b.shape, lambda i: (0, 0))],
        out_specs=[pl.BlockSpec((tm, g), lambda i: (i, 0)) for g in _PROJ_GROUPS],
        out_shape=[jax.ShapeDtypeStruct((r, g), F32) for g in _PROJ_GROUPS],
        compiler_params=_cparams(("parallel",)),
        name="inproj",
    )(x2d, sc3, sh3, w_in_b)


def _bias_tiles_kernel(tbl_ref, o_ref):
    h = pl.program_id(0)
    row = lax.broadcasted_iota(jnp.int32, (MOBA_BLOCK, MOBA_BLOCK), 0)
    col = lax.broadcasted_iota(jnp.int32, (MOBA_BLOCK, MOBA_BLOCK), 1)
    for t in range(2):
        dist = jnp.maximum(row - col + t * MOBA_BLOCK, 0)
        o_ref[0, t] = _bias_from_bucket(_t5_bucket(dist), tbl_ref, h)


def _bias_tiles(table):
    n_h = table.shape[1]
    return pl.pallas_call(
        _bias_tiles_kernel,
        grid=(n_h,),
        in_specs=[pl.BlockSpec(memory_space=pltpu.SMEM)],
        out_specs=pl.BlockSpec((1, 2, MOBA_BLOCK, MOBA_BLOCK), lambda h: (h, 0, 0, 0)),
        out_shape=jax.ShapeDtypeStruct((n_h, 2, MOBA_BLOCK, MOBA_BLOCK), F32),
        compiler_params=_cparams(("parallel",)),
        name="moba_bias_tiles",
    )(table)


def _moba_prompt_kernel(tbl_ref, q_ref, k_ref, v_ref, bias_ref, o_ref, *, n_blocks):
    h = pl.program_id(1)
    ob = pl.program_id(2)
    blk = MOBA_BLOCK
    scale = ATTN_HEAD_DIM ** -0.5
    q = q_ref[0, 0]
    qb = q.astype(BF16)

    means = [jnp.sum(k_ref[0, 0, n * blk:(n + 1) * blk, :], axis=0, keepdims=True) * (1.0 / blk)
             for n in range(n_blocks)]
    kmean = jnp.concatenate(means + [jnp.zeros((LANES - n_blocks, ATTN_HEAD_DIM), F32)], axis=0)
    gate = _dot_nt(q, kmean, precision=HIGHEST)
    lane = lax.broadcasted_iota(jnp.int32, (blk, LANES), 1)
    g = jnp.where(lane < ob, gate, NEG_INF)
    sel = jnp.zeros((blk, LANES), F32)
    for _ in range(MOBA_TOPK):
        mx = jnp.max(g, axis=-1, keepdims=True)
        idx = jnp.min(jnp.where(g == mx, lane, LANES), axis=-1, keepdims=True)
        hit = lane == idx
        sel = jnp.where(hit, 1.0, sel)
        g = jnp.where(hit, NEG_INF, g)
    sel = jnp.where(lane < ob, sel, 0.0)

    start = pl.multiple_of(ob * blk, blk)
    k_own = k_ref[0, 0, pl.ds(start, blk), :].astype(BF16)
    v_own = v_ref[0, 0, pl.ds(start, blk), :].astype(BF16)
    row = lax.broadcasted_iota(jnp.int32, (blk, blk), 0)
    col = lax.broadcasted_iota(jnp.int32, (blk, blk), 1)
    s = _dot_nt(qb, k_own) * scale + bias_ref[0, 0]
    s = jnp.where(col <= row, s, NEG_INF)
    m0 = jnp.max(s, axis=-1, keepdims=True)
    p = jnp.exp(s - m0)
    l0 = jnp.sum(p, axis=-1, keepdims=True)
    acc0 = _dot(p.astype(BF16), v_own)
    c_far = tbl_ref[N_BUCKETS - 1, h]

    def body(n, carry):
        m, l, acc = carry
        st = pl.multiple_of(n * blk, blk)
        k_n = k_ref[0, 0, pl.ds(st, blk), :].astype(BF16)
        v_n = v_ref[0, 0, pl.ds(st, blk), :].astype(BF16)
        bias = jnp.where(n == ob - 1, bias_ref[0, 1], c_far)
        s = _dot_nt(qb, k_n) * scale + bias
        picked = jnp.sum(jnp.where(lane == n, sel, 0.0), axis=-1, keepdims=True) > 0.5
        s = jnp.where(picked, s, NEG_INF)
        m_new = jnp.maximum(m, jnp.max(s, axis=-1, keepdims=True))
        a = jnp.exp(m - m_new)
        p = jnp.exp(s - m_new)
        l = a * l + jnp.sum(p, axis=-1, keepdims=True)
        acc = a * acc + _dot(p.astype(BF16), v_n)
        return m_new, l, acc

    _, l, acc = lax.fori_loop(0, ob, body, (m0, l0, acc0))
    o_ref[0, 0] = acc / l


def _moba_prompt(q, k, v, table, bias_tiles):
    b, n_h, s, dh = q.shape
    blk = MOBA_BLOCK
    n_blocks = s // blk
    assert s % blk == 0 and n_blocks >= MOBA_TOPK and n_blocks <= LANES
    return pl.pallas_call(
        functools.partial(_moba_prompt_kernel, n_blocks=n_blocks),
        grid=(b, n_h, n_blocks),
        in_specs=[pl.BlockSpec(memory_space=pltpu.SMEM),
                  pl.BlockSpec((1, 1, blk, dh), lambda i, h, j: (i, h, j, 0)),
                  pl.BlockSpec((1, 1, s, dh), lambda i, h, j: (i, h, 0, 0)),
                  pl.BlockSpec((1, 1, s, dh), lambda i, h, j: (i, h, 0, 0)),
                  pl.BlockSpec((1, 2, blk, blk), lambda i, h, j: (h, 0, 0, 0))],
        out_specs=pl.BlockSpec((1, 1, blk, dh), lambda i, h, j: (i, h, j, 0)),
        out_shape=jax.ShapeDtypeStruct((b, n_h, s, dh), F32),
        compiler_params=_cparams(("parallel", "parallel", "parallel")),
        name="moba_prompt",
    )(table, q, k, v, bias_tiles)


_CONV_PAD = SUBLANES


def _cumsum_lanes(x):
    lane = lax.broadcasted_iota(jnp.int32, x.shape, 1)
    k = 1
    while k < x.shape[-1]:
        x = x + jnp.where(lane >= k, pltpu.roll(x, k, 1), 0.0)
        k *= 2
    return x


def _mlstm_prompt_kernel(q_ref, k_ref, v_ref, o_ref, g_ref, cw_ref, cb_ref, bg_ref,
                         mem_ref, c_out, n_out, m_out, xq_s, xk_s, c_s, n_s, m_s):
    ci = pl.program_id(1)
    L = M_CHUNK
    dh = M_HEAD_DIM
    pad = _CONV_PAD
    hist = CONV_WIDTH - 1

    @pl.when(ci == 0)
    def _():
        xq_s[0:pad, :] = jnp.zeros((pad, M_WIDTH), F32)
        xk_s[0:pad, :] = jnp.zeros((pad, M_WIDTH), F32)
        c_s[...] = jnp.zeros(c_s.shape, F32)
        n_s[...] = jnp.zeros(n_s.shape, F32)
        m_s[...] = jnp.zeros(m_s.shape, F32)

    xq_s[pad:pad + L, :] = q_ref[0]
    xk_s[pad:pad + L, :] = k_ref[0]

    def conv(x_s, col0):
        y = cb_ref[:, col0:col0 + M_WIDTH]
        for j in range(CONV_WIDTH):
            y = y + x_s[pad - hist + j:pad - hist + j + L, :] * cw_ref[j:j + 1, col0:col0 + M_WIDTH]
        return y * _sigmoid(y)

    qc = conv(xq_s, 0)
    kc = conv(xk_s, M_WIDTH) * (dh ** -0.5)
    tq = xq_s[pad + L - hist:pad + L, :]
    tk = xk_s[pad + L - hist:pad + L, :]
    xq_s[pad - hist:pad, :] = tq
    xk_s[pad - hist:pad, :] = tk

    gt = g_ref[0].T
    pre = gt[0:2 * N_M_HEADS, :] + bg_ref[...]
    rsel = lax.broadcasted_iota(jnp.int32, pre.shape, 0) < N_M_HEADS
    cum = _cumsum_lanes(jnp.where(rsel, 0.0, _log_sigmoid(pre)))
    r8 = jnp.where(rsel, pre, cum)
    t8 = jnp.concatenate([r8, jnp.zeros((LANES - 2 * N_M_HEADS, L), F32)], axis=0).T

    row = lax.broadcasted_iota(jnp.int32, (L, L), 0)
    col = lax.broadcasted_iota(jnp.int32, (L, L), 1)
    causal = col <= row
    for h in range(N_M_HEADS):
        sl = slice(h * dh, (h + 1) * dh)
        q = qc[:, sl]
        k = kc[:, sl]
        v = v_ref[0, :, sl]
        i_row = r8[h:h + 1, :]
        b_row = r8[N_M_HEADS + h:N_M_HEADS + h + 1, :]
        i_col = t8[:, h:h + 1]
        b_col = t8[:, N_M_HEADS + h:N_M_HEADS + h + 1]
        c_prev = c_s[h]
        n_prev = n_s[h:h + 1, :]
        m_prev = m_s[h:h + 1, 0:1]

        d = jnp.where(causal, b_col - b_row + i_row, NEG_INF)
        inter = b_col + m_prev
        m_t = jnp.maximum(inter, jnp.max(d, axis=-1, keepdims=True))
        w_inter = jnp.exp(inter - m_t)
        qb = q.astype(BF16)
        kb = k.astype(BF16)
        s = _dot_nt(qb, kb) * jnp.exp(d - m_t)
        num = w_inter * _dot_nt(qb, c_prev.astype(BF16)) + _dot(s.astype(BF16), v.astype(BF16))
        den = w_inter * jnp.sum(q * n_prev, axis=-1, keepdims=True) + jnp.sum(s, axis=-1, keepdims=True)
        hh = num / jnp.maximum(jnp.abs(den), jnp.exp(-m_t))
        mem_ref[0, :, sl] = _sigmoid(o_ref[0, :, sl]) * hh

        b_last = b_row[:, L - 1:L]
        g_row = b_last - b_row + i_row
        g_col = b_last - b_col + i_col
        m_new = jnp.maximum(b_last + m_prev, jnp.max(g_row, axis=-1, keepdims=True))
        wc = jnp.exp(b_last + m_prev - m_new)
        ws = jnp.exp(g_col - m_new)
        wv_t = (ws * v).T.astype(BF16)
        c_s[h] = wc * c_prev + _dot(wv_t, kb)
        n_s[h:h + 1, :] = wc * n_prev + jnp.sum(ws * k, axis=0, keepdims=True)
        m_s[h:h + 1, :] = jnp.broadcast_to(m_new, (1, LANES))

    @pl.when(ci == pl.num_programs(1) - 1)
    def _():
        c_out[0] = c_s[...]
        n_out[0] = n_s[0:N_M_HEADS, :]
        m_out[0] = m_s[...]


def _mlstm_prompt(mqk, mv, mo, gates, conv_w, conv_b, bg8):
    b, s, _ = mv.shape
    L = M_CHUNK
    nc = s // L
    assert s % L == 0
    return pl.pallas_call(
        _mlstm_prompt_kernel,
        grid=(b, nc),
        in_specs=[pl.BlockSpec((1, L, M_WIDTH), lambda i, c: (i, c, 0)),
                  pl.BlockSpec((1, L, M_WIDTH), lambda i, c: (i, c, 1)),
                  pl.BlockSpec((1, L, M_WIDTH), lambda i, c: (i, c, 0)),
                  pl.BlockSpec((1, L, M_WIDTH), lambda i, c: (i, c, 0)),
                  pl.BlockSpec((1, L, LANES), lambda i, c: (i, c, 0)),
                  pl.BlockSpec((CONV_WIDTH, 2 * M_WIDTH), lambda i, c: (0, 0)),
                  pl.BlockSpec((1, 2 * M_WIDTH), lambda i, c: (0, 0)),
                  pl.BlockSpec((2 * N_M_HEADS, LANES), lambda i, c: (0, 0))],
        out_specs=[pl.BlockSpec((1, L, M_WIDTH), lambda i, c: (i, c, 0)),
                   pl.BlockSpec((1, N_M_HEADS, M_HEAD_DIM, M_HEAD_DIM), lambda i, c: (i, 0, 0, 0)),
                   pl.BlockSpec((1, N_M_HEADS, M_HEAD_DIM), lambda i, c: (i, 0, 0)),
                   pl.BlockSpec((1, SUBLANES, LANES), lambda i, c: (i, 0, 0))],
        out_shape=[jax.ShapeDtypeStruct((b, s, M_WIDTH), F32),
                   jax.ShapeDtypeStruct((b, N_M_HEADS, M_HEAD_DIM, M_HEAD_DIM), F32),
                   jax.ShapeDtypeStruct((b, N_M_HEADS, M_HEAD_DIM), F32),
                   jax.ShapeDtypeStruct((b, SUBLANES, LANES), F32)],
        scratch_shapes=[pltpu.VMEM((_CONV_PAD + L, M_WIDTH), F32),
                        pltpu.VMEM((_CONV_PAD + L, M_WIDTH), F32),
                        pltpu.VMEM((N_M_HEADS, M_HEAD_DIM, M_HEAD_DIM), F32),
                        pltpu.VMEM((SUBLANES, M_HEAD_DIM), F32),
                        pltpu.VMEM((SUBLANES, LANES), F32)],
        compiler_params=_cparams(("parallel", "arbitrary")),
        name="mlstm_prompt",
    )(mqk, mqk, mv, mo, gates, conv_w, conv_b, bg8)


def _outproj_kernel(a_ref, m_ref, x_ref, beta_ref, w_ref, g1_ref, sc_ref, sh_ref, lg_ref, lb_ref,
                    x1_ref, h2_ref):
    mixed = jnp.concatenate([a_ref[...], m_ref[...]], axis=-1) * beta_ref[...]
    y = _dot(mixed.astype(BF16), w_ref[...])
    z = ALPHA * x_ref[...] + g1_ref[0] * y
    x1 = _standardize(z) * lg_ref[...] + lb_ref[...]
    x1_ref[...] = x1
    h2_ref[...] = (_standardize(x1) * sc_ref[0] + sh_ref[0]).astype(h2_ref.dtype)


def _outproj(attn, mem, x2d, beta, w_out_b, g1, sc2, sh2, ln_g, ln_b, tm, rows_per_mod):
    r, d = x2d.shape
    m = g1.shape[1]
    if m == 1:
        mod_map = lambda i: ((i * tm) // rows_per_mod, 0, 0)
    else:
        mod_map = lambda i: (i, 0, 0)
    vec = pl.BlockSpec((1, d), lambda i: (0, 0))
    mod = pl.BlockSpec((1, m, d), mod_map)
    return pl.pallas_call(
        _outproj_kernel,
        grid=(r // tm,),
        in_specs=[pl.BlockSpec((tm, attn.shape[1]), lambda i: (i, 0)),
                  pl.BlockSpec((tm, mem.shape[1]), lambda i: (i, 0)),
                  pl.BlockSpec((tm, d), lambda i: (i, 0)),
                  vec,
                  pl.BlockSpec(w_out_b.shape, lambda i: (0, 0)),
                  mod, mod, mod, vec, vec],
        out_specs=[pl.BlockSpec((tm, d), lambda i: (i, 0)),
                   pl.BlockSpec((tm, d), lambda i: (i, 0))],
        out_shape=[jax.ShapeDtypeStruct((r, d), F32), jax.ShapeDtypeStruct((r, d), BF16)],
        compiler_params=_cparams(("parallel",)),
        name="outproj",
    )(attn, mem, x2d, beta, w_out_b, g1, sc2, sh2, ln_g, ln_b)


def _oddeven_merge_sort_pairs(n):
    pairs = []

    def merge(lo, m, r):
        step = r * 2
        if step < m:
            merge(lo, m, step)
            merge(lo + r, m, step)
            for i in range(lo + r, lo + m - r, step):
                pairs.append((i, i + r))
        else:
            pairs.append((lo, lo + r))

    def sort(lo, m):
        if m > 1:
            h = m // 2
            sort(lo, h)
            sort(lo + h, h)
            merge(lo, m, 1)

    sort(0, n)
    return tuple(pairs)


_SORT16 = _oddeven_merge_sort_pairs(PEER_TOPK)


def _vmax(a, b):
    if a is None:
        return b
    if b is None:
        return a
    return jnp.maximum(a, b)


def _cmpx(v, i, j):
    a, b = v[i], v[j]
    if b is None:
        return
    if a is None:
        v[i], v[j] = b, None
        return
    v[i], v[j] = jnp.maximum(a, b), jnp.minimum(a, b)


def _bitonic_to_desc(v):
    n = len(v)
    d = n // 2
    while d >= 1:
        for i in range(n):
            if (i & d) == 0:
                _cmpx(v, i, i + d)
        d //= 2
    return v


def _merge_top(x, y):
    n = len(x)
    return _bitonic_to_desc([_vmax(x[i], y[n - 1 - i]) for i in range(n)])


def _top16_desc(sc):
    groups = sc.shape[0] // SUBLANES
    assert groups == PEER_TOPK
    v = [sc[g * SUBLANES:(g + 1) * SUBLANES, :] for g in range(groups)]
    for i, j in _SORT16:
        _cmpx(v, i, j)
    shift = SUBLANES // 2
    while shift >= 1:
        partner = [pltpu.roll(a, shift, 0) for a in v]
        v = _merge_top(v, partner)
        shift //= 2
    return v


def _candidate_lists(a, b):
    k = PEER_TOPK
    lists = []
    for i in range(4):
        n = k // (i + 1)
        lists.append([a[i] + b[j] for j in range(n)])
    for j in range(3):
        n = k // (j + 1)
        col = [a[i] + b[j] for i in range(4, n)]
        if col:
            lists.append(col)
    return [l + [None] * (k - len(l)) for l in lists]


def _peer_route_kernel(h_ref, wq_ref, sk_ref, s0_ref, s1_ref, a_ref, b_ref, t_ref, qt_s, top_s):
    tm = h_ref.shape[0]
    kd = PEER_KEY_DIM // 2
    qt_s[...] = _dot_nt(wq_ref[...], h_ref[...])

    def head(p, carry):
        for s, dst in ((0, s0_ref), (1, s1_ref)):
            r0 = pl.multiple_of((2 * p + s) * kd, kd)
            sc = _dot(sk_ref[s], qt_s[pl.ds(r0, kd), :], precision=HIGHEST)
            dst[p] = sc
            srt = _top16_desc(sc)
            for r in range(PEER_TOPK):
                top_s[p, s, r] = srt[r]
        return carry

    lax.fori_loop(0, PEER_HEADS, head, 0)
    sub = lax.broadcasted_iota(jnp.int32, (SUBLANES, tm), 0)

    def on_sublanes(s, r):
        out = top_s[0, s, r]
        for p in range(1, PEER_HEADS):
            out = jnp.where(sub == p, top_s[p, s, r], out)
        return out

    top = [[on_sublanes(s, r) for r in range(PEER_TOPK)] for s in range(2)]
    lists = _candidate_lists(top[0], top[1])
    best = lists[0]
    for other in lists[1:]:
        best = _merge_top(best, other)
    z = jnp.ones_like(best[0])
    for r in range(1, PEER_TOPK):
        z = z + jnp.exp(best[r] - best[0])
    t_ref[...] = best[PEER_TOPK - 1]
    for p in range(PEER_HEADS):
        mx0 = top[0][0][p:p + 1, :]
        mx1 = top[1][0][p:p + 1, :]
        a_ref[p] = jnp.exp(s0_ref[p] - mx0) / z[p:p + 1, :]
        b_ref[p] = jnp.exp(s1_ref[p] - mx1)


def _peer_route(h2, wq_t, sub_keys, tm):
    r, d = h2.shape
    assert PEER_HEADS == SUBLANES
    big = jax.ShapeDtypeStruct((PEER_HEADS, N_KEYS, r), F32)
    bspec = pl.BlockSpec((PEER_HEADS, N_KEYS, tm), lambda i: (0, 0, i))
    return pl.pallas_call(
        _peer_route_kernel,
        grid=(r // tm,),
        in_specs=[pl.BlockSpec((tm, d), lambda i: (i, 0)),
                  pl.BlockSpec(wq_t.shape, lambda i: (0, 0)),
                  pl.BlockSpec(sub_keys.shape, lambda i: (0, 0, 0))],
        out_specs=[bspec, bspec, bspec, bspec, pl.BlockSpec((PEER_HEADS, tm), lambda i: (0, i))],
        out_shape=[big, big, big, big, jax.ShapeDtypeStruct((PEER_HEADS, r), F32)],
        scratch_shapes=[pltpu.VMEM((wq_t.shape[0], tm), F32),
                        pltpu.VMEM((PEER_HEADS, 2, PEER_TOPK, SUBLANES, tm), F32)],
        compiler_params=_cparams(("parallel",)),
        name="peer_route",
    )(h2, wq_t, sub_keys)


_EXPERT_CHUNK = SUBLANES * N_KEYS


def _peer_mix_kernel(h_ref, u_ref, vt_ref, s0_ref, s1_ref, a_ref, b_ref, t_ref, o_ref,
                     act_s, y_s, acc_s):
    c = pl.program_id(1)
    tm = h_ref.shape[0]

    @pl.when(c == 0)
    def _():
        acc_s[...] = jnp.zeros(acc_s.shape, F32)

    act_s[...] = _dot_nt(u_ref[...], h_ref[...])

    for ii in range(SUBLANES):
        rs = slice(ii * N_KEYS, (ii + 1) * N_KEYS)
        for lc in range(tm // LANES):
            ls = slice(lc * LANES, (lc + 1) * LANES)
            w = jnp.zeros((N_KEYS, LANES), F32)
            for p in range(PEER_HEADS):
                x = s0_ref[p, ii:ii + 1, ls] + s1_ref[p, :, ls]
                w = w + jnp.where(x >= t_ref[p:p + 1, ls], a_ref[p, ii:ii + 1, ls] * b_ref[p, :, ls], 0.0)
            y = w * _gelu_tanh(act_s[rs, ls])
            y_s[rs, ls] = y.astype(BF16)
    acc_s[...] += _dot(vt_ref[...], y_s[...])

    @pl.when(c == pl.num_programs(1) - 1)
    def _():
        o_ref[...] = acc_s[...].T


def _peer_mix(h2, u_b, vt_b, s0, s1, a, b, t, tm):
    r, d = h2.shape
    n_exp = u_b.shape[0]
    ch = _EXPERT_CHUNK
    assert n_exp == N_KEYS * N_KEYS and n_exp % ch == 0
    row_blk = pl.BlockSpec((PEER_HEADS, SUBLANES, tm), lambda i, c: (0, c, i))
    all_blk = pl.BlockSpec((PEER_HEADS, N_KEYS, tm), lambda i, c: (0, 0, i))
    return pl.pallas_call(
        _peer_mix_kernel,
        grid=(r // tm, n_exp // ch),
        in_specs=[pl.BlockSpec((tm, d), lambda i, c: (i, 0)),
                  pl.BlockSpec((ch, d), lambda i, c: (c, 0)),
                  pl.BlockSpec((d, ch), lambda i, c: (0, c)),
                  row_blk, all_blk, row_blk, all_blk,
                  pl.BlockSpec((PEER_HEADS, tm), lambda i, c: (0, i))],
        out_specs=pl.BlockSpec((tm, d), lambda i, c: (i, 0)),
        out_shape=jax.ShapeDtypeStruct((r, d), F32),
        scratch_shapes=[pltpu.VMEM((ch, tm), F32), pltpu.VMEM((ch, tm), BF16), pltpu.VMEM((d, tm), F32)],
        compiler_params=_cparams(("parallel", "arbitrary")),
        name="peer_mix",
    )(h2, u_b, vt_b, s0, s1, a, b, t)


def _peer(h2, wq_t, sub_keys, u_b, vt_b, tm_route, tm_mix):
    s0, s1, a, b, t = _peer_route(h2, wq_t, sub_keys, tm_route)
    return _peer_mix(h2, u_b, vt_b, s0, s1, a, b, t, tm_mix)


def _final_kernel(x_ref, f_ref, g2_ref, lg_ref, lb_ref, o_ref):
    z = ALPHA * x_ref[...] + g2_ref[0] * f_ref[...]
    o_ref[...] = _standardize(z) * lg_ref[...] + lb_ref[...]


def _final(x1, f, g2, ln_g, ln_b, tm, rows_per_mod):
    r, d = x1.shape
    m = g2.shape[1]
    if m == 1:
        mod_map = lambda i: ((i * tm) // rows_per_mod, 0, 0)
    else:
        mod_map = lambda i: (i, 0, 0)
    vec = pl.BlockSpec((1, d), lambda i: (0, 0))
    return pl.pallas_call(
        _final_kernel,
        grid=(r // tm,),
        in_specs=[pl.BlockSpec((tm, d), lambda i: (i, 0)),
                  pl.BlockSpec((tm, d), lambda i: (i, 0)),
                  pl.BlockSpec((1, m, d), mod_map), vec, vec],
        out_specs=pl.BlockSpec((tm, d), lambda i: (i, 0)),
        out_shape=jax.ShapeDtypeStruct((r, d), F32),
        compiler_params=_cparams(("parallel",)),
        name="final_norm",
    )(x1, f, g2, ln_g, ln_b)


_PAGES_PER_STEP = 8


def _page_sum_kernel(c_ref, o_ref):
    o_ref[...] = jnp.sum(c_ref[...], axis=1)


def _page_sums(cache):
    n_phys, page, n_h, dh = cache.shape
    pp = _PAGES_PER_STEP
    assert n_phys % pp == 0
    return pl.pallas_call(
        _page_sum_kernel,
        grid=(n_phys // pp,),
        in_specs=[pl.BlockSpec((pp, page, n_h, dh), lambda i: (i, 0, 0, 0))],
        out_specs=pl.BlockSpec((pp, n_h, dh), lambda i: (i, 0, 0)),
        out_shape=jax.ShapeDtypeStruct((n_phys, n_h, dh), F32),
        compiler_params=_cparams(("parallel",)),
        name="page_sums",
    )(cache)


def _block_gate_kernel(pt_ref, ps_ref, q_ref, sel_ref, km_s, *, n_blocks):
    b = pl.program_id(0)
    ppb = MOBA_BLOCK // PAGE_SIZE
    width = ps_ref.shape[1]

    km_s[...] = jnp.zeros(km_s.shape, F32)

    def gather(n, carry):
        acc = jnp.zeros((1, width), F32)
        for j in range(ppb):
            acc = acc + ps_ref[pl.ds(pt_ref[b, n * ppb + j], 1), :]
        km_s[pl.ds(n, 1), :] = acc * (1.0 / MOBA_BLOCK)
        return carry

    lax.fori_loop(0, n_blocks, gather, 0)
    q = q_ref[0]
    sub = lax.broadcasted_iota(jnp.int32, (N_ATTN_HEADS, width), 0)
    lane_w = lax.broadcasted_iota(jnp.int32, (N_ATTN_HEADS, width), 1)
    qb = jnp.where(lane_w // ATTN_HEAD_DIM == sub, jnp.broadcast_to(q, (N_ATTN_HEADS, width)), 0.0)
    gate = _dot_nt(qb, km_s[...], precision=HIGHEST)
    lane = lax.broadcasted_iota(jnp.int32, gate.shape, 1)
    g = jnp.where(lane < n_blocks, gate, NEG_INF)
    out = jnp.zeros(gate.shape, jnp.int32)
    for k in range(MOBA_TOPK):
        mx = jnp.max(g, axis=-1, keepdims=True)
        idx = jnp.min(jnp.where(g == mx, lane, LANES), axis=-1, keepdims=True)
        out = jnp.where(lane == k, idx, out)
        g = jnp.where(lane == idx, NEG_INF, g)
    sel_ref[0] = out


def _block_gate(page_table, page_sum2d, q3, n_blocks):
    db = q3.shape[0]
    assert n_blocks <= LANES and n_blocks >= MOBA_TOPK
    grid_spec = pltpu.PrefetchScalarGridSpec(
        num_scalar_prefetch=1,
        grid=(db,),
        in_specs=[pl.BlockSpec(page_sum2d.shape, lambda i, pt: (0, 0)),
                  pl.BlockSpec((1, 1, q3.shape[2]), lambda i, pt: (i, 0, 0))],
        out_specs=pl.BlockSpec((1, N_ATTN_HEADS, LANES), lambda i, pt: (i, 0, 0)),
        scratch_shapes=[pltpu.VMEM((LANES, page_sum2d.shape[1]), F32)],
    )
    return pl.pallas_call(
        functools.partial(_block_gate_kernel, n_blocks=n_blocks),
        grid_spec=grid_spec,
        out_shape=jax.ShapeDtypeStruct((db, N_ATTN_HEADS, LANES), jnp.int32),
        compiler_params=_cparams(("arbitrary",)),
        name="block_gate",
    )(page_table, page_sum2d, q3)


def _moba_sample_kernel(pt_ref, sel_ref, tbl_ref, q_ref, kn_ref, vn_ref, k_ref, v_ref, o_ref,
                        bias_s, m_s, l_s, acc_s, *, past_len):
    b = pl.program_id(0)
    h = pl.program_id(1)
    j = pl.program_id(2)
    ppb = MOBA_BLOCK // PAGE_SIZE
    n_steps = MOBA_TOPK * ppb
    scale = ATTN_HEAD_DIM ** -0.5
    blk = sel_ref[b, h * MOBA_TOPK + j // ppb]
    pos0 = blk * MOBA_BLOCK + (j % ppb) * PAGE_SIZE
    q = q_ref[0, 0]

    @pl.when(j == 0)
    def _():
        m_s[...] = jnp.full(m_s.shape, NEG_INF, F32)
        l_s[...] = jnp.zeros(l_s.shape, F32)
        acc_s[...] = jnp.zeros(acc_s.shape, F32)

    near = past_len - pos0 - (PAGE_SIZE - 1) < MAX_DISTANCE

    @pl.when(near)
    def _():
        r = lax.broadcasted_iota(jnp.int32, (PAGE_SIZE, 1), 0)
        dist = jnp.maximum(past_len - (pos0 + r), 0)
        bias_s[...] = _bias_from_bucket(_t5_bucket(dist), tbl_ref, h)

    @pl.when(jnp.logical_not(near))
    def _():
        bias_s[...] = jnp.full(bias_s.shape, tbl_ref[N_BUCKETS - 1, h], F32)

    for hh in range(N_ATTN_HEADS):
        @pl.when(h == hh)
        def _():
            kp = k_ref[0, pl.ds(hh, PAGE_SIZE, stride=N_ATTN_HEADS), :]
            vp = v_ref[0, pl.ds(hh, PAGE_SIZE, stride=N_ATTN_HEADS), :]
            s = jnp.sum(kp * q, axis=-1, keepdims=True) * scale + bias_s[...]
            m_old = m_s[...]
            m_new = jnp.maximum(m_old, jnp.max(s, axis=0, keepdims=True))
            a = jnp.exp(m_old - m_new)
            p = jnp.exp(s - m_new)
            l_s[...] = a * l_s[...] + jnp.sum(p, axis=0, keepdims=True)
            acc_s[...] = a * acc_s[...] + jnp.sum(p * vp, axis=0, keepdims=True)
            m_s[...] = m_new

    @pl.when(j == n_steps - 1)
    def _():
        s_new = jnp.sum(kn_ref[0, 0] * q, axis=-1, keepdims=True) * scale + tbl_ref[0, h]
        m_old = m_s[...]
        m_new = jnp.maximum(m_old, s_new)
        a = jnp.exp(m_old - m_new)
        p = jnp.exp(s_new - m_new)
        o_ref[0, 0] = (a * acc_s[...] + p * vn_ref[0, 0]) / (a * l_s[...] + p)


def _moba_sample(page_table, sel, table, q4, k4, v4, cache_k3, cache_v3, past_len):
    db, n_h, _, dh = q4.shape
    ppb = MOBA_BLOCK // PAGE_SIZE
    n_steps = MOBA_TOPK * ppb
    rows = cache_k3.shape[1]

    def page_map(i, h, j, pt, sl):
        blk = sl[i, h * MOBA_TOPK + j // ppb]
        return (pt[i, blk * ppb + j % ppb], 0, 0)

    vec = pl.BlockSpec((1, 1, 1, dh), lambda i, h, j, pt, sl: (i, h, 0, 0))
    grid_spec = pltpu.PrefetchScalarGridSpec(
        num_scalar_prefetch=2,
        grid=(db, n_h, n_steps),
        in_specs=[pl.BlockSpec(memory_space=pltpu.SMEM), vec, vec, vec,
                  pl.BlockSpec((1, rows, dh), page_map),
                  pl.BlockSpec((1, rows, dh), page_map)],
        out_specs=vec,
        scratch_shapes=[pltpu.VMEM((PAGE_SIZE, 1), F32), pltpu.VMEM((1, 1), F32),
                        pltpu.VMEM((1, 1), F32), pltpu.VMEM((1, dh), F32)],
    )
    return pl.pallas_call(
        functools.partial(_moba_sample_kernel, past_len=past_len),
        grid_spec=grid_spec,
        out_shape=jax.ShapeDtypeStruct((db, n_h, 1, dh), F32),
        compiler_params=_cparams(("parallel", "parallel", "arbitrary")),
        name="moba_sample",
    )(page_table, sel, table, q4, k4, v4, cache_k3, cache_v3)


def _mlstm_step_kernel(qk_ref, cs_ref, v_ref, o_ref, g_ref, cw_ref, cb_ref, bg_ref,
                       c_ref, n_ref, m_ref, mem_ref, c_out, n_out, m_out):
    dh = M_HEAD_DIM
    hist = CONV_WIDTH - 1
    y = cb_ref[...] + qk_ref[0] * cw_ref[hist:hist + 1, :]
    for j in range(hist):
        y = y + cs_ref[0, j:j + 1, :] * cw_ref[j:j + 1, :]
    y = y * _sigmoid(y)
    pre = g_ref[0] + bg_ref[...]
    row = lax.broadcasted_iota(jnp.int32, (dh, dh), 0)
    col = lax.broadcasted_iota(jnp.int32, (dh, dh), 1)
    lane = lax.broadcasted_iota(jnp.int32, (1, LANES), 1)
    m_all = jnp.zeros((1, LANES), F32)
    for h in range(N_M_HEADS):
        sl = slice(h * dh, (h + 1) * dh)
        q = y[:, sl]
        k = y[:, M_WIDTH + h * dh:M_WIDTH + (h + 1) * dh] * (dh ** -0.5)
        v = v_ref[0, :, sl]
        i_t = pre[:, h:h + 1]
        logf = _log_sigmoid(pre[:, N_M_HEADS + h:N_M_HEADS + h + 1])
        c_prev = c_ref[0, h]
        n_prev = n_ref[0, h:h + 1, :]
        m_prev = m_ref[0, :, h:h + 1]
        inter = logf + m_prev
        m_t = jnp.maximum(inter, i_t)
        w_inter = jnp.exp(inter - m_t)
        s = jnp.sum(q * k, axis=-1, keepdims=True) * jnp.exp(i_t - m_t)
        cq = _dot_nt(jnp.broadcast_to(q, (SUBLANES, dh)), c_prev, precision=HIGHEST)[0:1, :]
        num = w_inter * cq + s * v
        den = w_inter * jnp.sum(n_prev * q, axis=-1, keepdims=True) + s
        hh = num / jnp.maximum(jnp.abs(den), jnp.exp(-m_t))
        mem_ref[0, :, sl] = _sigmoid(o_ref[0, :, sl]) * hh
        wc = jnp.exp(inter - m_t)
        ws = jnp.exp(i_t - m_t)
        v_col = jnp.sum(jnp.where(row == col, jnp.broadcast_to(v, (dh, dh)), 0.0), axis=-1, keepdims=True)
        c_out[0, h] = wc * c_prev + (ws * v_col) * k
        n_out[0, h:h + 1, :] = wc * n_prev + ws * k
        m_all = jnp.where(lane == h, m_t, m_all)
    m_out[0] = m_all


def _mlstm_step(mqk, cstate, mv, mo, gates, conv_w, conv_b, bg_row, c0, n0, m0):
    db = mqk.shape[0]
    r3 = lambda w: pl.BlockSpec((1, 1, w), lambda i: (i, 0, 0))
    return pl.pallas_call(
        _mlstm_step_kernel,
        grid=(db,),
        in_specs=[r3(2 * M_WIDTH),
                  pl.BlockSpec((1, CONV_WIDTH - 1, 2 * M_WIDTH), lambda i: (i, 0, 0)),
                  r3(M_WIDTH), r3(M_WIDTH), r3(LANES),
                  pl.BlockSpec((CONV_WIDTH, 2 * M_WIDTH), lambda i: (0, 0)),
                  pl.BlockSpec((1, 2 * M_WIDTH), lambda i: (0, 0)),
                  pl.BlockSpec((1, LANES), lambda i: (0, 0)),
                  pl.BlockSpec((1, N_M_HEADS, M_HEAD_DIM, M_HEAD_DIM), lambda i: (i, 0, 0, 0)),
                  pl.BlockSpec((1, N_M_HEADS, M_HEAD_DIM), lambda i: (i, 0, 0)),
                  pl.BlockSpec((1, 1, N_M_HEADS), lambda i: (i, 0, 0))],
        out_specs=[r3(M_WIDTH),
                   pl.BlockSpec((1, N_M_HEADS, M_HEAD_DIM, M_HEAD_DIM), lambda i: (i, 0, 0, 0)),
                   pl.BlockSpec((1, N_M_HEADS, M_HEAD_DIM), lambda i: (i, 0, 0)),
                   r3(LANES)],
        out_shape=[jax.ShapeDtypeStruct((db, 1, M_WIDTH), F32),
                   jax.ShapeDtypeStruct((db, N_M_HEADS, M_HEAD_DIM, M_HEAD_DIM), F32),
                   jax.ShapeDtypeStruct((db, N_M_HEADS, M_HEAD_DIM), F32),
                   jax.ShapeDtypeStruct((db, 1, LANES), F32)],
        compiler_params=_cparams(("parallel",)),
        name="mlstm_step",
    )(mqk, cstate, mv, mo, gates, conv_w, conv_b, bg_row, c0, n0, m0)


def _pad_rows(x, mult):
    r = x.shape[0]
    rp = -(-r // mult) * mult
    return x if rp == r else jnp.pad(x, ((0, rp - r), (0, 0)))


def kernel(x_prompt, x_sample, cache_k, cache_v, page_table, state_C, state_n, state_m, state_conv,
           c_prompt, c_sample, rel_bias_table, w_ada, b_ada, w_in, b_gate, conv_w, conv_b,
           beta_attn, beta_mlstm, w_out, ln1_g, ln1_b, w_query, sub_keys, expert_u, expert_v,
           ln2_g, ln2_b):
    assert w_ada.shape[0] == DEPTH == 1
    B, S, D = x_prompt.shape
    DB, T, _ = x_sample.shape
    assert T == 1
    H, dh = N_ATTN_HEADS, ATTN_HEAD_DIM
    past_len = page_table.shape[1] * PAGE_SIZE
    assert past_len % MOBA_BLOCK == 0
    l = 0

    gate_cols = 2 * N_M_HEADS
    w_in_p = jnp.pad(w_in[l], ((0, 0), (0, LANES - gate_cols))).astype(BF16)
    w_out_b = w_out[l].astype(BF16)
    wq_t = w_query[l].T.astype(BF16)
    u_b = expert_u[l].astype(BF16)
    vt_b = expert_v[l].T.astype(BF16)
    beta = jnp.concatenate([beta_attn[l], beta_mlstm[l]])[None, :]
    bg = b_gate[l]
    bg8 = jnp.broadcast_to(bg[:, None], (gate_cols, LANES))
    bg_row = jnp.pad(bg, (0, LANES - gate_cols))[None, :]
    cw, cb = conv_w[l], conv_b[l][None, :]
    table = rel_bias_table
    lg1, lb1, lg2, lb2 = ln1_g[l][None, :], ln1_b[l][None, :], ln2_g[l][None, :], ln2_b[l][None, :]

    mod = _ada(jnp.concatenate([c_prompt, c_sample], axis=0), w_ada[l], b_ada[l])
    sh1, sc1, g1, sh2, sc2, g2 = [mod[:, i * D:(i + 1) * D] for i in range(6)]
    sc1, sc2 = 1.0 + sc1, 1.0 + sc2
    pm = lambda t: t[:B][:, None, :]
    sm = lambda t: t[B:][None, :, :]

    xp2 = x_prompt.reshape(B * S, D)
    aq, ak, av, mqk, mv, mo, gates = _inproj(xp2, pm(sc1), pm(sh1), w_in_p, 256, S)
    to_heads = lambda t: t.reshape(B, S, H, dh).transpose(0, 2, 1, 3)
    attn = _moba_prompt(to_heads(aq), to_heads(ak), to_heads(av), table, _bias_tiles(table))
    attn = attn.transpose(0, 2, 1, 3).reshape(B * S, H * dh)
    mqk3 = mqk.reshape(B, S, 2 * M_WIDTH)
    mem, c_p, n_p, m_p = _mlstm_prompt(mqk3, mv.reshape(B, S, M_WIDTH), mo.reshape(B, S, M_WIDTH),
                                       gates.reshape(B, S, LANES), cw, cb, bg8)
    x1, h2 = _outproj(attn, mem.reshape(B * S, M_WIDTH), xp2, beta, w_out_b,
                      pm(g1), pm(sc2), pm(sh2), lg1, lb1, 256, S)
    f = _peer(h2, wq_t, sub_keys[l], u_b, vt_b, 256, 512)
    y_prompt = _final(x1, f, pm(g2), lg2, lb2, 512, S).reshape(B, S, D)
    k_prompt = ak.reshape(1, B, S, H, dh)
    v_prompt = av.reshape(1, B, S, H, dh)
    conv_prompt = mqk3[:, S - (CONV_WIDTH - 1):, :][None]

    xs2 = x_sample.reshape(DB, D)
    saq, sak, sav, smqk, smv, smo, sgates = _inproj(xs2, sm(sc1), sm(sh1), w_in_p, DB, 1)
    n_phys = cache_k.shape[1]
    psum = _page_sums(cache_k[l]).reshape(n_phys, H * dh)
    n_blocks = past_len // MOBA_BLOCK
    sel = _block_gate(page_table, psum, saq.reshape(DB, 1, H * dh), n_blocks)
    sel = sel[:, :, :MOBA_TOPK].reshape(DB, H * MOBA_TOPK)
    h4 = lambda t: t.reshape(DB, H, 1, dh)
    s_attn = _moba_sample(page_table, sel, table, h4(saq), h4(sak), h4(sav),
                          cache_k[l].reshape(n_phys, PAGE_SIZE * H, dh),
                          cache_v[l].reshape(n_phys, PAGE_SIZE * H, dh), past_len)
    s_mem, c_s, n_s, m_s = _mlstm_step(
        smqk.reshape(DB, 1, 2 * M_WIDTH), state_conv[l], smv.reshape(DB, 1, M_WIDTH),
        smo.reshape(DB, 1, M_WIDTH), sgates.reshape(DB, 1, LANES), cw, cb, bg_row,
        state_C[l], state_n[l], state_m[l].reshape(DB, 1, N_M_HEADS))
    sx1, sh2_ = _outproj(s_attn.reshape(DB, H * dh), s_mem.reshape(DB, M_WIDTH), xs2, beta, w_out_b,
                         sm(g1), sm(sc2), sm(sh2), lg1, lb1, DB, 1)
    sf = _peer(_pad_rows(sh2_, LANES), wq_t, sub_keys[l], u_b, vt_b, LANES, LANES)[:DB]
    y_sample = _final(sx1, sf, sm(g2), lg2, lb2, DB, 1).reshape(DB, 1, D)
    conv_sample = jnp.concatenate([state_conv[l][:, 1:, :], smqk.reshape(DB, 1, 2 * M_WIDTH)], axis=1)[None]

    return (y_prompt, y_sample,
            k_prompt, v_prompt, c_p[None], n_p[None], m_p[:, :N_M_HEADS, 0][None], conv_prompt,
            sak.reshape(1, DB, 1, H, dh), sav.reshape(1, DB, 1, H, dh),
            c_s[None], n_s[None], m_s[:, 0, :N_M_HEADS][None], conv_sample)
```

```python
import functools
import math

import numpy as np
import jax
import jax.numpy as jnp
from jax import lax
from jax.experimental import pallas as pl
from jax.experimental.pallas import tpu as pltpu

F32 = jnp.float32
BF16 = jnp.bfloat16
NEG_INF = float("-inf")
HIGHEST = lax.Precision.HIGHEST

N_ATTN_HEADS = 8
ATTN_HEAD_DIM = 64
ATTN_WIDTH = N_ATTN_HEADS * ATTN_HEAD_DIM
MOBA_BLOCK = 256
MOBA_TOPK = 3
PAGE_SIZE = 128
N_BUCKETS = 32
MAX_DISTANCE = 128
N_M_HEADS = 4
M_HEAD_DIM = 128
M_WIDTH = N_M_HEADS * M_HEAD_DIM
CONV_WIDTH = 4
M_CHUNK = 128
N_KEYS = 128
PEER_HEADS = 8
PEER_KEY_DIM = 256
PEER_TOPK = 16
LN_EPS = 1e-5
DEPTH = 1
ALPHA = (2.0 * DEPTH) ** 0.25

LANES = 128
SUBLANES = 8
VMEM_LIMIT = 56 * 1024 * 1024


def _cparams(sem, flags=None):
    return pltpu.CompilerParams(dimension_semantics=sem, vmem_limit_bytes=VMEM_LIMIT, flags=flags)


def _bucket_thresholds():
    max_exact = N_BUCKETS // 2
    d = np.arange(0, MAX_DISTANCE + 1)
    far = max_exact + (np.log(np.maximum(d, 1) / max_exact) / math.log(MAX_DISTANCE / max_exact)
                       * (N_BUCKETS - max_exact)).astype(np.int64)
    bucket = np.where(d < max_exact, d, np.minimum(far, N_BUCKETS - 1))
    assert np.all(np.diff(bucket) >= 0) and bucket[-1] == N_BUCKETS - 1
    return tuple(int(np.argmax(bucket >= k)) for k in range(max_exact + 1, N_BUCKETS))


_BUCKET_THRESHOLDS = _bucket_thresholds()


def _t5_bucket(dist):
    max_exact = N_BUCKETS // 2
    far = jnp.full(dist.shape, max_exact, jnp.int32)
    for t in _BUCKET_THRESHOLDS:
        far = far + (dist >= t).astype(jnp.int32)
    return jnp.where(dist < max_exact, dist, far)


def _bias_from_bucket(bucket, tbl_ref, h):
    out = jnp.zeros(bucket.shape, F32)
    for j in range(N_BUCKETS):
        out = jnp.where(bucket == j, tbl_ref[j, h], out)
    return out


def _standardize(x):
    mu = jnp.mean(x, axis=-1, keepdims=True)
    xc = x - mu
    var = jnp.mean(xc * xc, axis=-1, keepdims=True)
    return xc * lax.rsqrt(var + LN_EPS)


def _sigmoid(x):
    return 1.0 / (1.0 + jnp.exp(-x))


def _log_sigmoid(x):
    return jnp.minimum(x, 0.0) - jnp.log1p(jnp.exp(-jnp.abs(x)))


def _gelu_tanh(x):
    c = math.sqrt(2.0 / math.pi)
    hx = 0.5 * x
    return hx + hx * jnp.tanh(x * (c + (c * 0.044715) * (x * x)))


def _dot_nt(a, b, **kw):
    return lax.dot_general(a, b, (((1,), (1,)), ((), ())), preferred_element_type=F32, **kw)


def _dot(a, b, **kw):
    return jnp.dot(a, b, preferred_element_type=F32, **kw)


def _ada_kernel(c_ref, w_ref, b_ref, o_ref):
    c = c_ref[...]
    s = c * _sigmoid(c)
    o_ref[...] = _dot(s, w_ref[...], precision=HIGHEST) + b_ref[...]


def _ada(c_all, w_ada, b_ada):
    n, d = c_all.shape
    n_out = w_ada.shape[1]
    tn = 1024
    return pl.pallas_call(
        _ada_kernel,
        grid=(n_out // tn,),
        in_specs=[pl.BlockSpec((n, d), lambda j: (0, 0)),
                  pl.BlockSpec((d, tn), lambda j: (0, j)),
                  pl.BlockSpec((1, tn), lambda j: (0, j))],
        out_specs=pl.BlockSpec((n, tn), lambda j: (0, j)),
        out_shape=jax.ShapeDtypeStruct((n, n_out), F32),
        compiler_params=_cparams(("parallel",)),
        name="ada_mod",
    )(c_all, w_ada, b_ada.reshape(1, n_out))


_PROJ_GROUPS = (ATTN_WIDTH, ATTN_WIDTH, ATTN_WIDTH, 2 * M_WIDTH, M_WIDTH, M_WIDTH, LANES)
_PROJ_OFFS = tuple(int(v) for v in np.cumsum((0,) + _PROJ_GROUPS))


def _inproj_kernel(x_ref, sc_ref, sh_ref, w_ref, *o_refs):
    h = _standardize(x_ref[...]) * sc_ref[0] + sh_ref[0]
    hb = h.astype(BF16)
    for g, o_ref in enumerate(o_refs):
        o_ref[...] = _dot(hb, w_ref[:, _PROJ_OFFS[g]:_PROJ_OFFS[g + 1]])


def _inproj(x2d, sc3, sh3, w_in_b, tm, rows_per_mod):
    r, d = x2d.shape
    m = sc3.shape[1]
    if m == 1:
        mod_map = lambda i: ((i * tm) // rows_per_mod, 0, 0)
    else:
        mod_map = lambda i: (i, 0, 0)
    return pl.pallas_call(
        _inproj_kernel,
        grid=(r // tm,),
        in_specs=[pl.BlockSpec((tm, d), lambda i: (i, 0)),
                  pl.BlockSpec((1, m, d), mod_map),
                  pl.BlockSpec((1, m, d), mod_map),
                  pl.BlockSpec(w_in_b.shape, lambda i: (0, 0))],
        out_specs=[pl.BlockSpec((tm, g), lambda i: (i, 0)) for g in _PROJ_GROUPS],
        out_shape=[jax.ShapeDtypeStruct((r, g), F32) for g in _PROJ_GROUPS],
        compiler_params=_cparams(("parallel",)),
        name="inproj",
    )(x2d, sc3, sh3, w_in_b)


def _bias_tiles_kernel(tbl_ref, o_ref):
    h = pl.program_id(0)
    row = lax.broadcasted_iota(jnp.int32, (MOBA_BLOCK, MOBA_BLOCK), 0)
    col = lax.broadcasted_iota(jnp.int32, (MOBA_BLOCK, MOBA_BLOCK), 1)
    for t in range(2):
        dist = jnp.maximum(row - col + t * MOBA_BLOCK, 0)
        o_ref[0, t] = _bias_from_bucket(_t5_bucket(dist), tbl_ref, h)


def _bias_tiles(table):
    n_h = table.shape[1]
    return pl.pallas_call(
        _bias_tiles_kernel,
        grid=(n_h,),
        in_specs=[pl.BlockSpec(memory_space=pltpu.SMEM)],
        out_specs=pl.BlockSpec((1, 2, MOBA_BLOCK, MOBA_BLOCK), lambda h: (h, 0, 0, 0)),
        out_shape=jax.ShapeDtypeStruct((n_h, 2, MOBA_BLOCK, MOBA_BLOCK), F32),
        compiler_params=_cparams(("parallel",)),
        name="moba_bias_tiles",
    )(table)


def _moba_prompt_kernel(tbl_ref, q_ref, k_ref, v_ref, bias_ref, o_ref, *, n_blocks):
    h = pl.program_id(1)
    ob = pl.program_id(2)
    blk = MOBA_BLOCK
    scale = ATTN_HEAD_DIM ** -0.5
    q = q_ref[0, 0]
    qb = q.astype(BF16)

    means = [jnp.sum(k_ref[0, 0, n * blk:(n + 1) * blk, :], axis=0, keepdims=True) * (1.0 / blk)
             for n in range(n_blocks)]
    kmean = jnp.concatenate(means + [jnp.zeros((LANES - n_blocks, ATTN_HEAD_DIM), F32)], axis=0)
    gate = _dot_nt(q, kmean, precision=HIGHEST)
    lane = lax.broadcasted_iota(jnp.int32, (blk, LANES), 1)
    g = jnp.where(lane < ob, gate, NEG_INF)
    sel = jnp.zeros((blk, LANES), F32)
    for _ in range(MOBA_TOPK):
        mx = jnp.max(g, axis=-1, keepdims=True)
        idx = jnp.min(jnp.where(g == mx, lane, LANES), axis=-1, keepdims=True)
        hit = lane == idx
        sel = jnp.where(hit, 1.0, sel)
        g = jnp.where(hit, NEG_INF, g)
    sel = jnp.where(lane < ob, sel, 0.0)

    start = pl.multiple_of(ob * blk, blk)
    k_own = k_ref[0, 0, pl.ds(start, blk), :].astype(BF16)
    v_own = v_ref[0, 0, pl.ds(start, blk), :].astype(BF16)
    row = lax.broadcasted_iota(jnp.int32, (blk, blk), 0)
    col = lax.broadcasted_iota(jnp.int32, (blk, blk), 1)
    s = _dot_nt(qb, k_own) * scale + bias_ref[0, 0]
    s = jnp.where(col <= row, s, NEG_INF)
    m0 = jnp.max(s, axis=-1, keepdims=True)
    p = jnp.exp(s - m0)
    l0 = jnp.sum(p, axis=-1, keepdims=True)
    acc0 = _dot(p.astype(BF16), v_own)
    c_far = tbl_ref[N_BUCKETS - 1, h]

    def body(n, carry):
        m, l, acc = carry
        st = pl.multiple_of(n * blk, blk)
        k_n = k_ref[0, 0, pl.ds(st, blk), :].astype(BF16)
        v_n = v_ref[0, 0, pl.ds(st, blk), :].astype(BF16)
        bias = jnp.where(n == ob - 1, bias_ref[0, 1], c_far)
        s = _dot_nt(qb, k_n) * scale + bias
        picked = jnp.sum(jnp.where(lane == n, sel, 0.0), axis=-1, keepdims=True) > 0.5
        s = jnp.where(picked, s, NEG_INF)
        m_new = jnp.maximum(m, jnp.max(s, axis=-1, keepdims=True))
        a = jnp.exp(m - m_new)
        p = jnp.exp(s - m_new)
        l = a * l + jnp.sum(p, axis=-1, keepdims=True)
        acc = a * acc + _dot(p.astype(BF16), v_n)
        return m_new, l, acc

    _, l, acc = lax.fori_loop(0, ob, body, (m0, l0, acc0))
    o_ref[0, 0] = acc / l


def _moba_prompt(q, k, v, table, bias_tiles):
    b, n_h, s, dh = q.shape
    blk = MOBA_BLOCK
    n_blocks = s // blk
    assert s % blk == 0 and n_blocks >= MOBA_TOPK and n_blocks <= LANES
    return pl.pallas_call(
        functools.partial(_moba_prompt_kernel, n_blocks=n_blocks),
        grid=(b, n_h, n_blocks),
        in_specs=[pl.BlockSpec(memory_space=pltpu.SMEM),
                  pl.BlockSpec((1, 1, blk, dh), lambda i, h, j: (i, h, j, 0)),
                  pl.BlockSpec((1, 1, s, dh), lambda i, h, j: (i, h, 0, 0)),
                  pl.BlockSpec((1, 1, s, dh), lambda i, h, j: (i, h, 0, 0)),
                  pl.BlockSpec((1, 2, blk, blk), lambda i, h, j: (h, 0, 0, 0))],
        out_specs=pl.BlockSpec((1, 1, blk, dh), lambda i, h, j: (i, h, j, 0)),
        out_shape=jax.ShapeDtypeStruct((b, n_h, s, dh), F32),
        compiler_params=_cparams(("parallel", "parallel", "parallel")),
        name="moba_prompt",
    )(table, q, k, v, bias_tiles)


_CONV_PAD = SUBLANES


def _cumsum_lanes(x):
    lane = lax.broadcasted_iota(jnp.int32, x.shape, 1)
    k = 1
    while k < x.shape[-1]:
        x = x + jnp.where(lane >= k, pltpu.roll(x, k, 1), 0.0)
        k *= 2
    return x


def _mlstm_prompt_kernel(q_ref, k_ref, v_ref, o_ref, g_ref, cw_ref, cb_ref, bg_ref,
                         mem_ref, c_out, n_out, m_out, xq_s, xk_s, c_s, n_s, m_s):
    ci = pl.program_id(1)
    L = M_CHUNK
    dh = M_HEAD_DIM
    pad = _CONV_PAD
    hist = CONV_WIDTH - 1

    @pl.when(ci == 0)
    def _():
        xq_s[0:pad, :] = jnp.zeros((pad, M_WIDTH), F32)
        xk_s[0:pad, :] = jnp.zeros((pad, M_WIDTH), F32)
        c_s[...] = jnp.zeros(c_s.shape, F32)
        n_s[...] = jnp.zeros(n_s.shape, F32)
        m_s[...] = jnp.zeros(m_s.shape, F32)

    xq_s[pad:pad + L, :] = q_ref[0]
    xk_s[pad:pad + L, :] = k_ref[0]

    def conv(x_s, col0):
        y = cb_ref[:, col0:col0 + M_WIDTH]
        for j in range(CONV_WIDTH):
            y = y + x_s[pad - hist + j:pad - hist + j + L, :] * cw_ref[j:j + 1, col0:col0 + M_WIDTH]
        return y * _sigmoid(y)

    qc = conv(xq_s, 0)
    kc = conv(xk_s, M_WIDTH) * (dh ** -0.5)
    tq = xq_s[pad + L - hist:pad + L, :]
    tk = xk_s[pad + L - hist:pad + L, :]
    xq_s[pad - hist:pad, :] = tq
    xk_s[pad - hist:pad, :] = tk

    gt = g_ref[0].T
    pre = gt[0:2 * N_M_HEADS, :] + bg_ref[...]
    rsel = lax.broadcasted_iota(jnp.int32, pre.shape, 0) < N_M_HEADS
    cum = _cumsum_lanes(jnp.where(rsel, 0.0, _log_sigmoid(pre)))
    r8 = jnp.where(rsel, pre, cum)
    t8 = jnp.concatenate([r8, jnp.zeros((LANES - 2 * N_M_HEADS, L), F32)], axis=0).T

    row = lax.broadcasted_iota(jnp.int32, (L, L), 0)
    col = lax.broadcasted_iota(jnp.int32, (L, L), 1)
    causal = col <= row
    for h in range(N_M_HEADS):
        sl = slice(h * dh, (h + 1) * dh)
        q = qc[:, sl]
        k = kc[:, sl]
        v = v_ref[0, :, sl]
        i_row = r8[h:h + 1, :]
        b_row = r8[N_M_HEADS + h:N_M_HEADS + h + 1, :]
        i_col = t8[:, h:h + 1]
        b_col = t8[:, N_M_HEADS + h:N_M_HEADS + h + 1]
        c_prev = c_s[h]
        n_prev = n_s[h:h + 1, :]
        m_prev = m_s[h:h + 1, 0:1]

        d = jnp.where(causal, b_col - b_row + i_row, NEG_INF)
        inter = b_col + m_prev
        m_t = jnp.maximum(inter, jnp.max(d, axis=-1, keepdims=True))
        w_inter = jnp.exp(inter - m_t)
        qb = q.astype(BF16)
        kb = k.astype(BF16)
        s = _dot_nt(qb, kb) * jnp.exp(d - m_t)
        num = w_inter * _dot_nt(qb, c_prev.astype(BF16)) + _dot(s.astype(BF16), v.astype(BF16))
        den = w_inter * jnp.sum(q * n_prev, axis=-1, keepdims=True) + jnp.sum(s, axis=-1, keepdims=True)
        hh = num / jnp.maximum(jnp.abs(den), jnp.exp(-m_t))
        mem_ref[0, :, sl] = _sigmoid(o_ref[0, :, sl]) * hh

        b_last = b_row[:, L - 1:L]
        g_row = b_last - b_row + i_row
        g_col = b_last - b_col + i_col
        m_new = jnp.maximum(b_last + m_prev, jnp.max(g_row, axis=-1, keepdims=True))
        wc = jnp.exp(b_last + m_prev - m_new)
        ws = jnp.exp(g_col - m_new)
        wv_t = (ws * v).T.astype(BF16)
        c_s[h] = wc * c_prev + _dot(wv_t, kb)
        n_s[h:h + 1, :] = wc * n_prev + jnp.sum(ws * k, axis=0, keepdims=True)
        m_s[h:h + 1, :] = jnp.broadcast_to(m_new, (1, LANES))

    @pl.when(ci == pl.num_programs(1) - 1)
    def _():
        c_out[0] = c_s[...]
        n_out[0] = n_s[0:N_M_HEADS, :]
        m_out[0] = m_s[...]


def _mlstm_prompt(mqk, mv, mo, gates, conv_w, conv_b, bg8):
    b, s, _ = mv.shape
    L = M_CHUNK
    nc = s // L
    assert s % L == 0
    return pl.pallas_call(
        _mlstm_prompt_kernel,
        grid=(b, nc),
        in_specs=[pl.BlockSpec((1, L, M_WIDTH), lambda i, c: (i, c, 0)),
                  pl.BlockSpec((1, L, M_WIDTH), lambda i, c: (i, c, 1)),
                  pl.BlockSpec((1, L, M_WIDTH), lambda i, c: (i, c, 0)),
                  pl.BlockSpec((1, L, M_WIDTH), lambda i, c: (i, c, 0)),
                  pl.BlockSpec((1, L, LANES), lambda i, c: (i, c, 0)),
                  pl.BlockSpec((CONV_WIDTH, 2 * M_WIDTH), lambda i, c: (0, 0)),
                  pl.BlockSpec((1, 2 * M_WIDTH), lambda i, c: (0, 0)),
                  pl.BlockSpec((2 * N_M_HEADS, LANES), lambda i, c: (0, 0))],
        out_specs=[pl.BlockSpec((1, L, M_WIDTH), lambda i, c: (i, c, 0)),
                   pl.BlockSpec((1, N_M_HEADS, M_HEAD_DIM, M_HEAD_DIM), lambda i, c: (i, 0, 0, 0)),
                   pl.BlockSpec((1, N_M_HEADS, M_HEAD_DIM), lambda i, c: (i, 0, 0)),
                   pl.BlockSpec((1, SUBLANES, LANES), lambda i, c: (i, 0, 0))],
        out_shape=[jax.ShapeDtypeStruct((b, s, M_WIDTH), F32),
                   jax.ShapeDtypeStruct((b, N_M_HEADS, M_HEAD_DIM, M_HEAD_DIM), F32),
                   jax.ShapeDtypeStruct((b, N_M_HEADS, M_HEAD_DIM), F32),
                   jax.ShapeDtypeStruct((b, SUBLANES, LANES), F32)],
        scratch_shapes=[pltpu.VMEM((_CONV_PAD + L, M_WIDTH), F32),
                        pltpu.VMEM((_CONV_PAD + L, M_WIDTH), F32),
                        pltpu.VMEM((N_M_HEADS, M_HEAD_DIM, M_HEAD_DIM), F32),
                        pltpu.VMEM((SUBLANES, M_HEAD_DIM), F32),
                        pltpu.VMEM((SUBLANES, LANES), F32)],
        compiler_params=_cparams(("parallel", "arbitrary")),
        name="mlstm_prompt",
    )(mqk, mqk, mv, mo, gates, conv_w, conv_b, bg8)


def _outproj_kernel(a_ref, m_ref, x_ref, beta_ref, w_ref, g1_ref, sc_ref, sh_ref, lg_ref, lb_ref,
                    x1_ref, h2_ref):
    mixed = jnp.concatenate([a_ref[...], m_ref[...]], axis=-1) * beta_ref[...]
    y = _dot(mixed.astype(BF16), w_ref[...])
    z = ALPHA * x_ref[...] + g1_ref[0] * y
    x1 = _standardize(z) * lg_ref[...] + lb_ref[...]
    x1_ref[...] = x1
    h2_ref[...] = (_standardize(x1) * sc_ref[0] + sh_ref[0]).astype(h2_ref.dtype)


def _outproj(attn, mem, x2d, beta, w_out_b, g1, sc2, sh2, ln_g, ln_b, tm, rows_per_mod):
    r, d = x2d.shape
    m = g1.shape[1]
    if m == 1:
        mod_map = lambda i: ((i * tm) // rows_per_mod, 0, 0)
    else:
        mod_map = lambda i: (i, 0, 0)
    vec = pl.BlockSpec((1, d), lambda i: (0, 0))
    mod = pl.BlockSpec((1, m, d), mod_map)
    return pl.pallas_call(
        _outproj_kernel,
        grid=(r // tm,),
        in_specs=[pl.BlockSpec((tm, attn.shape[1]), lambda i: (i, 0)),
                  pl.BlockSpec((tm, mem.shape[1]), lambda i: (i, 0)),
                  pl.BlockSpec((tm, d), lambda i: (i, 0)),
                  vec,
                  pl.BlockSpec(w_out_b.shape, lambda i: (0, 0)),
                  mod, mod, mod, vec, vec],
        out_specs=[pl.BlockSpec((tm, d), lambda i: (i, 0)),
                   pl.BlockSpec((tm, d), lambda i: (i, 0))],
        out_shape=[jax.ShapeDtypeStruct((r, d), F32), jax.ShapeDtypeStruct((r, d), BF16)],
        compiler_params=_cparams(("parallel",)),
        name="outproj",
    )(attn, mem, x2d, beta, w_out_b, g1, sc2, sh2, ln_g, ln_b)


def _oddeven_merge_sort_pairs(n):
    pairs = []

    def merge(lo, m, r):
        step = r * 2
        if step < m:
            merge(lo, m, step)
            merge(lo + r, m, step)
            for i in range(lo + r, lo + m - r, step):
                pairs.append((i, i + r))
        else:
            pairs.append((lo, lo + r))

    def sort(lo, m):
        if m > 1:
            h = m // 2
            sort(lo, h)
            sort(lo + h, h)
            merge(lo, m, 1)

    sort(0, n)
    return tuple(pairs)


_SORT16 = _oddeven_merge_sort_pairs(PEER_TOPK)


def _vmax(a, b):
    if a is None:
        return b
    if b is None:
        return a
    return jnp.maximum(a, b)


def _cmpx(v, i, j):
    a, b = v[i], v[j]
    if b is None:
        return
    if a is None:
        v[i], v[j] = b, None
        return
    v[i], v[j] = jnp.maximum(a, b), jnp.minimum(a, b)


def _bitonic_to_desc(v):
    n = len(v)
    d = n // 2
    while d >= 1:
        for i in range(n):
            if (i & d) == 0:
                _cmpx(v, i, i + d)
        d //= 2
    return v


def _merge_top(x, y):
    n = len(x)
    return _bitonic_to_desc([_vmax(x[i], y[n - 1 - i]) for i in range(n)])


def _top16_desc(sc):
    groups = sc.shape[0] // SUBLANES
    assert groups == PEER_TOPK
    v = [sc[g * SUBLANES:(g + 1) * SUBLANES, :] for g in range(groups)]
    for i, j in _SORT16:
        _cmpx(v, i, j)
    shift = SUBLANES // 2
    while shift >= 1:
        partner = [pltpu.roll(a, shift, 0) for a in v]
        v = _merge_top(v, partner)
        shift //= 2
    return v


def _candidate_lists(a, b):
    k = PEER_TOPK
    lists = []
    for i in range(4):
        n = k // (i + 1)
        lists.append([a[i] + b[j] for j in range(n)])
    for j in range(3):
        n = k // (j + 1)
        col = [a[i] + b[j] for i in range(4, n)]
        if col:
            lists.append(col)
    return [l + [None] * (k - len(l)) for l in lists]


_RANK_STEP = 2.0


def _peer_route_kernel(h_ref, wq_ref, sk_ref, cnt_ref, rk_ref, a_ref, b_ref, qt_s, sc_s, top_s, tz_s):
    tm = h_ref.shape[0]
    kd = PEER_KEY_DIM // 2
    qt_s[...] = _dot_nt(wq_ref[...], h_ref[...])

    def head(p, carry):
        for s in range(2):
            r0 = pl.multiple_of((2 * p + s) * kd, kd)
            sc = _dot(sk_ref[s], qt_s[pl.ds(r0, kd), :], precision=HIGHEST)
            sc_s[p, s] = sc
            srt = _top16_desc(sc)
            for r in range(PEER_TOPK):
                top_s[p, s, r] = srt[r]
        return carry

    lax.fori_loop(0, PEER_HEADS, head, 0)
    sub = lax.broadcasted_iota(jnp.int32, (SUBLANES, tm), 0)

    def on_sublanes(s, r):
        out = top_s[0, s, r]
        for p in range(1, PEER_HEADS):
            out = jnp.where(sub == p, top_s[p, s, r], out)
        return out

    top = [[on_sublanes(s, r) for r in range(PEER_TOPK)] for s in range(2)]
    lists = _candidate_lists(top[0], top[1])
    best = lists[0]
    for other in lists[1:]:
        best = _merge_top(best, other)
    z = jnp.ones_like(best[0])
    for r in range(1, PEER_TOPK):
        z = z + jnp.exp(best[r] - best[0])
    thr = best[PEER_TOPK - 1]
    for p in range(PEER_HEADS):
        tz_s[p, 0] = jnp.broadcast_to(thr[p:p + 1, :], (SUBLANES, tm))
        tz_s[p, 1] = jnp.broadcast_to(z[p:p + 1, :], (SUBLANES, tm))

    def emit(p, carry):
        s0 = sc_s[p, 0]
        s1 = sc_s[p, 1]
        t_row = tz_s[p, 0][0:1, :]
        z_row = tz_s[p, 1][0:1, :]
        rank = jnp.zeros((N_KEYS, tm), F32)
        cnt = jnp.zeros((N_KEYS, tm), F32)
        for r in range(PEER_TOPK):
            b_r = top_s[p, 1, r][0:1, :]
            rank = rank + jnp.where(b_r > s1, _RANK_STEP, 0.0)
            cnt = cnt + jnp.where(s0 + b_r >= t_row, _RANK_STEP, 0.0)
        cnt_ref[p] = cnt
        rk_ref[p] = rank.astype(rk_ref.dtype)
        a_ref[p] = jnp.exp(s0 - top_s[p, 0, 0][0:1, :]) / z_row
        b_ref[p] = jnp.exp(s1 - top_s[p, 1, 0][0:1, :]).astype(b_ref.dtype)
        return carry

    lax.fori_loop(0, PEER_HEADS, emit, 0)


def _peer_route(h2, wq_t, sub_keys, tm):
    r, d = h2.shape
    assert PEER_HEADS == SUBLANES
    shp = (PEER_HEADS, N_KEYS, r)
    bspec = pl.BlockSpec((PEER_HEADS, N_KEYS, tm), lambda i: (0, 0, i))
    return pl.pallas_call(
        _peer_route_kernel,
        grid=(r // tm,),
        in_specs=[pl.BlockSpec((tm, d), lambda i: (i, 0)),
                  pl.BlockSpec(wq_t.shape, lambda i: (0, 0)),
                  pl.BlockSpec(sub_keys.shape, lambda i: (0, 0, 0))],
        out_specs=[bspec, bspec, bspec, bspec],
        out_shape=[jax.ShapeDtypeStruct(shp, F32), jax.ShapeDtypeStruct(shp, BF16),
                   jax.ShapeDtypeStruct(shp, F32), jax.ShapeDtypeStruct(shp, BF16)],
        scratch_shapes=[pltpu.VMEM((wq_t.shape[0], tm), F32),
                        pltpu.VMEM((PEER_HEADS, 2, N_KEYS, tm), F32),
                        pltpu.VMEM((PEER_HEADS, 2, PEER_TOPK, SUBLANES, tm), F32),
                        pltpu.VMEM((PEER_HEADS, 2, SUBLANES, tm), F32)],
        compiler_params=_cparams(("parallel",)),
        name="peer_route",
    )(h2, wq_t, sub_keys)


_EXPERT_CHUNK = SUBLANES * N_KEYS
_MIX_SUBTILE = 2 * SUBLANES


def _peer_mix_kernel(h_ref, u_ref, vt_ref, cnt_ref, rk_ref, a_ref, b_ref, o_ref, act_s, y_s, acc_s):
    c = pl.program_id(1)
    tm = h_ref.shape[0]

    @pl.when(c == 0)
    def _():
        acc_s[...] = jnp.zeros(acc_s.shape, F32)

    act_s[...] = _dot_nt(u_ref[...], h_ref[...])

    sub = _MIX_SUBTILE
    zero = jnp.zeros((sub, LANES), BF16)
    for ii in range(SUBLANES):
        for lc in range(tm // LANES):
            ls = slice(lc * LANES, (lc + 1) * LANES)
            cb = [jnp.broadcast_to(cnt_ref[p, ii:ii + 1, ls], (sub, LANES)).astype(BF16) for p in range(PEER_HEADS)]
            ab = [jnp.broadcast_to(a_ref[p, ii:ii + 1, ls], (sub, LANES)).astype(BF16) for p in range(PEER_HEADS)]
            for js in range(N_KEYS // sub):
                jr = slice(js * sub, (js + 1) * sub)
                w = None
                for p in range(PEER_HEADS):
                    term = jnp.maximum(jnp.minimum(ab[p] * b_ref[p, jr, ls], cb[p] - rk_ref[p, jr, ls]), zero)
                    w = term if w is None else w + term
                rs = slice(ii * N_KEYS + js * sub, ii * N_KEYS + (js + 1) * sub)
                y_s[rs, ls] = w * _gelu_tanh(act_s[rs, ls]).astype(BF16)
    acc_s[...] += _dot(vt_ref[...], y_s[...])

    @pl.when(c == pl.num_programs(1) - 1)
    def _():
        o_ref[...] = acc_s[...].T


def _peer_mix(h2, u_b, vt_b, cnt, rk, a, b, tm):
    r, d = h2.shape
    n_exp = u_b.shape[0]
    ch = _EXPERT_CHUNK
    assert n_exp == N_KEYS * N_KEYS and n_exp % ch == 0
    row_blk = pl.BlockSpec((PEER_HEADS, SUBLANES, tm), lambda i, c: (0, c, i))
    all_blk = pl.BlockSpec((PEER_HEADS, N_KEYS, tm), lambda i, c: (0, 0, i))
    return pl.pallas_call(
        _peer_mix_kernel,
        grid=(r // tm, n_exp // ch),
        in_specs=[pl.BlockSpec((tm, d), lambda i, c: (i, 0)),
                  pl.BlockSpec((ch, d), lambda i, c: (c, 0)),
                  pl.BlockSpec((d, ch), lambda i, c: (0, c)),
                  row_blk, all_blk, row_blk, all_blk],
        out_specs=pl.BlockSpec((tm, d), lambda i, c: (i, 0)),
        out_shape=jax.ShapeDtypeStruct((r, d), F32),
        scratch_shapes=[pltpu.VMEM((ch, tm), F32), pltpu.VMEM((ch, tm), BF16), pltpu.VMEM((d, tm), F32)],
        compiler_params=_cparams(("parallel", "arbitrary")),
        name="peer_mix",
    )(h2, u_b, vt_b, cnt, rk, a, b)


def _peer(h2, wq_t, sub_keys, u_b, vt_b, tm_route, tm_mix):
    cnt, rk, a, b = _peer_route(h2, wq_t, sub_keys, tm_route)
    return _peer_mix(h2, u_b, vt_b, cnt, rk, a, b, tm_mix)


def _final_kernel(x_ref, f_ref, g2_ref, lg_ref, lb_ref, o_ref):
    z = ALPHA * x_ref[...] + g2_ref[0] * f_ref[...]
    o_ref[...] = _standardize(z) * lg_ref[...] + lb_ref[...]


def _final(x1, f, g2, ln_g, ln_b, tm, rows_per_mod):
    r, d = x1.shape
    m = g2.shape[1]
    if m == 1:
        mod_map = lambda i: ((i * tm) // rows_per_mod, 0, 0)
    else:
        mod_map = lambda i: (i, 0, 0)
    vec = pl.BlockSpec((1, d), lambda i: (0, 0))
    return pl.pallas_call(
        _final_kernel,
        grid=(r // tm,),
        in_specs=[pl.BlockSpec((tm, d), lambda i: (i, 0)),
                  pl.BlockSpec((tm, d), lambda i: (i, 0)),
                  pl.BlockSpec((1, m, d), mod_map), vec, vec],
        out_specs=pl.BlockSpec((tm, d), lambda i: (i, 0)),
        out_shape=jax.ShapeDtypeStruct((r, d), F32),
        compiler_params=_cparams(("parallel",)),
        name="final_norm",
    )(x1, f, g2, ln_g, ln_b)


_PAGES_PER_STEP = 8


def _page_sum_kernel(c_ref, o_ref):
    o_ref[...] = jnp.sum(c_ref[...], axis=1)


def _page_sums(cache):
    n_phys, page, n_h, dh = cache.shape
    pp = _PAGES_PER_STEP
    assert n_phys % pp == 0
    return pl.pallas_call(
        _page_sum_kernel,
        grid=(n_phys // pp,),
        in_specs=[pl.BlockSpec((pp, page, n_h, dh), lambda i: (i, 0, 0, 0))],
        out_specs=pl.BlockSpec((pp, n_h, dh), lambda i: (i, 0, 0)),
        out_shape=jax.ShapeDtypeStruct((n_phys, n_h, dh), F32),
        compiler_params=_cparams(("parallel",)),
        name="page_sums",
    )(cache)


def _block_gate_kernel(pt_ref, ps_ref, q_ref, sel_ref, km_s, *, n_blocks):
    b = pl.program_id(0)
    ppb = MOBA_BLOCK // PAGE_SIZE
    width = ps_ref.shape[1]

    km_s[...] = jnp.zeros(km_s.shape, F32)

    def gather(n, carry):
        acc = jnp.zeros((1, width), F32)
        for j in range(ppb):
            acc = acc + ps_ref[pl.ds(pt_ref[b, n * ppb + j], 1), :]
        km_s[pl.ds(n, 1), :] = acc * (1.0 / MOBA_BLOCK)
        return carry

    lax.fori_loop(0, n_blocks, gather, 0)
    q = q_ref[0]
    sub = lax.broadcasted_iota(jnp.int32, (N_ATTN_HEADS, width), 0)
    lane_w = lax.broadcasted_iota(jnp.int32, (N_ATTN_HEADS, width), 1)
    qb = jnp.where(lane_w // ATTN_HEAD_DIM == sub, jnp.broadcast_to(q, (N_ATTN_HEADS, width)), 0.0)
    gate = _dot_nt(qb, km_s[...], precision=HIGHEST)
    lane = lax.broadcasted_iota(jnp.int32, gate.shape, 1)
    g = jnp.where(lane < n_blocks, gate, NEG_INF)
    out = jnp.zeros(gate.shape, jnp.int32)
    for k in range(MOBA_TOPK):
        mx = jnp.max(g, axis=-1, keepdims=True)
        idx = jnp.min(jnp.where(g == mx, lane, LANES), axis=-1, keepdims=True)
        out = jnp.where(lane == k, idx, out)
        g = jnp.where(lane == idx, NEG_INF, g)
    sel_ref[0] = out


def _block_gate(page_table, page_sum2d, q3, n_blocks):
    db = q3.shape[0]
    assert n_blocks <= LANES and n_blocks >= MOBA_TOPK
    grid_spec = pltpu.PrefetchScalarGridSpec(
        num_scalar_prefetch=1,
        grid=(db,),
        in_specs=[pl.BlockSpec(page_sum2d.shape, lambda i, pt: (0, 0)),
                  pl.BlockSpec((1, 1, q3.shape[2]), lambda i, pt: (i, 0, 0))],
        out_specs=pl.BlockSpec((1, N_ATTN_HEADS, LANES), lambda i, pt: (i, 0, 0)),
        scratch_shapes=[pltpu.VMEM((LANES, page_sum2d.shape[1]), F32)],
    )
    return pl.pallas_call(
        functools.partial(_block_gate_kernel, n_blocks=n_blocks),
        grid_spec=grid_spec,
        out_shape=jax.ShapeDtypeStruct((db, N_ATTN_HEADS, LANES), jnp.int32),
        compiler_params=_cparams(("arbitrary",)),
        name="block_gate",
    )(page_table, page_sum2d, q3)


_PAGES_PER_BLOCK = MOBA_BLOCK // PAGE_SIZE
_SEL_PAGES = MOBA_TOPK * _PAGES_PER_BLOCK


def _sample_page_copies(pt_ref, sel_ref, k_hbm, v_hbm, kbuf, vbuf, sem, bb, par):
    out = []
    for h in range(N_ATTN_HEADS):
        for kt in range(MOBA_TOPK):
            blk = sel_ref[bb, h * MOBA_TOPK + kt]
            for pp in range(_PAGES_PER_BLOCK):
                page = pt_ref[bb, blk * _PAGES_PER_BLOCK + pp]
                slot = h * _SEL_PAGES + kt * _PAGES_PER_BLOCK + pp
                out.append(pltpu.make_async_copy(k_hbm.at[page, :, h, :], kbuf.at[par, slot], sem.at[0, par]))
                out.append(pltpu.make_async_copy(v_hbm.at[page, :, h, :], vbuf.at[par, slot], sem.at[1, par]))
    return out


def _moba_sample_kernel(pt_ref, sel_ref, tbl_ref, q_ref, kn_ref, vn_ref, k_hbm, v_hbm, o_ref,
                        kbuf, vbuf, bias_s, sem, *, past_len):
    b = pl.program_id(0)
    nb = pl.num_programs(0)
    par = b % 2
    scale = ATTN_HEAD_DIM ** -0.5
    copies = functools.partial(_sample_page_copies, pt_ref, sel_ref, k_hbm, v_hbm, kbuf, vbuf, sem)

    @pl.when(b == 0)
    def _():
        for c in copies(0, 0):
            c.start()

    @pl.when(b + 1 < nb)
    def _():
        for c in copies(b + 1, 1 - par):
            c.start()

    for c in copies(b, par):
        c.wait()

    r = lax.broadcasted_iota(jnp.int32, (MOBA_BLOCK, 1), 0)
    for h in range(N_ATTN_HEADS):
        q = q_ref[0, h:h + 1, :]
        for kt in range(MOBA_TOPK):
            pos0 = sel_ref[b, h * MOBA_TOPK + kt] * MOBA_BLOCK
            near = past_len - pos0 - (MOBA_BLOCK - 1) < MAX_DISTANCE
            rows = slice(kt * MOBA_BLOCK, (kt + 1) * MOBA_BLOCK)

            @pl.when(near)
            def _():
                dist = jnp.maximum(past_len - (pos0 + r), 0)
                bias_s[rows, :] = _bias_from_bucket(_t5_bucket(dist), tbl_ref, h)

            @pl.when(jnp.logical_not(near))
            def _():
                bias_s[rows, :] = jnp.full((MOBA_BLOCK, 1), tbl_ref[N_BUCKETS - 1, h], F32)

        n_rows = _SEL_PAGES * PAGE_SIZE
        kh = kbuf[par, h * _SEL_PAGES:(h + 1) * _SEL_PAGES].reshape(n_rows, ATTN_HEAD_DIM)
        vh = vbuf[par, h * _SEL_PAGES:(h + 1) * _SEL_PAGES].reshape(n_rows, ATTN_HEAD_DIM)
        s = jnp.sum(kh * q, axis=-1, keepdims=True) * scale + bias_s[...]
        s_new = jnp.sum(kn_ref[0, h:h + 1, :] * q, axis=-1, keepdims=True) * scale + tbl_ref[0, h]
        m = jnp.maximum(jnp.max(s, axis=0, keepdims=True), s_new)
        p = jnp.exp(s - m)
        p_new = jnp.exp(s_new - m)
        num = jnp.sum(p * vh, axis=0, keepdims=True) + p_new * vn_ref[0, h:h + 1, :]
        o_ref[0, h:h + 1, :] = num / (jnp.sum(p, axis=0, keepdims=True) + p_new)


def _moba_sample(page_table, sel, table, q3, k3, v3, cache_k, cache_v, past_len):
    db, n_h, dh = q3.shape
    vec = pl.BlockSpec((1, n_h, dh), lambda i, pt, sl: (i, 0, 0))
    grid_spec = pltpu.PrefetchScalarGridSpec(
        num_scalar_prefetch=2,
        grid=(db,),
        in_specs=[pl.BlockSpec(memory_space=pltpu.SMEM), vec, vec, vec,
                  pl.BlockSpec(memory_space=pl.ANY), pl.BlockSpec(memory_space=pl.ANY)],
        out_specs=vec,
        scratch_shapes=[pltpu.VMEM((2, n_h * _SEL_PAGES, PAGE_SIZE, dh), F32),
                        pltpu.VMEM((2, n_h * _SEL_PAGES, PAGE_SIZE, dh), F32),
                        pltpu.VMEM((_SEL_PAGES * PAGE_SIZE, 1), F32),
                        pltpu.SemaphoreType.DMA((2, 2))],
    )
    return pl.pallas_call(
        functools.partial(_moba_sample_kernel, past_len=past_len),
        grid_spec=grid_spec,
        out_shape=jax.ShapeDtypeStruct((db, n_h, dh), F32),
        compiler_params=_cparams(("arbitrary",)),
        name="moba_sample",
    )(page_table, sel, table, q3, k3, v3, cache_k, cache_v)


def _mlstm_step_kernel(qk_ref, cs_ref, v_ref, o_ref, g_ref, cw_ref, cb_ref, bg_ref,
                       c_ref, n_ref, m_ref, mem_ref, c_out, n_out, m_out):
    dh = M_HEAD_DIM
    hist = CONV_WIDTH - 1
    y = cb_ref[...] + qk_ref[0] * cw_ref[hist:hist + 1, :]
    for j in range(hist):
        y = y + cs_ref[0, j:j + 1, :] * cw_ref[j:j + 1, :]
    y = y * _sigmoid(y)
    pre = g_ref[0] + bg_ref[...]
    row = lax.broadcasted_iota(jnp.int32, (dh, dh), 0)
    col = lax.broadcasted_iota(jnp.int32, (dh, dh), 1)
    lane = lax.broadcasted_iota(jnp.int32, (1, LANES), 1)
    m_all = jnp.zeros((1, LANES), F32)
    for h in range(N_M_HEADS):
        sl = slice(h * dh, (h + 1) * dh)
        q = y[:, sl]
        k = y[:, M_WIDTH + h * dh:M_WIDTH + (h + 1) * dh] * (dh ** -0.5)
        v = v_ref[0, :, sl]
        i_t = pre[:, h:h + 1]
        logf = _log_sigmoid(pre[:, N_M_HEADS + h:N_M_HEADS + h + 1])
        c_prev = c_ref[0, h]
        n_prev = n_ref[0, h:h + 1, :]
        m_prev = m_ref[0, :, h:h + 1]
        inter = logf + m_prev
        m_t = jnp.maximum(inter, i_t)
        w_inter = jnp.exp(inter - m_t)
        s = jnp.sum(q * k, axis=-1, keepdims=True) * jnp.exp(i_t - m_t)
        cq = _dot_nt(jnp.broadcast_to(q, (SUBLANES, dh)), c_prev, precision=HIGHEST)[0:1, :]
        num = w_inter * cq + s * v
        den = w_inter * jnp.sum(n_prev * q, axis=-1, keepdims=True) + s
        hh = num / jnp.maximum(jnp.abs(den), jnp.exp(-m_t))
        mem_ref[0, :, sl] = _sigmoid(o_ref[0, :, sl]) * hh
        wc = jnp.exp(inter - m_t)
        ws = jnp.exp(i_t - m_t)
        v_col = jnp.sum(jnp.where(row == col, jnp.broadcast_to(v, (dh, dh)), 0.0), axis=-1, keepdims=True)
        c_out[0, h] = wc * c_prev + (ws * v_col) * k
        n_out[0, h:h + 1, :] = wc * n_prev + ws * k
        m_all = jnp.where(lane == h, m_t, m_all)
    m_out[0] = m_all


def _mlstm_step(mqk, cstate, mv, mo, gates, conv_w, conv_b, bg_row, c0, n0, m0):
    db = mqk.shape[0]
    r3 = lambda w: pl.BlockSpec((1, 1, w), lambda i: (i, 0, 0))
    return pl.pallas_call(
        _mlstm_step_kernel,
        grid=(db,),
        in_specs=[r3(2 * M_WIDTH),
                  pl.BlockSpec((1, CONV_WIDTH - 1, 2 * M_WIDTH), lambda i: (i, 0, 0)),
                  r3(M_WIDTH), r3(M_WIDTH), r3(LANES),
                  pl.BlockSpec((CONV_WIDTH, 2 * M_WIDTH), lambda i: (0, 0)),
                  pl.BlockSpec((1, 2 * M_WIDTH), lambda i: (0, 0)),
                  pl.BlockSpec((1, LANES), lambda i: (0, 0)),
                  pl.BlockSpec((1, N_M_HEADS, M_HEAD_DIM, M_HEAD_DIM), lambda i: (i, 0, 0, 0)),
                  pl.BlockSpec((1, N_M_HEADS, M_HEAD_DIM), lambda i: (i, 0, 0)),
                  pl.BlockSpec((1, 1, N_M_HEADS), lambda i: (i, 0, 0))],
        out_specs=[r3(M_WIDTH),
                   pl.BlockSpec((1, N_M_HEADS, M_HEAD_DIM, M_HEAD_DIM), lambda i: (i, 0, 0, 0)),
                   pl.BlockSpec((1, N_M_HEADS, M_HEAD_DIM), lambda i: (i, 0, 0)),
                   r3(LANES)],
        out_shape=[jax.ShapeDtypeStruct((db, 1, M_WIDTH), F32),
                   jax.ShapeDtypeStruct((db, N_M_HEADS, M_HEAD_DIM, M_HEAD_DIM), F32),
                   jax.ShapeDtypeStruct((db, N_M_HEADS, M_HEAD_DIM), F32),
                   jax.ShapeDtypeStruct((db, 1, LANES), F32)],
        compiler_params=_cparams(("parallel",)),
        name="mlstm_step",
    )(mqk, cstate, mv, mo, gates, conv_w, conv_b, bg_row, c0, n0, m0)


def _pad_rows(x, mult):
    r = x.shape[0]
    rp = -(-r // mult) * mult
    return x if rp == r else jnp.pad(x, ((0, rp - r), (0, 0)))


def kernel(x_prompt, x_sample, cache_k, cache_v, page_table, state_C, state_n, state_m, state_conv,
           c_prompt, c_sample, rel_bias_table, w_ada, b_ada, w_in, b_gate, conv_w, conv_b,
           beta_attn, beta_mlstm, w_out, ln1_g, ln1_b, w_query, sub_keys, expert_u, expert_v,
           ln2_g, ln2_b):
    assert w_ada.shape[0] == DEPTH == 1
    B, S, D = x_prompt.shape
    DB, T, _ = x_sample.shape
    assert T == 1
    H, dh = N_ATTN_HEADS, ATTN_HEAD_DIM
    past_len = page_table.shape[1] * PAGE_SIZE
    assert past_len % MOBA_BLOCK == 0
    l = 0

    gate_cols = 2 * N_M_HEADS
    w_in_p = jnp.pad(w_in[l], ((0, 0), (0, LANES - gate_cols))).astype(BF16)
    w_out_b = w_out[l].astype(BF16)
    wq_t = w_query[l].T.astype(BF16)
    u_b = expert_u[l].astype(BF16)
    vt_b = expert_v[l].T.astype(BF16)
    beta = jnp.concatenate([beta_attn[l], beta_mlstm[l]])[None, :]
    bg = b_gate[l]
    bg8 = jnp.broadcast_to(bg[:, None], (gate_cols, LANES))
    bg_row = jnp.pad(bg, (0, LANES - gate_cols))[None, :]
    cw, cb = conv_w[l], conv_b[l][None, :]
    table = rel_bias_table
    lg1, lb1, lg2, lb2 = ln1_g[l][None, :], ln1_b[l][None, :], ln2_g[l][None, :], ln2_b[l][None, :]

    mod = _ada(jnp.concatenate([c_prompt, c_sample], axis=0), w_ada[l], b_ada[l])
    sh1, sc1, g1, sh2, sc2, g2 = [mod[:, i * D:(i + 1) * D] for i in range(6)]
    sc1, sc2 = 1.0 + sc1, 1.0 + sc2
    pm = lambda t: t[:B][:, None, :]
    sm = lambda t: t[B:][None, :, :]

    xp2 = x_prompt.reshape(B * S, D)
    aq, ak, av, mqk, mv, mo, gates = _inproj(xp2, pm(sc1), pm(sh1), w_in_p, 256, S)
    to_heads = lambda t: t.reshape(B, S, H, dh).transpose(0, 2, 1, 3)
    attn = _moba_prompt(to_heads(aq), to_heads(ak), to_heads(av), table, _bias_tiles(table))
    attn = attn.transpose(0, 2, 1, 3).reshape(B * S, H * dh)
    mqk3 = mqk.reshape(B, S, 2 * M_WIDTH)
    mem, c_p, n_p, m_p = _mlstm_prompt(mqk3, mv.reshape(B, S, M_WIDTH), mo.reshape(B, S, M_WIDTH),
                                       gates.reshape(B, S, LANES), cw, cb, bg8)
    x1, h2 = _outproj(attn, mem.reshape(B * S, M_WIDTH), xp2, beta, w_out_b,
                      pm(g1), pm(sc2), pm(sh2), lg1, lb1, 256, S)
    f = _peer(h2, wq_t, sub_keys[l], u_b, vt_b, 256, 512)
    y_prompt = _final(x1, f, pm(g2), lg2, lb2, 512, S).reshape(B, S, D)
    k_prompt = ak.reshape(1, B, S, H, dh)
    v_prompt = av.reshape(1, B, S, H, dh)
    conv_prompt = mqk3[:, S - (CONV_WIDTH - 1):, :][None]

    xs2 = x_sample.reshape(DB, D)
    saq, sak, sav, smqk, smv, smo, sgates = _inproj(xs2, sm(sc1), sm(sh1), w_in_p, DB, 1)
    n_phys = cache_k.shape[1]
    psum = _page_sums(cache_k[l]).reshape(n_phys, H * dh)
    n_blocks = past_len // MOBA_BLOCK
    sel = _block_gate(page_table, psum, saq.reshape(DB, 1, H * dh), n_blocks)
    sel = sel[:, :, :MOBA_TOPK].reshape(DB, H * MOBA_TOPK)
    h3 = lambda t: t.reshape(DB, H, dh)
    s_attn = _moba_sample(page_table, sel, table, h3(saq), h3(sak), h3(sav), cache_k[l], cache_v[l], past_len)
    s_mem, c_s, n_s, m_s = _mlstm_step(
        smqk.reshape(DB, 1, 2 * M_WIDTH), state_conv[l], smv.reshape(DB, 1, M_WIDTH),
        smo.reshape(DB, 1, M_WIDTH), sgates.reshape(DB, 1, LANES), cw, cb, bg_row,
        state_C[l], state_n[l], state_m[l].reshape(DB, 1, N_M_HEADS))
    sx1, sh2_ = _outproj(s_attn.reshape(DB, H * dh), s_mem.reshape(DB, M_WIDTH), xs2, beta, w_out_b,
                         sm(g1), sm(sc2), sm(sh2), lg1, lb1, DB, 1)
    sf = _peer(_pad_rows(sh2_, LANES), wq_t, sub_keys[l], u_b, vt_b, LANES, LANES)[:DB]
    y_sample = _final(sx1, sf, sm(g2), lg2, lb2, DB, 1).reshape(DB, 1, D)
    conv_sample = jnp.concatenate([state_conv[l][:, 1:, :], smqk.reshape(DB, 1, 2 * M_WIDTH)], axis=1)[None]

    return (y_prompt, y_sample,
            k_prompt, v_prompt, c_p[None], n_p[None], m_p[:, :N_M_HEADS, 0][None], conv_prompt,
            sak.reshape(1, DB, 1, H, dh), sav.reshape(1, DB, 1, H, dh),
            c_s[None], n_s[None], m_s[:, 0, :N_M_HEADS][None], conv_sample)
```

```python
import functools
import math

import numpy as np
import jax
import jax.numpy as jnp
from jax import lax
from jax.experimental import pallas as pl
from jax.experimental.pallas import tpu as pltpu

F32 = jnp.float32
BF16 = jnp.bfloat16
NEG_INF = float("-inf")
HIGHEST = lax.Precision.HIGHEST

N_ATTN_HEADS = 8
ATTN_HEAD_DIM = 64
ATTN_WIDTH = N_ATTN_HEADS * ATTN_HEAD_DIM
MOBA_BLOCK = 256
MOBA_TOPK = 3
PAGE_SIZE = 128
N_BUCKETS = 32
MAX_DISTANCE = 128
N_M_HEADS = 4
M_HEAD_DIM = 128
M_WIDTH = N_M_HEADS * M_HEAD_DIM
CONV_WIDTH = 4
M_CHUNK = 128
N_KEYS = 128
PEER_HEADS = 8
PEER_KEY_DIM = 256
PEER_TOPK = 16
LN_EPS = 1e-5
DEPTH = 1
ALPHA = (2.0 * DEPTH) ** 0.25

LANES = 128
SUBLANES = 8
VMEM_LIMIT = 56 * 1024 * 1024


def _cparams(sem, flags=None):
    return pltpu.CompilerParams(dimension_semantics=sem, vmem_limit_bytes=VMEM_LIMIT, flags=flags)


def _bucket_thresholds():
    max_exact = N_BUCKETS // 2
    d = np.arange(0, MAX_DISTANCE + 1)
    far = max_exact + (np.log(np.maximum(d, 1) / max_exact) / math.log(MAX_DISTANCE / max_exact)
                       * (N_BUCKETS - max_exact)).astype(np.int64)
    bucket = np.where(d < max_exact, d, np.minimum(far, N_BUCKETS - 1))
    assert np.all(np.diff(bucket) >= 0) and bucket[-1] == N_BUCKETS - 1
    return tuple(int(np.argmax(bucket >= k)) for k in range(max_exact + 1, N_BUCKETS))


_BUCKET_THRESHOLDS = _bucket_thresholds()


def _t5_bucket(dist):
    max_exact = N_BUCKETS // 2
    far = jnp.full(dist.shape, max_exact, jnp.int32)
    for t in _BUCKET_THRESHOLDS:
        far = far + (dist >= t).astype(jnp.int32)
    return jnp.where(dist < max_exact, dist, far)


def _bias_from_bucket(bucket, tbl_ref, h):
    out = jnp.zeros(bucket.shape, F32)
    for j in range(N_BUCKETS):
        out = jnp.where(bucket == j, tbl_ref[j, h], out)
    return out


def _standardize(x):
    mu = jnp.mean(x, axis=-1, keepdims=True)
    xc = x - mu
    var = jnp.mean(xc * xc, axis=-1, keepdims=True)
    return xc * lax.rsqrt(var + LN_EPS)


def _sigmoid(x):
    return 1.0 / (1.0 + jnp.exp(-x))


def _log_sigmoid(x):
    return jnp.minimum(x, 0.0) - jnp.log1p(jnp.exp(-jnp.abs(x)))


def _gelu_tanh(x):
    c = math.sqrt(2.0 / math.pi)
    hx = 0.5 * x
    return hx + hx * jnp.tanh(x * (c + (c * 0.044715) * (x * x)))


def _dot_nt(a, b, **kw):
    return lax.dot_general(a, b, (((1,), (1,)), ((), ())), preferred_element_type=F32, **kw)


def _dot(a, b, **kw):
    return jnp.dot(a, b, preferred_element_type=F32, **kw)


def _ada_kernel(c_ref, w_ref, b_ref, o_ref):
    c = c_ref[...]
    s = c * _sigmoid(c)
    o_ref[...] = _dot(s, w_ref[...], precision=HIGHEST) + b_ref[...]


def _ada(c_all, w_ada, b_ada):
    n, d = c_all.shape
    n_out = w_ada.shape[1]
    tn = 1024
    return pl.pallas_call(
        _ada_kernel,
        grid=(n_out // tn,),
        in_specs=[pl.BlockSpec((n, d), lambda j: (0, 0)),
                  pl.BlockSpec((d, tn), lambda j: (0, j)),
                  pl.BlockSpec((1, tn), lambda j: (0, j))],
        out_specs=pl.BlockSpec((n, tn), lambda j: (0, j)),
        out_shape=jax.ShapeDtypeStruct((n, n_out), F32),
        compiler_params=_cparams(("parallel",)),
        name="ada_mod",
    )(c_all, w_ada, b_ada.reshape(1, n_out))


_PROJ_GROUPS = (ATTN_WIDTH, ATTN_WIDTH, ATTN_WIDTH, 2 * M_WIDTH, M_WIDTH, M_WIDTH, LANES)
_PROJ_OFFS = tuple(int(v) for v in np.cumsum((0,) + _PROJ_GROUPS))


def _inproj_kernel(x_ref, sc_ref, sh_ref, w_ref, *o_refs):
    h = _standardize(x_ref[...]) * sc_ref[0] + sh_ref[0]
    hb = h.astype(BF16)
    for g, o_ref in enumerate(o_refs):
        o_ref[...] = _dot(hb, w_ref[:, _PROJ_OFFS[g]:_PROJ_OFFS[g + 1]])


def _inproj(x2d, sc3, sh3, w_in_b, tm, rows_per_mod):
    r, d = x2d.shape
    m = sc3.shape[1]
    if m == 1:
        mod_map = lambda i: ((i * tm) // rows_per_mod, 0, 0)
    else:
        mod_map = lambda i: (i, 0, 0)
    return pl.pallas_call(
        _inproj_kernel,
        grid=(r // tm,),
        in_specs=[pl.BlockSpec((tm, d), lambda i: (i, 0)),
                  pl.BlockSpec((1, m, d), mod_map),
                  pl.BlockSpec((1, m, d), mod_map),
                  pl.BlockSpec(w_in_b.shape, lambda i: (0, 0))],
        out_specs=[pl.BlockSpec((tm, g), lambda i: (i, 0)) for g in _PROJ_GROUPS],
        out_shape=[jax.ShapeDtypeStruct((r, g), F32) for g in _PROJ_GROUPS],
        compiler_params=_cparams(("parallel",)),
        name="inproj",
    )(x2d, sc3, sh3, w_in_b)


def _inproj_prompt_kernel(x_ref, sc_ref, sh_ref, w_ref, wt_ref, qt_ref, k_ref, kt_ref, vt_ref, vb_ref,
                          mqk_ref, mv_ref, mo_ref, g_ref):
    h = _standardize(x_ref[...]) * sc_ref[0] + sh_ref[0]
    hb = h.astype(BF16)
    aw = ATTN_WIDTH
    k_ref[...] = _dot(hb, w_ref[:, _PROJ_OFFS[1]:_PROJ_OFFS[2]])
    for g, o_ref in ((3, mqk_ref), (4, mv_ref), (5, mo_ref), (6, g_ref)):
        o_ref[...] = _dot(hb, w_ref[:, _PROJ_OFFS[g]:_PROJ_OFFS[g + 1]])
    qt_ref[0] = _dot_nt(wt_ref[0:aw, :], hb)
    kt_ref[0] = _dot_nt(wt_ref[aw:2 * aw, :], hb)
    vt = _dot_nt(wt_ref[2 * aw:3 * aw, :], hb)
    vt_ref[0] = vt
    vb_ref[0, 0] = vt


def _inproj_prompt(x3, sc3, sh3, w_in_b, w_in_tb):
    b, s, d = x3.shape
    tm = MOBA_BLOCK
    nt = s // tm
    r = b * s
    aw = ATTN_WIDTH
    assert s % tm == 0
    row = lambda w: pl.BlockSpec((tm, w), lambda i: (i, 0))
    tr = pl.BlockSpec((1, aw, tm), lambda i: (i // nt, 0, i % nt))
    mod = pl.BlockSpec((1, 1, d), lambda i: (i // nt, 0, 0))
    return pl.pallas_call(
        _inproj_prompt_kernel,
        grid=(r // tm,),
        in_specs=[row(d), mod, mod,
                  pl.BlockSpec(w_in_b.shape, lambda i: (0, 0)),
                  pl.BlockSpec(w_in_tb.shape, lambda i: (0, 0))],
        out_specs=[tr, row(aw), tr, tr,
                   pl.BlockSpec((1, 1, aw, tm), lambda i: (i // nt, i % nt, 0, 0)),
                   row(2 * M_WIDTH), row(M_WIDTH), row(M_WIDTH), row(LANES)],
        out_shape=[jax.ShapeDtypeStruct((b, aw, s), F32), jax.ShapeDtypeStruct((r, aw), F32),
                   jax.ShapeDtypeStruct((b, aw, s), F32), jax.ShapeDtypeStruct((b, aw, s), F32),
                   jax.ShapeDtypeStruct((b, nt, aw, tm), F32),
                   jax.ShapeDtypeStruct((r, 2 * M_WIDTH), F32), jax.ShapeDtypeStruct((r, M_WIDTH), F32),
                   jax.ShapeDtypeStruct((r, M_WIDTH), F32), jax.ShapeDtypeStruct((r, LANES), F32)],
        compiler_params=_cparams(("parallel",)),
        name="inproj_prompt",
    )(x3.reshape(r, d), sc3, sh3, w_in_b, w_in_tb)


def _bias_tiles_kernel(tbl_ref, o_ref):
    h = pl.program_id(0)
    key = lax.broadcasted_iota(jnp.int32, (MOBA_BLOCK, MOBA_BLOCK), 0)
    qry = lax.broadcasted_iota(jnp.int32, (MOBA_BLOCK, MOBA_BLOCK), 1)
    for t in range(2):
        dist = jnp.maximum(qry - key + t * MOBA_BLOCK, 0)
        o_ref[0, t] = _bias_from_bucket(_t5_bucket(dist), tbl_ref, h)


def _bias_tiles(table):
    n_h = table.shape[1]
    return pl.pallas_call(
        _bias_tiles_kernel,
        grid=(n_h,),
        in_specs=[pl.BlockSpec(memory_space=pltpu.SMEM)],
        out_specs=pl.BlockSpec((1, 2, MOBA_BLOCK, MOBA_BLOCK), lambda h: (h, 0, 0, 0)),
        out_shape=jax.ShapeDtypeStruct((n_h, 2, MOBA_BLOCK, MOBA_BLOCK), F32),
        compiler_params=_cparams(("parallel",)),
        name="moba_bias_tiles",
    )(table)


_HEADS_PER_STEP = 4
_HEAD_GROUP_COLS = _HEADS_PER_STEP * ATTN_HEAD_DIM


def _moba_prompt_kernel(tbl_ref, qt_ref, k_ref, vb_ref, bias_ref, o_ref, selb_s, *, n_blocks):
    hp = pl.program_id(1)
    ob = pl.program_id(2)
    blk = MOBA_BLOCK
    dh = ATTN_HEAD_DIM
    gc = _HEAD_GROUP_COLS
    scale = dh ** -0.5
    heads = range(_HEADS_PER_STEP)

    kmean = jnp.concatenate(
        [jnp.sum(k_ref[0, n * blk:(n + 1) * blk, :], axis=0, keepdims=True) * (1.0 / blk)
         for n in range(n_blocks)], axis=0)
    sub = lax.broadcasted_iota(jnp.int32, (n_blocks, blk), 0)
    key = lax.broadcasted_iota(jnp.int32, (blk, blk), 0)
    qry = lax.broadcasted_iota(jnp.int32, (blk, blk), 1)
    start = pl.multiple_of(ob * blk, blk)
    fsub = lax.broadcasted_iota(jnp.int32, (gc, blk), 0)
    k_own = k_ref[0, pl.ds(start, blk), :].astype(BF16)
    c_fars = [tbl_ref[N_BUCKETS - 1, hp * _HEADS_PER_STEP + j] for j in heads]

    qzs = [jnp.where((fsub >= j * dh) & (fsub < (j + 1) * dh), qt_ref[0], 0.0) for j in heads]
    qzbs = [qz.astype(BF16) for qz in qzs]
    gates = [_dot(kmean, qz, precision=HIGHEST) for qz in qzs]
    qk_own = [_dot(k_own, qzb) for qzb in qzbs]
    for j in heads:
        g = jnp.where(sub < ob, gates[j], NEG_INF)
        sel = jnp.zeros((n_blocks, blk), F32)
        for _ in range(MOBA_TOPK):
            mx = jnp.max(g, axis=0, keepdims=True)
            idx = jnp.min(jnp.where(g == mx, sub, n_blocks), axis=0, keepdims=True)
            hit = sub == idx
            sel = jnp.where(hit, 1.0, sel)
            g = jnp.where(hit, NEG_INF, g)
        sel = jnp.where(sub < ob, sel, 0.0)
        for n in range(n_blocks):
            selb_s[j, n] = jnp.broadcast_to(sel[n:n + 1, :], (SUBLANES, blk))
    own = []
    for j in heads:
        s = jnp.where(key <= qry, qk_own[j] * scale + bias_ref[j, 0], NEG_INF)
        m0 = jnp.max(s, axis=0, keepdims=True)
        p = jnp.exp(s - m0)
        own.append((m0, jnp.sum(p, axis=0, keepdims=True), p.astype(BF16)))
    init = tuple((own[j][0], own[j][1], _dot(vb_ref[0, ob, j * dh:(j + 1) * dh, :].astype(BF16), own[j][2]))
                 for j in heads)

    def body(n, carry):
        st = pl.multiple_of(n * blk, blk)
        k_n = k_ref[0, pl.ds(st, blk), :].astype(BF16)
        qk = [_dot(k_n, qzbs[j]) for j in heads]
        stats = []
        for j in heads:
            m, l, _ = carry[j]
            bias = jnp.where(n == ob - 1, bias_ref[j, 1], c_fars[j])
            s = jnp.where(selb_s[j, n][0:1, :] > 0.5, qk[j] * scale + bias, NEG_INF)
            m_new = jnp.maximum(m, jnp.max(s, axis=0, keepdims=True))
            a = jnp.exp(m - m_new)
            p = jnp.exp(s - m_new)
            stats.append((m_new, a * l + jnp.sum(p, axis=0, keepdims=True), a, p.astype(BF16)))
        pv = [_dot(vb_ref[0, n, j * dh:(j + 1) * dh, :].astype(BF16), stats[j][3]) for j in heads]
        return tuple((stats[j][0], stats[j][1], stats[j][2] * carry[j][2] + pv[j]) for j in heads)

    final = lax.fori_loop(0, ob, body, init)
    for j in heads:
        _, l, acc = final[j]
        o_ref[0, j * dh:(j + 1) * dh, :] = acc / l


def _moba_prompt(qt, k, vb, table, bias_tiles):
    b, aw, s = qt.shape
    blk = MOBA_BLOCK
    n_blocks = s // blk
    hps = _HEADS_PER_STEP
    gc = _HEAD_GROUP_COLS
    assert s % blk == 0 and MOBA_TOPK <= n_blocks <= SUBLANES and aw % gc == 0 and gc % LANES == 0
    return pl.pallas_call(
        functools.partial(_moba_prompt_kernel, n_blocks=n_blocks),
        grid=(b, aw // gc, n_blocks),
        in_specs=[pl.BlockSpec(memory_space=pltpu.SMEM),
                  pl.BlockSpec((1, gc, blk), lambda i, g, j: (i, g, j)),
                  pl.BlockSpec((1, s, gc), lambda i, g, j: (i, 0, g)),
                  pl.BlockSpec((1, n_blocks, gc, blk), lambda i, g, j: (i, 0, g, 0)),
                  pl.BlockSpec((hps, 2, blk, blk), lambda i, g, j: (g, 0, 0, 0))],
        out_specs=pl.BlockSpec((1, gc, blk), lambda i, g, j: (i, g, j)),
        out_shape=jax.ShapeDtypeStruct((b, aw, s), F32),
        scratch_shapes=[pltpu.VMEM((hps, n_blocks, SUBLANES, blk), F32)],
        compiler_params=_cparams(("parallel", "parallel", "parallel")),
        name="moba_prompt",
    )(table, qt, k, vb, bias_tiles)


_CONV_PAD = SUBLANES


def _cumsum_lanes(x):
    lane = lax.broadcasted_iota(jnp.int32, x.shape, 1)
    k = 1
    while k < x.shape[-1]:
        x = x + jnp.where(lane >= k, pltpu.roll(x, k, 1), 0.0)
        k *= 2
    return x


def _mlstm_prompt_kernel(q_ref, k_ref, v_ref, o_ref, g_ref, cw_ref, cb_ref, bg_ref,
                         mem_ref, c_out, n_out, m_out, xq_s, xk_s, c_s, n_s, m_s):
    ci = pl.program_id(1)
    L = M_CHUNK
    dh = M_HEAD_DIM
    pad = _CONV_PAD
    hist = CONV_WIDTH - 1

    @pl.when(ci == 0)
    def _():
        xq_s[0:pad, :] = jnp.zeros((pad, M_WIDTH), F32)
        xk_s[0:pad, :] = jnp.zeros((pad, M_WIDTH), F32)
        c_s[...] = jnp.zeros(c_s.shape, F32)
        n_s[...] = jnp.zeros(n_s.shape, F32)
        m_s[...] = jnp.zeros(m_s.shape, F32)

    xq_s[pad:pad + L, :] = q_ref[0]
    xk_s[pad:pad + L, :] = k_ref[0]

    def conv(x_s, col0):
        y = cb_ref[:, col0:col0 + M_WIDTH]
        for j in range(CONV_WIDTH):
            y = y + x_s[pad - hist + j:pad - hist + j + L, :] * cw_ref[j:j + 1, col0:col0 + M_WIDTH]
        return y * _sigmoid(y)

    qc = conv(xq_s, 0)
    kc = conv(xk_s, M_WIDTH) * (dh ** -0.5)
    tq = xq_s[pad + L - hist:pad + L, :]
    tk = xk_s[pad + L - hist:pad + L, :]
    xq_s[pad - hist:pad, :] = tq
    xk_s[pad - hist:pad, :] = tk

    gt = g_ref[0].T
    pre = gt[0:2 * N_M_HEADS, :] + bg_ref[...]
    rsel = lax.broadcasted_iota(jnp.int32, pre.shape, 0) < N_M_HEADS
    cum = _cumsum_lanes(jnp.where(rsel, 0.0, _log_sigmoid(pre)))
    r8 = jnp.where(rsel, pre, cum)
    t8 = jnp.concatenate([r8, jnp.zeros((LANES - 2 * N_M_HEADS, L), F32)], axis=0).T

    row = lax.broadcasted_iota(jnp.int32, (L, L), 0)
    col = lax.broadcasted_iota(jnp.int32, (L, L), 1)
    causal = col <= row
    for h in range(N_M_HEADS):
        sl = slice(h * dh, (h + 1) * dh)
        q = qc[:, sl]
        k = kc[:, sl]
        v = v_ref[0, :, sl]
        i_row = r8[h:h + 1, :]
        b_row = r8[N_M_HEADS + h:N_M_HEADS + h + 1, :]
        i_col = t8[:, h:h + 1]
        b_col = t8[:, N_M_HEADS + h:N_M_HEADS + h + 1]
        c_prev = c_s[h]
        n_prev = n_s[h:h + 1, :]
        m_prev = m_s[h:h + 1, 0:1]

        d = jnp.where(causal, b_col - b_row + i_row, NEG_INF)
        inter = b_col + m_prev
        m_t = jnp.maximum(inter, jnp.max(d, axis=-1, keepdims=True))
        w_inter = jnp.exp(inter - m_t)
        qb = q.astype(BF16)
        kb = k.astype(BF16)
        s = _dot_nt(qb, kb) * jnp.exp(d - m_t)
        num = w_inter * _dot_nt(qb, c_prev.astype(BF16)) + _dot(s.astype(BF16), v.astype(BF16))
        den = w_inter * jnp.sum(q * n_prev, axis=-1, keepdims=True) + jnp.sum(s, axis=-1, keepdims=True)
        hh = num / jnp.maximum(jnp.abs(den), jnp.exp(-m_t))
        mem_ref[0, :, sl] = _sigmoid(o_ref[0, :, sl]) * hh

        b_last = b_row[:, L - 1:L]
        g_row = b_last - b_row + i_row
        g_col = b_last - b_col + i_col
        m_new = jnp.maximum(b_last + m_prev, jnp.max(g_row, axis=-1, keepdims=True))
        wc = jnp.exp(b_last + m_prev - m_new)
        ws = jnp.exp(g_col - m_new)
        wv_t = (ws * v).T.astype(BF16)
        c_s[h] = wc * c_prev + _dot(wv_t, kb)
        n_s[h:h + 1, :] = wc * n_prev + jnp.sum(ws * k, axis=0, keepdims=True)
        m_s[h:h + 1, :] = jnp.broadcast_to(m_new, (1, LANES))

    @pl.when(ci == pl.num_programs(1) - 1)
    def _():
        c_out[0] = c_s[...]
        n_out[0] = n_s[0:N_M_HEADS, :]
        m_out[0] = m_s[...]


def _mlstm_prompt(mqk, mv, mo, gates, conv_w, conv_b, bg8):
    b, s, _ = mv.shape
    L = M_CHUNK
    nc = s // L
    assert s % L == 0
    return pl.pallas_call(
        _mlstm_prompt_kernel,
        grid=(b, nc),
        in_specs=[pl.BlockSpec((1, L, M_WIDTH), lambda i, c: (i, c, 0)),
                  pl.BlockSpec((1, L, M_WIDTH), lambda i, c: (i, c, 1)),
                  pl.BlockSpec((1, L, M_WIDTH), lambda i, c: (i, c, 0)),
                  pl.BlockSpec((1, L, M_WIDTH), lambda i, c: (i, c, 0)),
                  pl.BlockSpec((1, L, LANES), lambda i, c: (i, c, 0)),
                  pl.BlockSpec((CONV_WIDTH, 2 * M_WIDTH), lambda i, c: (0, 0)),
                  pl.BlockSpec((1, 2 * M_WIDTH), lambda i, c: (0, 0)),
                  pl.BlockSpec((2 * N_M_HEADS, LANES), lambda i, c: (0, 0))],
        out_specs=[pl.BlockSpec((1, L, M_WIDTH), lambda i, c: (i, c, 0)),
                   pl.BlockSpec((1, N_M_HEADS, M_HEAD_DIM, M_HEAD_DIM), lambda i, c: (i, 0, 0, 0)),
                   pl.BlockSpec((1, N_M_HEADS, M_HEAD_DIM), lambda i, c: (i, 0, 0)),
                   pl.BlockSpec((1, SUBLANES, LANES), lambda i, c: (i, 0, 0))],
        out_shape=[jax.ShapeDtypeStruct((b, s, M_WIDTH), F32),
                   jax.ShapeDtypeStruct((b, N_M_HEADS, M_HEAD_DIM, M_HEAD_DIM), F32),
                   jax.ShapeDtypeStruct((b, N_M_HEADS, M_HEAD_DIM), F32),
                   jax.ShapeDtypeStruct((b, SUBLANES, LANES), F32)],
        scratch_shapes=[pltpu.VMEM((_CONV_PAD + L, M_WIDTH), F32),
                        pltpu.VMEM((_CONV_PAD + L, M_WIDTH), F32),
                        pltpu.VMEM((N_M_HEADS, M_HEAD_DIM, M_HEAD_DIM), F32),
                        pltpu.VMEM((SUBLANES, M_HEAD_DIM), F32),
                        pltpu.VMEM((SUBLANES, LANES), F32)],
        compiler_params=_cparams(("parallel", "arbitrary")),
        name="mlstm_prompt",
    )(mqk, mqk, mv, mo, gates, conv_w, conv_b, bg8)


def _outproj_kernel(a_ref, m_ref, x_ref, beta_ref, w_ref, g1_ref, sc_ref, sh_ref, lg_ref, lb_ref,
                    x1_ref, h2_ref, *, attn_feature_major):
    attn = a_ref[0].T if attn_feature_major else a_ref[...]
    mixed = jnp.concatenate([attn, m_ref[...]], axis=-1) * beta_ref[...]
    y = _dot(mixed.astype(BF16), w_ref[...])
    z = ALPHA * x_ref[...] + g1_ref[0] * y
    x1 = _standardize(z) * lg_ref[...] + lb_ref[...]
    x1_ref[...] = x1
    h2_ref[...] = (_standardize(x1) * sc_ref[0] + sh_ref[0]).astype(h2_ref.dtype)


def _outproj(attn, mem, x2d, beta, w_out_b, g1, sc2, sh2, ln_g, ln_b, tm, rows_per_mod):
    r, d = x2d.shape
    m = g1.shape[1]
    if m == 1:
        mod_map = lambda i: ((i * tm) // rows_per_mod, 0, 0)
    else:
        mod_map = lambda i: (i, 0, 0)
    vec = pl.BlockSpec((1, d), lambda i: (0, 0))
    mod = pl.BlockSpec((1, m, d), mod_map)
    feature_major = attn.ndim == 3
    if feature_major:
        nt = rows_per_mod // tm
        assert rows_per_mod % tm == 0 and attn.shape[2] == rows_per_mod
        attn_spec = pl.BlockSpec((1, attn.shape[1], tm), lambda i: (i // nt, 0, i % nt))
    else:
        attn_spec = pl.BlockSpec((tm, attn.shape[1]), lambda i: (i, 0))
    return pl.pallas_call(
        functools.partial(_outproj_kernel, attn_feature_major=feature_major),
        grid=(r // tm,),
        in_specs=[attn_spec,
                  pl.BlockSpec((tm, mem.shape[1]), lambda i: (i, 0)),
                  pl.BlockSpec((tm, d), lambda i: (i, 0)),
                  vec,
                  pl.BlockSpec(w_out_b.shape, lambda i: (0, 0)),
                  mod, mod, mod, vec, vec],
        out_specs=[pl.BlockSpec((tm, d), lambda i: (i, 0)),
                   pl.BlockSpec((tm, d), lambda i: (i, 0))],
        out_shape=[jax.ShapeDtypeStruct((r, d), F32), jax.ShapeDtypeStruct((r, d), BF16)],
        compiler_params=_cparams(("parallel",)),
        name="outproj",
    )(attn, mem, x2d, beta, w_out_b, g1, sc2, sh2, ln_g, ln_b)


def _oddeven_merge_sort_pairs(n):
    pairs = []

    def merge(lo, m, r):
        step = r * 2
        if step < m:
            merge(lo, m, step)
            merge(lo + r, m, step)
            for i in range(lo + r, lo + m - r, step):
                pairs.append((i, i + r))
        else:
            pairs.append((lo, lo + r))

    def sort(lo, m):
        if m > 1:
            h = m // 2
            sort(lo, h)
            sort(lo + h, h)
            merge(lo, m, 1)

    sort(0, n)
    return tuple(pairs)


_SORT16 = _oddeven_merge_sort_pairs(PEER_TOPK)


def _vmax(a, b):
    if a is None:
        return b
    if b is None:
        return a
    return jnp.maximum(a, b)


def _cmpx(v, i, j):
    a, b = v[i], v[j]
    if b is None:
        return
    if a is None:
        v[i], v[j] = b, None
        return
    v[i], v[j] = jnp.maximum(a, b), jnp.minimum(a, b)


def _bitonic_to_desc(v):
    n = len(v)
    d = n // 2
    while d >= 1:
        for i in range(n):
            if (i & d) == 0:
                _cmpx(v, i, i + d)
        d //= 2
    return v


def _merge_top(x, y):
    n = len(x)
    return _bitonic_to_desc([_vmax(x[i], y[n - 1 - i]) for i in range(n)])


def _top16_desc(sc):
    groups = sc.shape[0] // SUBLANES
    assert groups == PEER_TOPK
    v = [sc[g * SUBLANES:(g + 1) * SUBLANES, :] for g in range(groups)]
    for i, j in _SORT16:
        _cmpx(v, i, j)
    shift = SUBLANES // 2
    while shift >= 1:
        partner = [pltpu.roll(a, shift, 0) for a in v]
        v = _merge_top(v, partner)
        shift //= 2
    return v


def _candidate_lists(a, b):
    k = PEER_TOPK
    lists = []
    for i in range(4):
        n = k // (i + 1)
        lists.append([a[i] + b[j] for j in range(n)])
    for j in range(3):
        n = k // (j + 1)
        col = [a[i] + b[j] for i in range(4, n)]
        if col:
            lists.append(col)
    return [l + [None] * (k - len(l)) for l in lists]


_RANK_STEP = 2.0


def _peer_route_kernel(h_ref, wq_ref, sk_ref, cnt_ref, rk_ref, a_ref, b_ref, qt_s, sc_s, top_s, tz_s):
    tm = h_ref.shape[0]
    kd = PEER_KEY_DIM // 2
    qt_s[...] = _dot_nt(wq_ref[...], h_ref[...])

    def head(p, carry):
        for s in range(2):
            r0 = pl.multiple_of((2 * p + s) * kd, kd)
            sc = _dot(sk_ref[s], qt_s[pl.ds(r0, kd), :], precision=HIGHEST)
            sc_s[p, s] = sc
            srt = _top16_desc(sc)
            for r in range(PEER_TOPK):
                top_s[p, s, r] = srt[r]
        return carry

    lax.fori_loop(0, PEER_HEADS, head, 0)
    sub = lax.broadcasted_iota(jnp.int32, (SUBLANES, tm), 0)

    def on_sublanes(s, r):
        out = top_s[0, s, r]
        for p in range(1, PEER_HEADS):
            out = jnp.where(sub == p, top_s[p, s, r], out)
        return out

    top = [[on_sublanes(s, r) for r in range(PEER_TOPK)] for s in range(2)]
    lists = _candidate_lists(top[0], top[1])
    best = lists[0]
    for other in lists[1:]:
        best = _merge_top(best, other)
    z = jnp.ones_like(best[0])
    for r in range(1, PEER_TOPK):
        z = z + jnp.exp(best[r] - best[0])
    thr = best[PEER_TOPK - 1]
    for p in range(PEER_HEADS):
        tz_s[p, 0] = jnp.broadcast_to(thr[p:p + 1, :], (SUBLANES, tm))
        tz_s[p, 1] = jnp.broadcast_to(z[p:p + 1, :], (SUBLANES, tm))

    def emit(p, carry):
        s0 = sc_s[p, 0]
        s1 = sc_s[p, 1]
        t_row = tz_s[p, 0][0:1, :]
        z_row = tz_s[p, 1][0:1, :]
        rank = jnp.zeros((N_KEYS, tm), F32)
        cnt = jnp.zeros((N_KEYS, tm), F32)
        for r in range(PEER_TOPK):
            b_r = top_s[p, 1, r][0:1, :]
            rank = rank + jnp.where(b_r > s1, _RANK_STEP, 0.0)
            cnt = cnt + jnp.where(s0 + b_r >= t_row, _RANK_STEP, 0.0)
        cnt_ref[p] = cnt
        rk_ref[p] = rank.astype(rk_ref.dtype)
        a_ref[p] = jnp.exp(s0 - top_s[p, 0, 0][0:1, :]) / z_row
        b_ref[p] = jnp.exp(s1 - top_s[p, 1, 0][0:1, :]).astype(b_ref.dtype)
        return carry

    lax.fori_loop(0, PEER_HEADS, emit, 0)


def _peer_route(h2, wq_t, sub_keys, tm):
    r, d = h2.shape
    assert PEER_HEADS == SUBLANES
    shp = (PEER_HEADS, N_KEYS, r)
    bspec = pl.BlockSpec((PEER_HEADS, N_KEYS, tm), lambda i: (0, 0, i))
    return pl.pallas_call(
        _peer_route_kernel,
        grid=(r // tm,),
        in_specs=[pl.BlockSpec((tm, d), lambda i: (i, 0)),
                  pl.BlockSpec(wq_t.shape, lambda i: (0, 0)),
                  pl.BlockSpec(sub_keys.shape, lambda i: (0, 0, 0))],
        out_specs=[bspec, bspec, bspec, bspec],
        out_shape=[jax.ShapeDtypeStruct(shp, F32), jax.ShapeDtypeStruct(shp, BF16),
                   jax.ShapeDtypeStruct(shp, F32), jax.ShapeDtypeStruct(shp, BF16)],
        scratch_shapes=[pltpu.VMEM((wq_t.shape[0], tm), F32),
                        pltpu.VMEM((PEER_HEADS, 2, N_KEYS, tm), F32),
                        pltpu.VMEM((PEER_HEADS, 2, PEER_TOPK, SUBLANES, tm), F32),
                        pltpu.VMEM((PEER_HEADS, 2, SUBLANES, tm), F32)],
        compiler_params=_cparams(("parallel",)),
        name="peer_route",
    )(h2, wq_t, sub_keys)


_EXPERT_CHUNK = SUBLANES * N_KEYS
_MIX_SUBTILE = 2 * SUBLANES


def _peer_mix_kernel(h_ref, u_ref, vt_ref, cnt_ref, rk_ref, a_ref, b_ref, o_ref, act_s, y_s, acc_s):
    c = pl.program_id(1)
    tm = h_ref.shape[0]

    @pl.when(c == 0)
    def _():
        acc_s[...] = jnp.zeros(acc_s.shape, F32)

    act_s[...] = _dot_nt(u_ref[...], h_ref[...])

    sub = _MIX_SUBTILE
    zero = jnp.zeros((sub, LANES), BF16)
    for ii in range(SUBLANES):
        for lc in range(tm // LANES):
            ls = slice(lc * LANES, (lc + 1) * LANES)
            cb = [jnp.broadcast_to(cnt_ref[p, ii:ii + 1, ls], (sub, LANES)).astype(BF16) for p in range(PEER_HEADS)]
            ab = [jnp.broadcast_to(a_ref[p, ii:ii + 1, ls], (sub, LANES)).astype(BF16) for p in range(PEER_HEADS)]
            for js in range(N_KEYS // sub):
                jr = slice(js * sub, (js + 1) * sub)
                w = None
                for p in range(PEER_HEADS):
                    term = jnp.maximum(jnp.minimum(ab[p] * b_ref[p, jr, ls], cb[p] - rk_ref[p, jr, ls]), zero)
                    w = term if w is None else w + term
                rs = slice(ii * N_KEYS + js * sub, ii * N_KEYS + (js + 1) * sub)
                y_s[rs, ls] = w * _gelu_tanh(act_s[rs, ls]).astype(BF16)
    acc_s[...] += _dot(vt_ref[...], y_s[...])

    @pl.when(c == pl.num_programs(1) - 1)
    def _():
        o_ref[...] = acc_s[...].T


def _peer_mix(h2, u_b, vt_b, cnt, rk, a, b, tm):
    r, d = h2.shape
    n_exp = u_b.shape[0]
    ch = _EXPERT_CHUNK
    assert n_exp == N_KEYS * N_KEYS and n_exp % ch == 0
    row_blk = pl.BlockSpec((PEER_HEADS, SUBLANES, tm), lambda i, c: (0, c, i))
    all_blk = pl.BlockSpec((PEER_HEADS, N_KEYS, tm), lambda i, c: (0, 0, i))
    return pl.pallas_call(
        _peer_mix_kernel,
        grid=(r // tm, n_exp // ch),
        in_specs=[pl.BlockSpec((tm, d), lambda i, c: (i, 0)),
                  pl.BlockSpec((ch, d), lambda i, c: (c, 0)),
                  pl.BlockSpec((d, ch), lambda i, c: (0, c)),
                  row_blk, all_blk, row_blk, all_blk],
        out_specs=pl.BlockSpec((tm, d), lambda i, c: (i, 0)),
        out_shape=jax.ShapeDtypeStruct((r, d), F32),
        scratch_shapes=[pltpu.VMEM((ch, tm), F32), pltpu.VMEM((ch, tm), BF16), pltpu.VMEM((d, tm), F32)],
        compiler_params=_cparams(("parallel", "arbitrary")),
        name="peer_mix",
    )(h2, u_b, vt_b, cnt, rk, a, b)


def _peer(h2, wq_t, sub_keys, u_b, vt_b, tm_route, tm_mix):
    cnt, rk, a, b = _peer_route(h2, wq_t, sub_keys, tm_route)
    return _peer_mix(h2, u_b, vt_b, cnt, rk, a, b, tm_mix)


def _final_kernel(x_ref, f_ref, g2_ref, lg_ref, lb_ref, o_ref):
    z = ALPHA * x_ref[...] + g2_ref[0] * f_ref[...]
    o_ref[...] = _standardize(z) * lg_ref[...] + lb_ref[...]


def _final(x1, f, g2, ln_g, ln_b, tm, rows_per_mod):
    r, d = x1.shape
    m = g2.shape[1]
    if m == 1:
        mod_map = lambda i: ((i * tm) // rows_per_mod, 0, 0)
    else:
        mod_map = lambda i: (i, 0, 0)
    vec = pl.BlockSpec((1, d), lambda i: (0, 0))
    return pl.pallas_call(
        _final_kernel,
        grid=(r // tm,),
        in_specs=[pl.BlockSpec((tm, d), lambda i: (i, 0)),
                  pl.BlockSpec((tm, d), lambda i: (i, 0)),
                  pl.BlockSpec((1, m, d), mod_map), vec, vec],
        out_specs=pl.BlockSpec((tm, d), lambda i: (i, 0)),
        out_shape=jax.ShapeDtypeStruct((r, d), F32),
        compiler_params=_cparams(("parallel",)),
        name="final_norm",
    )(x1, f, g2, ln_g, ln_b)


_PAGES_PER_STEP = 32


def _page_sum_kernel(c_ref, o_ref):
    pp, n_h, dh, page = c_ref.shape
    width = n_h * dh
    lane = lax.broadcasted_iota(jnp.int32, (width, LANES), 1)
    t = jnp.zeros((width, LANES), F32)
    for pg in range(pp):
        col = jnp.sum(c_ref[pg].reshape(width, page), axis=-1, keepdims=True)
        t = jnp.where(lane == pg, col, t)
    o_ref[...] = t.T[0:pp, :]


def _page_sums(cache_t):
    n_phys, n_h, dh, page = cache_t.shape
    pp = _PAGES_PER_STEP
    assert n_phys % pp == 0 and pp <= LANES
    return pl.pallas_call(
        _page_sum_kernel,
        grid=(n_phys // pp,),
        in_specs=[pl.BlockSpec((pp, n_h, dh, page), lambda i: (i, 0, 0, 0))],
        out_specs=pl.BlockSpec((pp, n_h * dh), lambda i: (i, 0)),
        out_shape=jax.ShapeDtypeStruct((n_phys, n_h * dh), F32),
        compiler_params=_cparams(("parallel",)),
        name="page_sums",
    )(cache_t)


def _block_gate_kernel(pt_ref, ps_ref, q_ref, sel_ref, km_s, *, n_blocks):
    b = pl.program_id(0)
    ppb = MOBA_BLOCK // PAGE_SIZE
    width = ps_ref.shape[1]

    km_s[...] = jnp.zeros(km_s.shape, F32)

    def gather(n, carry):
        acc = jnp.zeros((1, width), F32)
        for j in range(ppb):
            acc = acc + ps_ref[pl.ds(pt_ref[b, n * ppb + j], 1), :]
        km_s[pl.ds(n, 1), :] = acc * (1.0 / MOBA_BLOCK)
        return carry

    lax.fori_loop(0, n_blocks, gather, 0)
    q = q_ref[0]
    sub = lax.broadcasted_iota(jnp.int32, (N_ATTN_HEADS, width), 0)
    lane_w = lax.broadcasted_iota(jnp.int32, (N_ATTN_HEADS, width), 1)
    qb = jnp.where(lane_w // ATTN_HEAD_DIM == sub, jnp.broadcast_to(q, (N_ATTN_HEADS, width)), 0.0)
    gate = _dot_nt(qb, km_s[...], precision=HIGHEST)
    lane = lax.broadcasted_iota(jnp.int32, gate.shape, 1)
    g = jnp.where(lane < n_blocks, gate, NEG_INF)
    out = jnp.zeros(gate.shape, jnp.int32)
    for k in range(MOBA_TOPK):
        mx = jnp.max(g, axis=-1, keepdims=True)
        idx = jnp.min(jnp.where(g == mx, lane, LANES), axis=-1, keepdims=True)
        out = jnp.where(lane == k, idx, out)
        g = jnp.where(lane == idx, NEG_INF, g)
    sel_ref[0] = out


def _block_gate(page_table, page_sum2d, q3, n_blocks):
    db = q3.shape[0]
    assert n_blocks <= LANES and n_blocks >= MOBA_TOPK
    grid_spec = pltpu.PrefetchScalarGridSpec(
        num_scalar_prefetch=1,
        grid=(db,),
        in_specs=[pl.BlockSpec(page_sum2d.shape, lambda i, pt: (0, 0)),
                  pl.BlockSpec((1, 1, q3.shape[2]), lambda i, pt: (i, 0, 0))],
        out_specs=pl.BlockSpec((1, N_ATTN_HEADS, LANES), lambda i, pt: (i, 0, 0)),
        scratch_shapes=[pltpu.VMEM((LANES, page_sum2d.shape[1]), F32)],
    )
    return pl.pallas_call(
        functools.partial(_block_gate_kernel, n_blocks=n_blocks),
        grid_spec=grid_spec,
        out_shape=jax.ShapeDtypeStruct((db, N_ATTN_HEADS, LANES), jnp.int32),
        compiler_params=_cparams(("arbitrary",)),
        name="block_gate",
    )(page_table, page_sum2d, q3)


_PAGES_PER_BLOCK = MOBA_BLOCK // PAGE_SIZE
_SEL_PAGES = MOBA_TOPK * _PAGES_PER_BLOCK


def _sample_page_copies(pt_ref, sel_ref, k_hbm, v_hbm, kbuf, vbuf, sem, bb, par):
    out = []
    for h in range(N_ATTN_HEADS):
        for kt in range(MOBA_TOPK):
            blk = sel_ref[bb, h * MOBA_TOPK + kt]
            for pp in range(_PAGES_PER_BLOCK):
                page = pt_ref[bb, blk * _PAGES_PER_BLOCK + pp]
                slot = h * _SEL_PAGES + kt * _PAGES_PER_BLOCK + pp
                out.append(pltpu.make_async_copy(k_hbm.at[page, h], kbuf.at[par, slot], sem.at[0, par]))
                out.append(pltpu.make_async_copy(v_hbm.at[page, h], vbuf.at[par, slot], sem.at[1, par]))
    return out


def _col_from_row(row):
    n = row.shape[1]
    r = lax.broadcasted_iota(jnp.int32, (n, n), 0)
    c = lax.broadcasted_iota(jnp.int32, (n, n), 1)
    return jnp.sum(jnp.where(r == c, jnp.broadcast_to(row, (n, n)), 0.0), axis=-1, keepdims=True)


def _row_from_col(col):
    n = col.shape[0]
    r = lax.broadcasted_iota(jnp.int32, (n, n), 0)
    c = lax.broadcasted_iota(jnp.int32, (n, n), 1)
    return jnp.sum(jnp.where(r == c, jnp.broadcast_to(col, (n, n)), 0.0), axis=0, keepdims=True)


def _moba_sample_kernel(pt_ref, sel_ref, tbl_ref, q_ref, kn_ref, vn_ref, k_hbm, v_hbm, o_ref,
                        kbuf, vbuf, bias_s, sem, *, past_len):
    b = pl.program_id(0)
    nb = pl.num_programs(0)
    par = b % 2
    scale = ATTN_HEAD_DIM ** -0.5
    copies = functools.partial(_sample_page_copies, pt_ref, sel_ref, k_hbm, v_hbm, kbuf, vbuf, sem)

    @pl.when(b == 0)
    def _():
        for c in copies(0, 0):
            c.start()

    @pl.when(b + 1 < nb)
    def _():
        for c in copies(b + 1, 1 - par):
            c.start()

    for c in copies(b, par):
        c.wait()

    sub = lax.broadcasted_iota(jnp.int32, (_PAGES_PER_BLOCK, PAGE_SIZE), 0)
    lane = lax.broadcasted_iota(jnp.int32, (_PAGES_PER_BLOCK, PAGE_SIZE), 1)
    for h in range(N_ATTN_HEADS):
        q = q_ref[0, h:h + 1, :]
        q_col = _col_from_row(q)
        for kt in range(MOBA_TOPK):
            pos0 = sel_ref[b, h * MOBA_TOPK + kt] * MOBA_BLOCK
            near = past_len - pos0 - (MOBA_BLOCK - 1) < MAX_DISTANCE
            rows = slice(kt * _PAGES_PER_BLOCK, (kt + 1) * _PAGES_PER_BLOCK)

            @pl.when(near)
            def _():
                dist = jnp.maximum(past_len - (pos0 + sub * PAGE_SIZE + lane), 0)
                bias_s[rows, :] = _bias_from_bucket(_t5_bucket(dist), tbl_ref, h)

            @pl.when(jnp.logical_not(near))
            def _():
                bias_s[rows, :] = jnp.full((_PAGES_PER_BLOCK, PAGE_SIZE), tbl_ref[N_BUCKETS - 1, h], F32)

        s = jnp.concatenate([jnp.sum(kbuf[par, h * _SEL_PAGES + j] * q_col, axis=0, keepdims=True)
                             for j in range(_SEL_PAGES)], axis=0) * scale + bias_s[0:_SEL_PAGES, :]
        s_new = jnp.sum(kn_ref[0, h:h + 1, :] * q, axis=-1, keepdims=True) * scale + tbl_ref[0, h]
        m = jnp.maximum(jnp.max(jnp.max(s, axis=-1, keepdims=True), axis=0, keepdims=True), s_new)
        p = jnp.exp(s - m)
        p_new = jnp.exp(s_new - m)
        den = jnp.sum(jnp.sum(p, axis=-1, keepdims=True), axis=0, keepdims=True) + p_new
        pv = vbuf[par, h * _SEL_PAGES] * p[0:1, :]
        for j in range(1, _SEL_PAGES):
            pv = pv + vbuf[par, h * _SEL_PAGES + j] * p[j:j + 1, :]
        num = _row_from_col(jnp.sum(pv, axis=-1, keepdims=True)) + p_new * vn_ref[0, h:h + 1, :]
        o_ref[0, h:h + 1, :] = num / den


def _moba_sample(page_table, sel, table, q3, k3, v3, cache_kt, cache_vt, past_len):
    db, n_h, dh = q3.shape
    assert cache_kt.shape[1:] == (n_h, dh, PAGE_SIZE) and PAGE_SIZE == LANES and _SEL_PAGES <= SUBLANES
    vec = pl.BlockSpec((1, n_h, dh), lambda i, pt, sl: (i, 0, 0))
    grid_spec = pltpu.PrefetchScalarGridSpec(
        num_scalar_prefetch=2,
        grid=(db,),
        in_specs=[pl.BlockSpec(memory_space=pltpu.SMEM), vec, vec, vec,
                  pl.BlockSpec(memory_space=pl.ANY), pl.BlockSpec(memory_space=pl.ANY)],
        out_specs=vec,
        scratch_shapes=[pltpu.VMEM((2, n_h * _SEL_PAGES, dh, PAGE_SIZE), F32),
                        pltpu.VMEM((2, n_h * _SEL_PAGES, dh, PAGE_SIZE), F32),
                        pltpu.VMEM((SUBLANES, PAGE_SIZE), F32),
                        pltpu.SemaphoreType.DMA((2, 2))],
    )
    return pl.pallas_call(
        functools.partial(_moba_sample_kernel, past_len=past_len),
        grid_spec=grid_spec,
        out_shape=jax.ShapeDtypeStruct((db, n_h, dh), F32),
        compiler_params=_cparams(("arbitrary",)),
        name="moba_sample",
    )(page_table, sel, table, q3, k3, v3, cache_kt, cache_vt)


def _mlstm_step_kernel(qk_ref, cs_ref, v_ref, o_ref, g_ref, cw_ref, cb_ref, bg_ref,
                       c_ref, n_ref, m_ref, mem_ref, c_out, n_out, m_out):
    dh = M_HEAD_DIM
    hist = CONV_WIDTH - 1
    y = cb_ref[...] + qk_ref[0] * cw_ref[hist:hist + 1, :]
    for j in range(hist):
        y = y + cs_ref[0, j:j + 1, :] * cw_ref[j:j + 1, :]
    y = y * _sigmoid(y)
    pre = g_ref[0] + bg_ref[...]
    row = lax.broadcasted_iota(jnp.int32, (dh, dh), 0)
    col = lax.broadcasted_iota(jnp.int32, (dh, dh), 1)
    lane = lax.broadcasted_iota(jnp.int32, (1, LANES), 1)
    m_all = jnp.zeros((1, LANES), F32)
    for h in range(N_M_HEADS):
        sl = slice(h * dh, (h + 1) * dh)
        q = y[:, sl]
        k = y[:, M_WIDTH + h * dh:M_WIDTH + (h + 1) * dh] * (dh ** -0.5)
        v = v_ref[0, :, sl]
        i_t = pre[:, h:h + 1]
        logf = _log_sigmoid(pre[:, N_M_HEADS + h:N_M_HEADS + h + 1])
        c_prev = c_ref[0, h]
        n_prev = n_ref[0, h:h + 1, :]
        m_prev = m_ref[0, :, h:h + 1]
        inter = logf + m_prev
        m_t = jnp.maximum(inter, i_t)
        w_inter = jnp.exp(inter - m_t)
        s = jnp.sum(q * k, axis=-1, keepdims=True) * jnp.exp(i_t - m_t)
        cq = _dot_nt(jnp.broadcast_to(q, (SUBLANES, dh)), c_prev, precision=HIGHEST)[0:1, :]
        num = w_inter * cq + s * v
        den = w_inter * jnp.sum(n_prev * q, axis=-1, keepdims=True) + s
        hh = num / jnp.maximum(jnp.abs(den), jnp.exp(-m_t))
        mem_ref[0, :, sl] = _sigmoid(o_ref[0, :, sl]) * hh
        wc = jnp.exp(inter - m_t)
        ws = jnp.exp(i_t - m_t)
        v_col = jnp.sum(jnp.where(row == col, jnp.broadcast_to(v, (dh, dh)), 0.0), axis=-1, keepdims=True)
        c_out[0, h] = wc * c_prev + (ws * v_col) * k
        n_out[0, h:h + 1, :] = wc * n_prev + ws * k
        m_all = jnp.where(lane == h, m_t, m_all)
    m_out[0] = m_all


def _mlstm_step(mqk, cstate, mv, mo, gates, conv_w, conv_b, bg_row, c0, n0, m0):
    db = mqk.shape[0]
    r3 = lambda w: pl.BlockSpec((1, 1, w), lambda i: (i, 0, 0))
    return pl.pallas_call(
        _mlstm_step_kernel,
        grid=(db,),
        in_specs=[r3(2 * M_WIDTH),
                  pl.BlockSpec((1, CONV_WIDTH - 1, 2 * M_WIDTH), lambda i: (i, 0, 0)),
                  r3(M_WIDTH), r3(M_WIDTH), r3(LANES),
                  pl.BlockSpec((CONV_WIDTH, 2 * M_WIDTH), lambda i: (0, 0)),
                  pl.BlockSpec((1, 2 * M_WIDTH), lambda i: (0, 0)),
                  pl.BlockSpec((1, LANES), lambda i: (0, 0)),
                  pl.BlockSpec((1, N_M_HEADS, M_HEAD_DIM, M_HEAD_DIM), lambda i: (i, 0, 0, 0)),
                  pl.BlockSpec((1, N_M_HEADS, M_HEAD_DIM), lambda i: (i, 0, 0)),
                  pl.BlockSpec((1, 1, N_M_HEADS), lambda i: (i, 0, 0))],
        out_specs=[r3(M_WIDTH),
                   pl.BlockSpec((1, N_M_HEADS, M_HEAD_DIM, M_HEAD_DIM), lambda i: (i, 0, 0, 0)),
                   pl.BlockSpec((1, N_M_HEADS, M_HEAD_DIM), lambda i: (i, 0, 0)),
                   r3(LANES)],
        out_shape=[jax.ShapeDtypeStruct((db, 1, M_WIDTH), F32),
                   jax.ShapeDtypeStruct((db, N_M_HEADS, M_HEAD_DIM, M_HEAD_DIM), F32),
                   jax.ShapeDtypeStruct((db, N_M_HEADS, M_HEAD_DIM), F32),
                   jax.ShapeDtypeStruct((db, 1, LANES), F32)],
        compiler_params=_cparams(("parallel",)),
        name="mlstm_step",
    )(mqk, cstate, mv, mo, gates, conv_w, conv_b, bg_row, c0, n0, m0)


def _pad_rows(x, mult):
    r = x.shape[0]
    rp = -(-r // mult) * mult
    return x if rp == r else jnp.pad(x, ((0, rp - r), (0, 0)))


def kernel(x_prompt, x_sample, cache_k, cache_v, page_table, state_C, state_n, state_m, state_conv,
           c_prompt, c_sample, rel_bias_table, w_ada, b_ada, w_in, b_gate, conv_w, conv_b,
           beta_attn, beta_mlstm, w_out, ln1_g, ln1_b, w_query, sub_keys, expert_u, expert_v,
           ln2_g, ln2_b):
    assert w_ada.shape[0] == DEPTH == 1
    B, S, D = x_prompt.shape
    DB, T, _ = x_sample.shape
    assert T == 1
    H, dh = N_ATTN_HEADS, ATTN_HEAD_DIM
    past_len = page_table.shape[1] * PAGE_SIZE
    assert past_len % MOBA_BLOCK == 0
    l = 0

    gate_cols = 2 * N_M_HEADS
    w_in_p = jnp.pad(w_in[l], ((0, 0), (0, LANES - gate_cols))).astype(BF16)
    w_out_b = w_out[l].astype(BF16)
    wq_t = w_query[l].T.astype(BF16)
    u_b = expert_u[l].astype(BF16)
    vt_b = expert_v[l].T.astype(BF16)
    beta = jnp.concatenate([beta_attn[l], beta_mlstm[l]])[None, :]
    bg = b_gate[l]
    bg8 = jnp.broadcast_to(bg[:, None], (gate_cols, LANES))
    bg_row = jnp.pad(bg, (0, LANES - gate_cols))[None, :]
    cw, cb = conv_w[l], conv_b[l][None, :]
    table = rel_bias_table
    lg1, lb1, lg2, lb2 = ln1_g[l][None, :], ln1_b[l][None, :], ln2_g[l][None, :], ln2_b[l][None, :]

    mod = _ada(jnp.concatenate([c_prompt, c_sample], axis=0), w_ada[l], b_ada[l])
    sh1, sc1, g1, sh2, sc2, g2 = [mod[:, i * D:(i + 1) * D] for i in range(6)]
    sc1, sc2 = 1.0 + sc1, 1.0 + sc2
    pm = lambda t: t[:B][:, None, :]
    sm = lambda t: t[B:][None, :, :]

    xp2 = x_prompt.reshape(B * S, D)
    w_in_tb = w_in[l].T[:3 * ATTN_WIDTH].astype(BF16)
    aqt, ak, akt, avt, avb, mqk, mv, mo, gates = _inproj_prompt(x_prompt, pm(sc1), pm(sh1), w_in_p, w_in_tb)
    attn = _moba_prompt(aqt, ak.reshape(B, S, ATTN_WIDTH), avb, table, _bias_tiles(table))
    mqk3 = mqk.reshape(B, S, 2 * M_WIDTH)
    mem, c_p, n_p, m_p = _mlstm_prompt(mqk3, mv.reshape(B, S, M_WIDTH), mo.reshape(B, S, M_WIDTH),
                                       gates.reshape(B, S, LANES), cw, cb, bg8)
    x1, h2 = _outproj(attn, mem.reshape(B * S, M_WIDTH), xp2, beta, w_out_b,
                      pm(g1), pm(sc2), pm(sh2), lg1, lb1, 256, S)
    f = _peer(h2, wq_t, sub_keys[l], u_b, vt_b, 256, 512)
    y_prompt = _final(x1, f, pm(g2), lg2, lb2, 512, S).reshape(B, S, D)
    from_t = lambda t: jnp.transpose(t.reshape(B, H, dh, S), (0, 3, 1, 2))[None]
    k_prompt = from_t(akt)
    v_prompt = from_t(avt)
    conv_prompt = mqk3[:, S - (CONV_WIDTH - 1):, :][None]

    xs2 = x_sample.reshape(DB, D)
    saq, sak, sav, smqk, smv, smo, sgates = _inproj(xs2, sm(sc1), sm(sh1), w_in_p, DB, 1)
    cache_kt = jnp.transpose(cache_k[l], (0, 2, 3, 1))
    cache_vt = jnp.transpose(cache_v[l], (0, 2, 3, 1))
    psum = _page_sums(cache_kt)
    n_blocks = past_len // MOBA_BLOCK
    sel = _block_gate(page_table, psum, saq.reshape(DB, 1, H * dh), n_blocks)
    sel = sel[:, :, :MOBA_TOPK].reshape(DB, H * MOBA_TOPK)
    h3 = lambda t: t.reshape(DB, H, dh)
    s_attn = _moba_sample(page_table, sel, table, h3(saq), h3(sak), h3(sav), cache_kt, cache_vt, past_len)
    s_mem, c_s, n_s, m_s = _mlstm_step(
        smqk.reshape(DB, 1, 2 * M_WIDTH), state_conv[l], smv.reshape(DB, 1, M_WIDTH),
        smo.reshape(DB, 1, M_WIDTH), sgates.reshape(DB, 1, LANES), cw, cb, bg_row,
        state_C[l], state_n[l], state_m[l].reshape(DB, 1, N_M_HEADS))
    sx1, sh2_ = _outproj(s_attn.reshape(DB, H * dh), s_mem.reshape(DB, M_WIDTH), xs2, beta, w_out_b,
                         sm(g1), sm(sc2), sm(sh2), lg1, lb1, DB, 1)
    sf = _peer(_pad_rows(sh2_, LANES), wq_t, sub_keys[l], u_b, vt_b, LANES, LANES)[:DB]
    y_sample = _final(sx1, sf, sm(g2), lg2, lb2, DB, 1).reshape(DB, 1, D)
    conv_sample = jnp.concatenate([state_conv[l][:, 1:, :], smqk.reshape(DB, 1, 2 * M_WIDTH)], axis=1)[None]

    return (y_prompt, y_sample,
            k_prompt, v_prompt, c_p[None], n_p[None], m_p[:, :N_M_HEADS, 0][None], conv_prompt,
            sak.reshape(1, DB, 1, H, dh), sav.reshape(1, DB, 1, H, dh),
            c_s[None], n_s[None], m_s[:, 0, :N_M_HEADS][None], conv_sample)
```

```python
import functools
import math

import numpy as np
import jax
import jax.numpy as jnp
from jax import lax
from jax.experimental import pallas as pl
from jax.experimental.pallas import tpu as pltpu

F32 = jnp.float32
BF16 = jnp.bfloat16
NEG_INF = float("-inf")
HIGHEST = lax.Precision.HIGHEST

N_ATTN_HEADS = 8
ATTN_HEAD_DIM = 64
ATTN_WIDTH = N_ATTN_HEADS * ATTN_HEAD_DIM
MOBA_BLOCK = 256
MOBA_TOPK = 3
PAGE_SIZE = 128
N_BUCKETS = 32
MAX_DISTANCE = 128
N_M_HEADS = 4
M_HEAD_DIM = 128
M_WIDTH = N_M_HEADS * M_HEAD_DIM
CONV_WIDTH = 4
M_CHUNK = 128
N_KEYS = 128
PEER_HEADS = 8
PEER_KEY_DIM = 256
PEER_TOPK = 16
LN_EPS = 1e-5
DEPTH = 1
ALPHA = (2.0 * DEPTH) ** 0.25

LANES = 128
SUBLANES = 8
VMEM_LIMIT = 56 * 1024 * 1024


def _cparams(sem, flags=None):
    return pltpu.CompilerParams(dimension_semantics=sem, vmem_limit_bytes=VMEM_LIMIT, flags=flags)


def _bucket_thresholds():
    max_exact = N_BUCKETS // 2
    d = np.arange(0, MAX_DISTANCE + 1)
    far = max_exact + (np.log(np.maximum(d, 1) / max_exact) / math.log(MAX_DISTANCE / max_exact)
                       * (N_BUCKETS - max_exact)).astype(np.int64)
    bucket = np.where(d < max_exact, d, np.minimum(far, N_BUCKETS - 1))
    assert np.all(np.diff(bucket) >= 0) and bucket[-1] == N_BUCKETS - 1
    return tuple(int(np.argmax(bucket >= k)) for k in range(max_exact + 1, N_BUCKETS))


_BUCKET_THRESHOLDS = _bucket_thresholds()


def _t5_bucket(dist):
    max_exact = N_BUCKETS // 2
    far = jnp.full(dist.shape, max_exact, jnp.int32)
    for t in _BUCKET_THRESHOLDS:
        far = far + (dist >= t).astype(jnp.int32)
    return jnp.where(dist < max_exact, dist, far)


def _bias_from_bucket(bucket, tbl_ref, h):
    out = jnp.zeros(bucket.shape, F32)
    for j in range(N_BUCKETS):
        out = jnp.where(bucket == j, tbl_ref[j, h], out)
    return out


def _standardize(x):
    mu = jnp.mean(x, axis=-1, keepdims=True)
    xc = x - mu
    var = jnp.mean(xc * xc, axis=-1, keepdims=True)
    return xc * lax.rsqrt(var + LN_EPS)


def _sigmoid(x):
    return 1.0 / (1.0 + jnp.exp(-x))


def _log_sigmoid(x):
    return jnp.minimum(x, 0.0) - jnp.log1p(jnp.exp(-jnp.abs(x)))


def _gelu_tanh(x):
    c = math.sqrt(2.0 / math.pi)
    hx = 0.5 * x
    return hx + hx * jnp.tanh(x * (c + (c * 0.044715) * (x * x)))


def _dot_nt(a, b, **kw):
    return lax.dot_general(a, b, (((1,), (1,)), ((), ())), preferred_element_type=F32, **kw)


def _dot(a, b, **kw):
    return jnp.dot(a, b, preferred_element_type=F32, **kw)


def _ada_kernel(c_ref, w_ref, b_ref, o_ref):
    c = c_ref[...]
    s = c * _sigmoid(c)
    o_ref[...] = _dot(s, w_ref[...], precision=HIGHEST) + b_ref[...]


def _ada(c_all, w_ada, b_ada):
    n, d = c_all.shape
    n_out = w_ada.shape[1]
    tn = 1024
    return pl.pallas_call(
        _ada_kernel,
        grid=(n_out // tn,),
        in_specs=[pl.BlockSpec((n, d), lambda j: (0, 0)),
                  pl.BlockSpec((d, tn), lambda j: (0, j)),
                  pl.BlockSpec((1, tn), lambda j: (0, j))],
        out_specs=pl.BlockSpec((n, tn), lambda j: (0, j)),
        out_shape=jax.ShapeDtypeStruct((n, n_out), F32),
        compiler_params=_cparams(("parallel",)),
        name="ada_mod",
    )(c_all, w_ada, b_ada.reshape(1, n_out))


_PROJ_GROUPS = (ATTN_WIDTH, ATTN_WIDTH, ATTN_WIDTH, 2 * M_WIDTH, M_WIDTH, M_WIDTH, LANES)
_PROJ_OFFS = tuple(int(v) for v in np.cumsum((0,) + _PROJ_GROUPS))


def _inproj_kernel(x_ref, sc_ref, sh_ref, w_ref, *o_refs):
    h = _standardize(x_ref[...]) * sc_ref[0] + sh_ref[0]
    hb = h.astype(BF16)
    for g, o_ref in enumerate(o_refs):
        o_ref[...] = _dot(hb, w_ref[:, _PROJ_OFFS[g]:_PROJ_OFFS[g + 1]])


def _inproj(x2d, sc3, sh3, w_in_b, tm, rows_per_mod):
    r, d = x2d.shape
    m = sc3.shape[1]
    if m == 1:
        mod_map = lambda i: ((i * tm) // rows_per_mod, 0, 0)
    else:
        mod_map = lambda i: (i, 0, 0)
    return pl.pallas_call(
        _inproj_kernel,
        grid=(r // tm,),
        in_specs=[pl.BlockSpec((tm, d), lambda i: (i, 0)),
                  pl.BlockSpec((1, m, d), mod_map),
                  pl.BlockSpec((1, m, d), mod_map),
                  pl.BlockSpec(w_in_b.shape, lambda i: (0, 0))],
        out_specs=[pl.BlockSpec((tm, g), lambda i: (i, 0)) for g in _PROJ_GROUPS],
        out_shape=[jax.ShapeDtypeStruct((r, g), F32) for g in _PROJ_GROUPS],
        compiler_params=_cparams(("parallel",)),
        name="inproj",
    )(x2d, sc3, sh3, w_in_b)


def _inproj_prompt_kernel(x_ref, sc_ref, sh_ref, w_ref, wt_ref, qt_ref, k_ref, kt_ref, vt_ref, vb_ref,
                          mqk_ref, mv_ref, mo_ref, g_ref):
    h = _standardize(x_ref[...]) * sc_ref[0] + sh_ref[0]
    hb = h.astype(BF16)
    aw = ATTN_WIDTH
    k_ref[...] = _dot(hb, w_ref[:, _PROJ_OFFS[1]:_PROJ_OFFS[2]])
    for g, o_ref in ((3, mqk_ref), (4, mv_ref), (5, mo_ref), (6, g_ref)):
        o_ref[...] = _dot(hb, w_ref[:, _PROJ_OFFS[g]:_PROJ_OFFS[g + 1]])
    qt_ref[0] = _dot_nt(wt_ref[0:aw, :], hb)
    kt_ref[0] = _dot_nt(wt_ref[aw:2 * aw, :], hb)
    vt = _dot_nt(wt_ref[2 * aw:3 * aw, :], hb)
    vt_ref[0] = vt
    vb_ref[0, 0] = vt


def _inproj_prompt(x3, sc3, sh3, w_in_b, w_in_tb):
    b, s, d = x3.shape
    tm = MOBA_BLOCK
    nt = s // tm
    r = b * s
    aw = ATTN_WIDTH
    assert s % tm == 0
    row = lambda w: pl.BlockSpec((tm, w), lambda i: (i, 0))
    tr = pl.BlockSpec((1, aw, tm), lambda i: (i // nt, 0, i % nt))
    mod = pl.BlockSpec((1, 1, d), lambda i: (i // nt, 0, 0))
    return pl.pallas_call(
        _inproj_prompt_kernel,
        grid=(r // tm,),
        in_specs=[row(d), mod, mod,
                  pl.BlockSpec(w_in_b.shape, lambda i: (0, 0)),
                  pl.BlockSpec(w_in_tb.shape, lambda i: (0, 0))],
        out_specs=[tr, row(aw), tr, tr,
                   pl.BlockSpec((1, 1, aw, tm), lambda i: (i // nt, i % nt, 0, 0)),
                   row(2 * M_WIDTH), row(M_WIDTH), row(M_WIDTH), row(LANES)],
        out_shape=[jax.ShapeDtypeStruct((b, aw, s), F32), jax.ShapeDtypeStruct((r, aw), F32),
                   jax.ShapeDtypeStruct((b, aw, s), F32), jax.ShapeDtypeStruct((b, aw, s), F32),
                   jax.ShapeDtypeStruct((b, nt, aw, tm), F32),
                   jax.ShapeDtypeStruct((r, 2 * M_WIDTH), F32), jax.ShapeDtypeStruct((r, M_WIDTH), F32),
                   jax.ShapeDtypeStruct((r, M_WIDTH), F32), jax.ShapeDtypeStruct((r, LANES), F32)],
        compiler_params=_cparams(("parallel",)),
        name="inproj_prompt",
    )(x3.reshape(r, d), sc3, sh3, w_in_b, w_in_tb)


def _bias_tiles_kernel(tbl_ref, o_ref):
    h = pl.program_id(0)
    key = lax.broadcasted_iota(jnp.int32, (MOBA_BLOCK, MOBA_BLOCK), 0)
    qry = lax.broadcasted_iota(jnp.int32, (MOBA_BLOCK, MOBA_BLOCK), 1)
    for t in range(2):
        dist = jnp.maximum(qry - key + t * MOBA_BLOCK, 0)
        o_ref[0, t] = _bias_from_bucket(_t5_bucket(dist), tbl_ref, h)


def _bias_tiles(table):
    n_h = table.shape[1]
    return pl.pallas_call(
        _bias_tiles_kernel,
        grid=(n_h,),
        in_specs=[pl.BlockSpec(memory_space=pltpu.SMEM)],
        out_specs=pl.BlockSpec((1, 2, MOBA_BLOCK, MOBA_BLOCK), lambda h: (h, 0, 0, 0)),
        out_shape=jax.ShapeDtypeStruct((n_h, 2, MOBA_BLOCK, MOBA_BLOCK), F32),
        compiler_params=_cparams(("parallel",)),
        name="moba_bias_tiles",
    )(table)


_HEADS_PER_STEP = 4
_HEAD_GROUP_COLS = _HEADS_PER_STEP * ATTN_HEAD_DIM


def _moba_prompt_kernel(tbl_ref, qt_ref, k_ref, vb_ref, bias_ref, o_ref, selb_s, *, n_blocks):
    hp = pl.program_id(1)
    ob = pl.program_id(2)
    blk = MOBA_BLOCK
    dh = ATTN_HEAD_DIM
    gc = _HEAD_GROUP_COLS
    scale = dh ** -0.5
    heads = range(_HEADS_PER_STEP)

    kmean = jnp.concatenate(
        [jnp.sum(k_ref[0, n * blk:(n + 1) * blk, :], axis=0, keepdims=True) * (1.0 / blk)
         for n in range(n_blocks)], axis=0)
    sub = lax.broadcasted_iota(jnp.int32, (n_blocks, blk), 0)
    key = lax.broadcasted_iota(jnp.int32, (blk, blk), 0)
    qry = lax.broadcasted_iota(jnp.int32, (blk, blk), 1)
    start = pl.multiple_of(ob * blk, blk)
    fsub = lax.broadcasted_iota(jnp.int32, (gc, blk), 0)
    k_own = k_ref[0, pl.ds(start, blk), :].astype(BF16)
    c_fars = [tbl_ref[N_BUCKETS - 1, hp * _HEADS_PER_STEP + j] for j in heads]

    qzs = [jnp.where((fsub >= j * dh) & (fsub < (j + 1) * dh), qt_ref[0], 0.0) for j in heads]
    qzbs = [qz.astype(BF16) for qz in qzs]
    gates = [_dot(kmean, qz, precision=HIGHEST) for qz in qzs]
    qk_own = [_dot(k_own, qzb) for qzb in qzbs]
    for j in heads:
        g = jnp.where(sub < ob, gates[j], NEG_INF)
        sel = jnp.zeros((n_blocks, blk), F32)
        for _ in range(MOBA_TOPK):
            mx = jnp.max(g, axis=0, keepdims=True)
            idx = jnp.min(jnp.where(g == mx, sub, n_blocks), axis=0, keepdims=True)
            hit = sub == idx
            sel = jnp.where(hit, 1.0, sel)
            g = jnp.where(hit, NEG_INF, g)
        sel = jnp.where(sub < ob, sel, 0.0)
        for n in range(n_blocks):
            selb_s[j, n] = jnp.broadcast_to(sel[n:n + 1, :], (SUBLANES, blk))
    own = []
    for j in heads:
        s = jnp.where(key <= qry, qk_own[j] * scale + bias_ref[j, 0], NEG_INF)
        m0 = jnp.max(s, axis=0, keepdims=True)
        p = jnp.exp(s - m0)
        own.append((m0, jnp.sum(p, axis=0, keepdims=True), p.astype(BF16)))
    init = tuple((own[j][0], own[j][1], _dot(vb_ref[0, ob, j * dh:(j + 1) * dh, :].astype(BF16), own[j][2]))
                 for j in heads)

    def body(n, carry):
        st = pl.multiple_of(n * blk, blk)
        k_n = k_ref[0, pl.ds(st, blk), :].astype(BF16)
        qk = [_dot(k_n, qzbs[j]) for j in heads]
        stats = []
        for j in heads:
            m, l, _ = carry[j]
            bias = jnp.where(n == ob - 1, bias_ref[j, 1], c_fars[j])
            s = jnp.where(selb_s[j, n][0:1, :] > 0.5, qk[j] * scale + bias, NEG_INF)
            m_new = jnp.maximum(m, jnp.max(s, axis=0, keepdims=True))
            a = jnp.exp(m - m_new)
            p = jnp.exp(s - m_new)
            stats.append((m_new, a * l + jnp.sum(p, axis=0, keepdims=True), a, p.astype(BF16)))
        pv = [_dot(vb_ref[0, n, j * dh:(j + 1) * dh, :].astype(BF16), stats[j][3]) for j in heads]
        return tuple((stats[j][0], stats[j][1], stats[j][2] * carry[j][2] + pv[j]) for j in heads)

    final = lax.fori_loop(0, ob, body, init)
    for j in heads:
        _, l, acc = final[j]
        o_ref[0, j * dh:(j + 1) * dh, :] = acc / l


def _moba_prompt(qt, k, vb, table, bias_tiles):
    b, aw, s = qt.shape
    blk = MOBA_BLOCK
    n_blocks = s // blk
    hps = _HEADS_PER_STEP
    gc = _HEAD_GROUP_COLS
    assert s % blk == 0 and MOBA_TOPK <= n_blocks <= SUBLANES and aw % gc == 0 and gc % LANES == 0
    return pl.pallas_call(
        functools.partial(_moba_prompt_kernel, n_blocks=n_blocks),
        grid=(b, aw // gc, n_blocks),
        in_specs=[pl.BlockSpec(memory_space=pltpu.SMEM),
                  pl.BlockSpec((1, gc, blk), lambda i, g, j: (i, g, j)),
                  pl.BlockSpec((1, s, gc), lambda i, g, j: (i, 0, g)),
                  pl.BlockSpec((1, n_blocks, gc, blk), lambda i, g, j: (i, 0, g, 0)),
                  pl.BlockSpec((hps, 2, blk, blk), lambda i, g, j: (g, 0, 0, 0))],
        out_specs=pl.BlockSpec((1, gc, blk), lambda i, g, j: (i, g, j)),
        out_shape=jax.ShapeDtypeStruct((b, aw, s), F32),
        scratch_shapes=[pltpu.VMEM((hps, n_blocks, SUBLANES, blk), F32)],
        compiler_params=_cparams(("parallel", "parallel", "parallel")),
        name="moba_prompt",
    )(table, qt, k, vb, bias_tiles)


_CONV_PAD = SUBLANES


def _cumsum_lanes(x):
    lane = lax.broadcasted_iota(jnp.int32, x.shape, 1)
    k = 1
    while k < x.shape[-1]:
        x = x + jnp.where(lane >= k, pltpu.roll(x, k, 1), 0.0)
        k *= 2
    return x


def _mlstm_prompt_kernel(q_ref, k_ref, v_ref, o_ref, g_ref, cw_ref, cb_ref, bg_ref,
                         mem_ref, c_out, n_out, m_out, xq_s, xk_s, c_s, n_s, m_s):
    ci = pl.program_id(1)
    L = M_CHUNK
    dh = M_HEAD_DIM
    pad = _CONV_PAD
    hist = CONV_WIDTH - 1

    @pl.when(ci == 0)
    def _():
        xq_s[0:pad, :] = jnp.zeros((pad, M_WIDTH), F32)
        xk_s[0:pad, :] = jnp.zeros((pad, M_WIDTH), F32)
        c_s[...] = jnp.zeros(c_s.shape, F32)
        n_s[...] = jnp.zeros(n_s.shape, F32)
        m_s[...] = jnp.zeros(m_s.shape, F32)

    xq_s[pad:pad + L, :] = q_ref[0]
    xk_s[pad:pad + L, :] = k_ref[0]

    def conv(x_s, col0):
        y = cb_ref[:, col0:col0 + M_WIDTH]
        for j in range(CONV_WIDTH):
            y = y + x_s[pad - hist + j:pad - hist + j + L, :] * cw_ref[j:j + 1, col0:col0 + M_WIDTH]
        return y * _sigmoid(y)

    qc = conv(xq_s, 0)
    kc = conv(xk_s, M_WIDTH) * (dh ** -0.5)
    tq = xq_s[pad + L - hist:pad + L, :]
    tk = xk_s[pad + L - hist:pad + L, :]
    xq_s[pad - hist:pad, :] = tq
    xk_s[pad - hist:pad, :] = tk

    heads = range(N_M_HEADS)
    hs = [slice(h * dh, (h + 1) * dh) for h in heads]
    qs = [qc[:, hs[h]] for h in heads]
    ks = [kc[:, hs[h]] for h in heads]
    vs = [v_ref[0, :, hs[h]] for h in heads]
    qbs = [q.astype(BF16) for q in qs]
    kbs = [k.astype(BF16) for k in ks]
    c_prevs = [c_s[h] for h in heads]
    n_prevs = [n_s[h:h + 1, :] for h in heads]
    m_prevs = [m_s[h:h + 1, 0:1] for h in heads]
    qk = [_dot_nt(qbs[h], kbs[h]) for h in heads]
    qc_prev = [_dot_nt(qbs[h], c_prevs[h].astype(BF16)) for h in heads]
    qn = [jnp.sum(qs[h] * n_prevs[h], axis=-1, keepdims=True) for h in heads]

    gt = g_ref[0].T
    pre = gt[0:2 * N_M_HEADS, :] + bg_ref[...]
    rsel = lax.broadcasted_iota(jnp.int32, pre.shape, 0) < N_M_HEADS
    cum = _cumsum_lanes(jnp.where(rsel, 0.0, _log_sigmoid(pre)))
    r8 = jnp.where(rsel, pre, cum)
    t8 = jnp.concatenate([r8, jnp.zeros((LANES - 2 * N_M_HEADS, L), F32)], axis=0).T

    row = lax.broadcasted_iota(jnp.int32, (L, L), 0)
    col = lax.broadcasted_iota(jnp.int32, (L, L), 1)
    causal = col <= row
    i_rows = [r8[h:h + 1, :] for h in heads]
    b_rows = [r8[N_M_HEADS + h:N_M_HEADS + h + 1, :] for h in heads]
    i_cols = [t8[:, h:h + 1] for h in heads]
    b_cols = [t8[:, N_M_HEADS + h:N_M_HEADS + h + 1] for h in heads]

    intra = []
    for h in heads:
        d = jnp.where(causal, b_cols[h] - b_rows[h] + i_rows[h], NEG_INF)
        inter = b_cols[h] + m_prevs[h]
        m_t = jnp.maximum(inter, jnp.max(d, axis=-1, keepdims=True))
        w_inter = jnp.exp(inter - m_t)
        s = qk[h] * jnp.exp(d - m_t)
        intra.append((m_t, w_inter, s, jnp.sum(s, axis=-1, keepdims=True)))
    sv = [_dot(intra[h][2].astype(BF16), vs[h].astype(BF16)) for h in heads]

    carry = []
    for h in heads:
        b_last = b_rows[h][:, L - 1:L]
        g_row = b_last - b_rows[h] + i_rows[h]
        g_col = b_last - b_cols[h] + i_cols[h]
        m_new = jnp.maximum(b_last + m_prevs[h], jnp.max(g_row, axis=-1, keepdims=True))
        wc = jnp.exp(b_last + m_prevs[h] - m_new)
        ws = jnp.exp(g_col - m_new)
        carry.append((m_new, wc, ws, (ws * vs[h]).T.astype(BF16)))
    vk = [_dot(carry[h][3], kbs[h]) for h in heads]

    for h in heads:
        m_t, w_inter, _, s_sum = intra[h]
        num = w_inter * qc_prev[h] + sv[h]
        den = w_inter * qn[h] + s_sum
        hh = num / jnp.maximum(jnp.abs(den), jnp.exp(-m_t))
        mem_ref[0, :, hs[h]] = _sigmoid(o_ref[0, :, hs[h]]) * hh
    for h in heads:
        m_new, wc, ws, _ = carry[h]
        c_s[h] = wc * c_prevs[h] + vk[h]
        n_s[h:h + 1, :] = wc * n_prevs[h] + jnp.sum(ws * ks[h], axis=0, keepdims=True)
        m_s[h:h + 1, :] = jnp.broadcast_to(m_new, (1, LANES))

    @pl.when(ci == pl.num_programs(1) - 1)
    def _():
        c_out[0] = c_s[...]
        n_out[0] = n_s[0:N_M_HEADS, :]
        m_out[0] = m_s[...]


def _mlstm_prompt(mqk, mv, mo, gates, conv_w, conv_b, bg8):
    b, s, _ = mv.shape
    L = M_CHUNK
    nc = s // L
    assert s % L == 0
    return pl.pallas_call(
        _mlstm_prompt_kernel,
        grid=(b, nc),
        in_specs=[pl.BlockSpec((1, L, M_WIDTH), lambda i, c: (i, c, 0)),
                  pl.BlockSpec((1, L, M_WIDTH), lambda i, c: (i, c, 1)),
                  pl.BlockSpec((1, L, M_WIDTH), lambda i, c: (i, c, 0)),
                  pl.BlockSpec((1, L, M_WIDTH), lambda i, c: (i, c, 0)),
                  pl.BlockSpec((1, L, LANES), lambda i, c: (i, c, 0)),
                  pl.BlockSpec((CONV_WIDTH, 2 * M_WIDTH), lambda i, c: (0, 0)),
                  pl.BlockSpec((1, 2 * M_WIDTH), lambda i, c: (0, 0)),
                  pl.BlockSpec((2 * N_M_HEADS, LANES), lambda i, c: (0, 0))],
        out_specs=[pl.BlockSpec((1, L, M_WIDTH), lambda i, c: (i, c, 0)),
                   pl.BlockSpec((1, N_M_HEADS, M_HEAD_DIM, M_HEAD_DIM), lambda i, c: (i, 0, 0, 0)),
                   pl.BlockSpec((1, N_M_HEADS, M_HEAD_DIM), lambda i, c: (i, 0, 0)),
                   pl.BlockSpec((1, SUBLANES, LANES), lambda i, c: (i, 0, 0))],
        out_shape=[jax.ShapeDtypeStruct((b, s, M_WIDTH), F32),
                   jax.ShapeDtypeStruct((b, N_M_HEADS, M_HEAD_DIM, M_HEAD_DIM), F32),
                   jax.ShapeDtypeStruct((b, N_M_HEADS, M_HEAD_DIM), F32),
                   jax.ShapeDtypeStruct((b, SUBLANES, LANES), F32)],
        scratch_shapes=[pltpu.VMEM((_CONV_PAD + L, M_WIDTH), F32),
                        pltpu.VMEM((_CONV_PAD + L, M_WIDTH), F32),
                        pltpu.VMEM((N_M_HEADS, M_HEAD_DIM, M_HEAD_DIM), F32),
                        pltpu.VMEM((SUBLANES, M_HEAD_DIM), F32),
                        pltpu.VMEM((SUBLANES, LANES), F32)],
        compiler_params=_cparams(("parallel", "arbitrary")),
        name="mlstm_prompt",
    )(mqk, mqk, mv, mo, gates, conv_w, conv_b, bg8)


def _outproj_kernel(a_ref, m_ref, x_ref, beta_ref, w_ref, g1_ref, sc_ref, sh_ref, lg_ref, lb_ref,
                    x1_ref, h2_ref, *, attn_feature_major):
    attn = a_ref[0].T if attn_feature_major else a_ref[...]
    mixed = jnp.concatenate([attn, m_ref[...]], axis=-1) * beta_ref[...]
    y = _dot(mixed.astype(BF16), w_ref[...])
    z = ALPHA * x_ref[...] + g1_ref[0] * y
    x1 = _standardize(z) * lg_ref[...] + lb_ref[...]
    x1_ref[...] = x1
    h2_ref[...] = (_standardize(x1) * sc_ref[0] + sh_ref[0]).astype(h2_ref.dtype)


def _outproj(attn, mem, x2d, beta, w_out_b, g1, sc2, sh2, ln_g, ln_b, tm, rows_per_mod):
    r, d = x2d.shape
    m = g1.shape[1]
    if m == 1:
        mod_map = lambda i: ((i * tm) // rows_per_mod, 0, 0)
    else:
        mod_map = lambda i: (i, 0, 0)
    vec = pl.BlockSpec((1, d), lambda i: (0, 0))
    mod = pl.BlockSpec((1, m, d), mod_map)
    feature_major = attn.ndim == 3
    if feature_major:
        nt = rows_per_mod // tm
        assert rows_per_mod % tm == 0 and attn.shape[2] == rows_per_mod
        attn_spec = pl.BlockSpec((1, attn.shape[1], tm), lambda i: (i // nt, 0, i % nt))
    else:
        attn_spec = pl.BlockSpec((tm, attn.shape[1]), lambda i: (i, 0))
    return pl.pallas_call(
        functools.partial(_outproj_kernel, attn_feature_major=feature_major),
        grid=(r // tm,),
        in_specs=[attn_spec,
                  pl.BlockSpec((tm, mem.shape[1]), lambda i: (i, 0)),
                  pl.BlockSpec((tm, d), lambda i: (i, 0)),
                  vec,
                  pl.BlockSpec(w_out_b.shape, lambda i: (0, 0)),
                  mod, mod, mod, vec, vec],
        out_specs=[pl.BlockSpec((tm, d), lambda i: (i, 0)),
                   pl.BlockSpec((tm, d), lambda i: (i, 0))],
        out_shape=[jax.ShapeDtypeStruct((r, d), F32), jax.ShapeDtypeStruct((r, d), BF16)],
        compiler_params=_cparams(("parallel",)),
        name="outproj",
    )(attn, mem, x2d, beta, w_out_b, g1, sc2, sh2, ln_g, ln_b)


def _oddeven_merge_sort_pairs(n):
    pairs = []

    def merge(lo, m, r):
        step = r * 2
        if step < m:
            merge(lo, m, step)
            merge(lo + r, m, step)
            for i in range(lo + r, lo + m - r, step):
                pairs.append((i, i + r))
        else:
            pairs.append((lo, lo + r))

    def sort(lo, m):
        if m > 1:
            h = m // 2
            sort(lo, h)
            sort(lo + h, h)
            merge(lo, m, 1)

    sort(0, n)
    return tuple(pairs)


_SORT16 = _oddeven_merge_sort_pairs(PEER_TOPK)


def _vmax(a, b):
    if a is None:
        return b
    if b is None:
        return a
    return jnp.maximum(a, b)


def _cmpx(v, i, j):
    a, b = v[i], v[j]
    if b is None:
        return
    if a is None:
        v[i], v[j] = b, None
        return
    v[i], v[j] = jnp.maximum(a, b), jnp.minimum(a, b)


def _bitonic_to_desc(v):
    n = len(v)
    d = n // 2
    while d >= 1:
        for i in range(n):
            if (i & d) == 0:
                _cmpx(v, i, i + d)
        d //= 2
    return v


def _merge_top(x, y):
    n = len(x)
    return _bitonic_to_desc([_vmax(x[i], y[n - 1 - i]) for i in range(n)])


def _top16_desc(sc):
    groups = sc.shape[0] // SUBLANES
    assert groups == PEER_TOPK
    v = [sc[g * SUBLANES:(g + 1) * SUBLANES, :] for g in range(groups)]
    for i, j in _SORT16:
        _cmpx(v, i, j)
    shift = SUBLANES // 2
    while shift >= 1:
        partner = [pltpu.roll(a, shift, 0) for a in v]
        v = _merge_top(v, partner)
        shift //= 2
    return v


def _candidate_lists(a, b):
    k = PEER_TOPK
    lists = []
    for i in range(4):
        n = k // (i + 1)
        lists.append([a[i] + b[j] for j in range(n)])
    for j in range(3):
        n = k // (j + 1)
        col = [a[i] + b[j] for i in range(4, n)]
        if col:
            lists.append(col)
    return [l + [None] * (k - len(l)) for l in lists]


_RANK_STEP = 2.0


def _prefix_count(pred, vals):
    assert len(vals) == PEER_TOPK == 16
    sel = jnp.where
    c8 = pred(vals[7])
    c4 = pred(sel(c8, vals[11], vals[3]))
    c2 = pred(sel(c8, sel(c4, vals[13], vals[9]), sel(c4, vals[5], vals[1])))
    c1 = pred(sel(c8, sel(c4, sel(c2, vals[14], vals[12]), sel(c2, vals[10], vals[8])),
                  sel(c4, sel(c2, vals[6], vals[4]), sel(c2, vals[2], vals[0]))))
    n = (sel(c8, 8 * _RANK_STEP, 0.0) + sel(c4, 4 * _RANK_STEP, 0.0)
         + sel(c2, 2 * _RANK_STEP, 0.0) + sel(c1, _RANK_STEP, 0.0))
    return sel(pred(vals[15]), 16 * _RANK_STEP, n)


def _peer_route_kernel(h_ref, wq_ref, sk_ref, cnt_ref, rk_ref, a_ref, b_ref, qt_s, sc_s, top_s, tz_s):
    tm = h_ref.shape[0]
    kd = PEER_KEY_DIM // 2
    qt_s[...] = _dot_nt(wq_ref[...], h_ref[...])

    def head(p, carry):
        scs = [_dot(sk_ref[s], qt_s[pl.ds(pl.multiple_of((2 * p + s) * kd, kd), kd), :], precision=HIGHEST)
               for s in range(2)]
        for s in range(2):
            sc_s[p, s] = scs[s]
            srt = _top16_desc(scs[s])
            for r in range(PEER_TOPK):
                top_s[p, s, r] = srt[r]
        return carry

    lax.fori_loop(0, PEER_HEADS, head, 0)
    sub = lax.broadcasted_iota(jnp.int32, (SUBLANES, tm), 0)

    def on_sublanes(s, r):
        out = top_s[0, s, r]
        for p in range(1, PEER_HEADS):
            out = jnp.where(sub == p, top_s[p, s, r], out)
        return out

    top = [[on_sublanes(s, r) for r in range(PEER_TOPK)] for s in range(2)]
    lists = _candidate_lists(top[0], top[1])
    best = lists[0]
    for other in lists[1:]:
        best = _merge_top(best, other)
    z = jnp.ones_like(best[0])
    for r in range(1, PEER_TOPK):
        z = z + jnp.exp(best[r] - best[0])
    thr = best[PEER_TOPK - 1]
    for p in range(PEER_HEADS):
        tz_s[p, 0] = jnp.broadcast_to(thr[p:p + 1, :], (SUBLANES, tm))
        tz_s[p, 1] = jnp.broadcast_to(z[p:p + 1, :], (SUBLANES, tm))

    def emit(p, carry):
        s0 = sc_s[p, 0]
        s1 = sc_s[p, 1]
        t_row = tz_s[p, 0][0:1, :]
        z_row = tz_s[p, 1][0:1, :]
        b_top = [top_s[p, 1, r][0:1, :] for r in range(PEER_TOPK)]
        cnt_ref[p] = _prefix_count(lambda v: s0 + v >= t_row, b_top)
        rk_ref[p] = _prefix_count(lambda v: v > s1, b_top).astype(rk_ref.dtype)
        a_ref[p] = jnp.exp(s0 - top_s[p, 0, 0][0:1, :]) / z_row
        b_ref[p] = jnp.exp(s1 - top_s[p, 1, 0][0:1, :]).astype(b_ref.dtype)
        return carry

    lax.fori_loop(0, PEER_HEADS, emit, 0)


def _peer_route(h2, wq_t, sub_keys, tm):
    r, d = h2.shape
    assert PEER_HEADS == SUBLANES
    shp = (PEER_HEADS, N_KEYS, r)
    bspec = pl.BlockSpec((PEER_HEADS, N_KEYS, tm), lambda i: (0, 0, i))
    return pl.pallas_call(
        _peer_route_kernel,
        grid=(r // tm,),
        in_specs=[pl.BlockSpec((tm, d), lambda i: (i, 0)),
                  pl.BlockSpec(wq_t.shape, lambda i: (0, 0)),
                  pl.BlockSpec(sub_keys.shape, lambda i: (0, 0, 0))],
        out_specs=[bspec, bspec, bspec, bspec],
        out_shape=[jax.ShapeDtypeStruct(shp, F32), jax.ShapeDtypeStruct(shp, BF16),
                   jax.ShapeDtypeStruct(shp, F32), jax.ShapeDtypeStruct(shp, BF16)],
        scratch_shapes=[pltpu.VMEM((wq_t.shape[0], tm), F32),
                        pltpu.VMEM((PEER_HEADS, 2, N_KEYS, tm), F32),
                        pltpu.VMEM((PEER_HEADS, 2, PEER_TOPK, SUBLANES, tm), F32),
                        pltpu.VMEM((PEER_HEADS, 2, SUBLANES, tm), F32)],
        compiler_params=_cparams(("parallel",)),
        name="peer_route",
    )(h2, wq_t, sub_keys)


_EXPERT_CHUNK = SUBLANES * N_KEYS
_MIX_SUBTILE = 2 * SUBLANES


def _peer_mix_kernel(h_ref, u_ref, vt_ref, cnt_ref, rk_ref, a_ref, b_ref, o_ref, act_s, y_s, acc_s):
    c = pl.program_id(1)
    tm = h_ref.shape[0]

    @pl.when(c == 0)
    def _():
        acc_s[...] = jnp.zeros(acc_s.shape, F32)

    act_s[...] = _dot_nt(u_ref[...], h_ref[...])

    sub = _MIX_SUBTILE
    zero = jnp.zeros((sub, LANES), BF16)
    for ii in range(SUBLANES):
        for lc in range(tm // LANES):
            ls = slice(lc * LANES, (lc + 1) * LANES)
            cb = [jnp.broadcast_to(cnt_ref[p, ii:ii + 1, ls], (sub, LANES)).astype(BF16) for p in range(PEER_HEADS)]
            ab = [jnp.broadcast_to(a_ref[p, ii:ii + 1, ls], (sub, LANES)).astype(BF16) for p in range(PEER_HEADS)]
            for js in range(N_KEYS // sub):
                jr = slice(js * sub, (js + 1) * sub)
                terms = [jnp.maximum(jnp.minimum(ab[p] * b_ref[p, jr, ls], cb[p] - rk_ref[p, jr, ls]), zero)
                         for p in range(PEER_HEADS)]
                while len(terms) > 1:
                    terms = [terms[i] + terms[i + 1] for i in range(0, len(terms), 2)]
                w = terms[0]
                rs = slice(ii * N_KEYS + js * sub, ii * N_KEYS + (js + 1) * sub)
                y_s[rs, ls] = w * _gelu_tanh(act_s[rs, ls].astype(BF16))
    acc_s[...] += _dot(vt_ref[0], y_s[...])

    @pl.when(c == pl.num_programs(1) - 1)
    def _():
        o_ref[...] = acc_s[...].T


def _peer_mix(h2, u_b, vt_c, cnt, rk, a, b, tm):
    r, d = h2.shape
    n_exp = u_b.shape[0]
    ch = _EXPERT_CHUNK
    assert n_exp == N_KEYS * N_KEYS and vt_c.shape == (n_exp // ch, d, ch)
    row_blk = pl.BlockSpec((PEER_HEADS, SUBLANES, tm), lambda i, c: (0, c, i))
    all_blk = pl.BlockSpec((PEER_HEADS, N_KEYS, tm), lambda i, c: (0, 0, i))
    return pl.pallas_call(
        _peer_mix_kernel,
        grid=(r // tm, n_exp // ch),
        in_specs=[pl.BlockSpec((tm, d), lambda i, c: (i, 0)),
                  pl.BlockSpec((ch, d), lambda i, c: (c, 0)),
                  pl.BlockSpec((1, d, ch), lambda i, c: (c, 0, 0)),
                  row_blk, all_blk, row_blk, all_blk],
        out_specs=pl.BlockSpec((tm, d), lambda i, c: (i, 0)),
        out_shape=jax.ShapeDtypeStruct((r, d), F32),
        scratch_shapes=[pltpu.VMEM((ch, tm), F32), pltpu.VMEM((ch, tm), BF16), pltpu.VMEM((d, tm), F32)],
        compiler_params=_cparams(("parallel", "arbitrary")),
        name="peer_mix",
    )(h2, u_b, vt_c, cnt, rk, a, b)


def _peer(h2, wq_t, sub_keys, u_b, vt_b, tm_route, tm_mix):
    cnt, rk, a, b = _peer_route(h2, wq_t, sub_keys, tm_route)
    return _peer_mix(h2, u_b, vt_b, cnt, rk, a, b, tm_mix)


def _final_kernel(x_ref, f_ref, g2_ref, lg_ref, lb_ref, o_ref):
    z = ALPHA * x_ref[...] + g2_ref[0] * f_ref[...]
    o_ref[...] = _standardize(z) * lg_ref[...] + lb_ref[...]


def _final(x1, f, g2, ln_g, ln_b, tm, rows_per_mod):
    r, d = x1.shape
    m = g2.shape[1]
    if m == 1:
        mod_map = lambda i: ((i * tm) // rows_per_mod, 0, 0)
    else:
        mod_map = lambda i: (i, 0, 0)
    vec = pl.BlockSpec((1, d), lambda i: (0, 0))
    return pl.pallas_call(
        _final_kernel,
        grid=(r // tm,),
        in_specs=[pl.BlockSpec((tm, d), lambda i: (i, 0)),
                  pl.BlockSpec((tm, d), lambda i: (i, 0)),
                  pl.BlockSpec((1, m, d), mod_map), vec, vec],
        out_specs=pl.BlockSpec((tm, d), lambda i: (i, 0)),
        out_shape=jax.ShapeDtypeStruct((r, d), F32),
        compiler_params=_cparams(("parallel",)),
        name="final_norm",
    )(x1, f, g2, ln_g, ln_b)


_PAGES_PER_STEP = 32


def _page_sum_kernel(c_ref, o_ref):
    pp, n_h, dh, page = c_ref.shape
    width = n_h * dh
    lane = lax.broadcasted_iota(jnp.int32, (width, LANES), 1)
    t = jnp.zeros((width, LANES), F32)
    for pg in range(pp):
        col = jnp.sum(c_ref[pg].reshape(width, page), axis=-1, keepdims=True)
        t = jnp.where(lane == pg, col, t)
    o_ref[...] = t.T[0:pp, :]


def _page_sums(cache_t):
    n_phys, n_h, dh, page = cache_t.shape
    pp = _PAGES_PER_STEP
    assert n_phys % pp == 0 and pp <= LANES
    return pl.pallas_call(
        _page_sum_kernel,
        grid=(n_phys // pp,),
        in_specs=[pl.BlockSpec((pp, n_h, dh, page), lambda i: (i, 0, 0, 0))],
        out_specs=pl.BlockSpec((pp, n_h * dh), lambda i: (i, 0)),
        out_shape=jax.ShapeDtypeStruct((n_phys, n_h * dh), F32),
        compiler_params=_cparams(("parallel",)),
        name="page_sums",
    )(cache_t)


def _block_gate_kernel(pt_ref, ps_ref, q_ref, sel_ref, km_s, *, n_blocks):
    b = pl.program_id(0)
    ppb = MOBA_BLOCK // PAGE_SIZE
    width = ps_ref.shape[1]

    km_s[...] = jnp.zeros(km_s.shape, F32)

    def gather(n, carry):
        acc = jnp.zeros((1, width), F32)
        for j in range(ppb):
            acc = acc + ps_ref[pl.ds(pt_ref[b, n * ppb + j], 1), :]
        km_s[pl.ds(n, 1), :] = acc * (1.0 / MOBA_BLOCK)
        return carry

    lax.fori_loop(0, n_blocks, gather, 0)
    q = q_ref[0]
    sub = lax.broadcasted_iota(jnp.int32, (N_ATTN_HEADS, width), 0)
    lane_w = lax.broadcasted_iota(jnp.int32, (N_ATTN_HEADS, width), 1)
    qb = jnp.where(lane_w // ATTN_HEAD_DIM == sub, jnp.broadcast_to(q, (N_ATTN_HEADS, width)), 0.0)
    gate = _dot_nt(qb, km_s[...], precision=HIGHEST)
    lane = lax.broadcasted_iota(jnp.int32, gate.shape, 1)
    g = jnp.where(lane < n_blocks, gate, NEG_INF)
    out = jnp.zeros(gate.shape, jnp.int32)
    for k in range(MOBA_TOPK):
        mx = jnp.max(g, axis=-1, keepdims=True)
        idx = jnp.min(jnp.where(g == mx, lane, LANES), axis=-1, keepdims=True)
        out = jnp.where(lane == k, idx, out)
        g = jnp.where(lane == idx, NEG_INF, g)
    sel_ref[0] = out


def _block_gate(page_table, page_sum2d, q3, n_blocks):
    db = q3.shape[0]
    assert n_blocks <= LANES and n_blocks >= MOBA_TOPK
    grid_spec = pltpu.PrefetchScalarGridSpec(
        num_scalar_prefetch=1,
        grid=(db,),
        in_specs=[pl.BlockSpec(page_sum2d.shape, lambda i, pt: (0, 0)),
                  pl.BlockSpec((1, 1, q3.shape[2]), lambda i, pt: (i, 0, 0))],
        out_specs=pl.BlockSpec((1, N_ATTN_HEADS, LANES), lambda i, pt: (i, 0, 0)),
        scratch_shapes=[pltpu.VMEM((LANES, page_sum2d.shape[1]), F32)],
    )
    return pl.pallas_call(
        functools.partial(_block_gate_kernel, n_blocks=n_blocks),
        grid_spec=grid_spec,
        out_shape=jax.ShapeDtypeStruct((db, N_ATTN_HEADS, LANES), jnp.int32),
        compiler_params=_cparams(("arbitrary",)),
        name="block_gate",
    )(page_table, page_sum2d, q3)


_PAGES_PER_BLOCK = MOBA_BLOCK // PAGE_SIZE
_SEL_PAGES = MOBA_TOPK * _PAGES_PER_BLOCK


def _sample_page_copies(pt_ref, sel_ref, k_hbm, v_hbm, kbuf, vbuf, sem, bb, par):
    out = []
    for h in range(N_ATTN_HEADS):
        for kt in range(MOBA_TOPK):
            blk = sel_ref[bb, h * MOBA_TOPK + kt]
            for pp in range(_PAGES_PER_BLOCK):
                page = pt_ref[bb, blk * _PAGES_PER_BLOCK + pp]
                slot = h * _SEL_PAGES + kt * _PAGES_PER_BLOCK + pp
                out.append(pltpu.make_async_copy(k_hbm.at[page, h], kbuf.at[par, slot], sem.at[0, par]))
                out.append(pltpu.make_async_copy(v_hbm.at[page, h], vbuf.at[par, slot], sem.at[1, par]))
    return out


def _col_from_row(row):
    n = row.shape[1]
    r = lax.broadcasted_iota(jnp.int32, (n, n), 0)
    c = lax.broadcasted_iota(jnp.int32, (n, n), 1)
    return jnp.sum(jnp.where(r == c, jnp.broadcast_to(row, (n, n)), 0.0), axis=-1, keepdims=True)


def _row_from_col(col):
    n = col.shape[0]
    r = lax.broadcasted_iota(jnp.int32, (n, n), 0)
    c = lax.broadcasted_iota(jnp.int32, (n, n), 1)
    return jnp.sum(jnp.where(r == c, jnp.broadcast_to(col, (n, n)), 0.0), axis=0, keepdims=True)


def _moba_sample_kernel(pt_ref, sel_ref, tbl_ref, q_ref, kn_ref, vn_ref, k_hbm, v_hbm, o_ref,
                        kbuf, vbuf, bias_s, sem, *, past_len):
    b = pl.program_id(0)
    nb = pl.num_programs(0)
    par = b % 2
    scale = ATTN_HEAD_DIM ** -0.5
    copies = functools.partial(_sample_page_copies, pt_ref, sel_ref, k_hbm, v_hbm, kbuf, vbuf, sem)

    @pl.when(b == 0)
    def _():
        for c in copies(0, 0):
            c.start()

    @pl.when(b + 1 < nb)
    def _():
        for c in copies(b + 1, 1 - par):
            c.start()

    for c in copies(b, par):
        c.wait()

    sub = lax.broadcasted_iota(jnp.int32, (_PAGES_PER_BLOCK, PAGE_SIZE), 0)
    lane = lax.broadcasted_iota(jnp.int32, (_PAGES_PER_BLOCK, PAGE_SIZE), 1)
    for h in range(N_ATTN_HEADS):
        q = q_ref[0, h:h + 1, :]
        q_col = _col_from_row(q)
        for kt in range(MOBA_TOPK):
            pos0 = sel_ref[b, h * MOBA_TOPK + kt] * MOBA_BLOCK
            near = past_len - pos0 - (MOBA_BLOCK - 1) < MAX_DISTANCE
            rows = slice(kt * _PAGES_PER_BLOCK, (kt + 1) * _PAGES_PER_BLOCK)

            @pl.when(near)
            def _():
                dist = jnp.maximum(past_len - (pos0 + sub * PAGE_SIZE + lane), 0)
                bias_s[rows, :] = _bias_from_bucket(_t5_bucket(dist), tbl_ref, h)

            @pl.when(jnp.logical_not(near))
            def _():
                bias_s[rows, :] = jnp.full((_PAGES_PER_BLOCK, PAGE_SIZE), tbl_ref[N_BUCKETS - 1, h], F32)

        s = jnp.concatenate([jnp.sum(kbuf[par, h * _SEL_PAGES + j] * q_col, axis=0, keepdims=True)
                             for j in range(_SEL_PAGES)], axis=0) * scale + bias_s[0:_SEL_PAGES, :]
        s_new = jnp.sum(kn_ref[0, h:h + 1, :] * q, axis=-1, keepdims=True) * scale + tbl_ref[0, h]
        m = jnp.maximum(jnp.max(jnp.max(s, axis=-1, keepdims=True), axis=0, keepdims=True), s_new)
        p = jnp.exp(s - m)
        p_new = jnp.exp(s_new - m)
        den = jnp.sum(jnp.sum(p, axis=-1, keepdims=True), axis=0, keepdims=True) + p_new
        pv = vbuf[par, h * _SEL_PAGES] * p[0:1, :]
        for j in range(1, _SEL_PAGES):
            pv = pv + vbuf[par, h * _SEL_PAGES + j] * p[j:j + 1, :]
        num = _row_from_col(jnp.sum(pv, axis=-1, keepdims=True)) + p_new * vn_ref[0, h:h + 1, :]
        o_ref[0, h:h + 1, :] = num / den


def _moba_sample(page_table, sel, table, q3, k3, v3, cache_kt, cache_vt, past_len):
    db, n_h, dh = q3.shape
    assert cache_kt.shape[1:] == (n_h, dh, PAGE_SIZE) and PAGE_SIZE == LANES and _SEL_PAGES <= SUBLANES
    vec = pl.BlockSpec((1, n_h, dh), lambda i, pt, sl: (i, 0, 0))
    grid_spec = pltpu.PrefetchScalarGridSpec(
        num_scalar_prefetch=2,
        grid=(db,),
        in_specs=[pl.BlockSpec(memory_space=pltpu.SMEM), vec, vec, vec,
                  pl.BlockSpec(memory_space=pl.ANY), pl.BlockSpec(memory_space=pl.ANY)],
        out_specs=vec,
        scratch_shapes=[pltpu.VMEM((2, n_h * _SEL_PAGES, dh, PAGE_SIZE), F32),
                        pltpu.VMEM((2, n_h * _SEL_PAGES, dh, PAGE_SIZE), F32),
                        pltpu.VMEM((SUBLANES, PAGE_SIZE), F32),
                        pltpu.SemaphoreType.DMA((2, 2))],
    )
    return pl.pallas_call(
        functools.partial(_moba_sample_kernel, past_len=past_len),
        grid_spec=grid_spec,
        out_shape=jax.ShapeDtypeStruct((db, n_h, dh), F32),
        compiler_params=_cparams(("arbitrary",)),
        name="moba_sample",
    )(page_table, sel, table, q3, k3, v3, cache_kt, cache_vt)


def _mlstm_step_kernel(qk_ref, cs_ref, v_ref, o_ref, g_ref, cw_ref, cb_ref, bg_ref,
                       c_ref, n_ref, m_ref, mem_ref, c_out, n_out, m_out):
    dh = M_HEAD_DIM
    hist = CONV_WIDTH - 1
    y = cb_ref[...] + qk_ref[0] * cw_ref[hist:hist + 1, :]
    for j in range(hist):
        y = y + cs_ref[0, j:j + 1, :] * cw_ref[j:j + 1, :]
    y = y * _sigmoid(y)
    pre = g_ref[0] + bg_ref[...]
    row = lax.broadcasted_iota(jnp.int32, (dh, dh), 0)
    col = lax.broadcasted_iota(jnp.int32, (dh, dh), 1)
    lane = lax.broadcasted_iota(jnp.int32, (1, LANES), 1)
    m_all = jnp.zeros((1, LANES), F32)
    for h in range(N_M_HEADS):
        sl = slice(h * dh, (h + 1) * dh)
        q = y[:, sl]
        k = y[:, M_WIDTH + h * dh:M_WIDTH + (h + 1) * dh] * (dh ** -0.5)
        v = v_ref[0, :, sl]
        i_t = pre[:, h:h + 1]
        logf = _log_sigmoid(pre[:, N_M_HEADS + h:N_M_HEADS + h + 1])
        c_prev = c_ref[0, h]
        n_prev = n_ref[0, h:h + 1, :]
        m_prev = m_ref[0, :, h:h + 1]
        inter = logf + m_prev
        m_t = jnp.maximum(inter, i_t)
        w_inter = jnp.exp(inter - m_t)
        s = jnp.sum(q * k, axis=-1, keepdims=True) * jnp.exp(i_t - m_t)
        cq = _dot_nt(jnp.broadcast_to(q, (SUBLANES, dh)), c_prev, precision=HIGHEST)[0:1, :]
        num = w_inter * cq + s * v
        den = w_inter * jnp.sum(n_prev * q, axis=-1, keepdims=True) + s
        hh = num / jnp.maximum(jnp.abs(den), jnp.exp(-m_t))
        mem_ref[0, :, sl] = _sigmoid(o_ref[0, :, sl]) * hh
        wc = jnp.exp(inter - m_t)
        ws = jnp.exp(i_t - m_t)
        v_col = jnp.sum(jnp.where(row == col, jnp.broadcast_to(v, (dh, dh)), 0.0), axis=-1, keepdims=True)
        c_out[0, h] = wc * c_prev + (ws * v_col) * k
        n_out[0, h:h + 1, :] = wc * n_prev + ws * k
        m_all = jnp.where(lane == h, m_t, m_all)
    m_out[0] = m_all


def _mlstm_step(mqk, cstate, mv, mo, gates, conv_w, conv_b, bg_row, c0, n0, m0):
    db = mqk.shape[0]
    r3 = lambda w: pl.BlockSpec((1, 1, w), lambda i: (i, 0, 0))
    return pl.pallas_call(
        _mlstm_step_kernel,
        grid=(db,),
        in_specs=[r3(2 * M_WIDTH),
                  pl.BlockSpec((1, CONV_WIDTH - 1, 2 * M_WIDTH), lambda i: (i, 0, 0)),
                  r3(M_WIDTH), r3(M_WIDTH), r3(LANES),
                  pl.BlockSpec((CONV_WIDTH, 2 * M_WIDTH), lambda i: (0, 0)),
                  pl.BlockSpec((1, 2 * M_WIDTH), lambda i: (0, 0)),
                  pl.BlockSpec((1, LANES), lambda i: (0, 0)),
                  pl.BlockSpec((1, N_M_HEADS, M_HEAD_DIM, M_HEAD_DIM), lambda i: (i, 0, 0, 0)),
                  pl.BlockSpec((1, N_M_HEADS, M_HEAD_DIM), lambda i: (i, 0, 0)),
                  pl.BlockSpec((1, 1, N_M_HEADS), lambda i: (i, 0, 0))],
        out_specs=[r3(M_WIDTH),
                   pl.BlockSpec((1, N_M_HEADS, M_HEAD_DIM, M_HEAD_DIM), lambda i: (i, 0, 0, 0)),
                   pl.BlockSpec((1, N_M_HEADS, M_HEAD_DIM), lambda i: (i, 0, 0)),
                   r3(LANES)],
        out_shape=[jax.ShapeDtypeStruct((db, 1, M_WIDTH), F32),
                   jax.ShapeDtypeStruct((db, N_M_HEADS, M_HEAD_DIM, M_HEAD_DIM), F32),
                   jax.ShapeDtypeStruct((db, N_M_HEADS, M_HEAD_DIM), F32),
                   jax.ShapeDtypeStruct((db, 1, LANES), F32)],
        compiler_params=_cparams(("parallel",)),
        name="mlstm_step",
    )(mqk, cstate, mv, mo, gates, conv_w, conv_b, bg_row, c0, n0, m0)


def _pad_rows(x, mult):
    r = x.shape[0]
    rp = -(-r // mult) * mult
    return x if rp == r else jnp.pad(x, ((0, rp - r), (0, 0)))


def kernel(x_prompt, x_sample, cache_k, cache_v, page_table, state_C, state_n, state_m, state_conv,
           c_prompt, c_sample, rel_bias_table, w_ada, b_ada, w_in, b_gate, conv_w, conv_b,
           beta_attn, beta_mlstm, w_out, ln1_g, ln1_b, w_query, sub_keys, expert_u, expert_v,
           ln2_g, ln2_b):
    assert w_ada.shape[0] == DEPTH == 1
    B, S, D = x_prompt.shape
    DB, T, _ = x_sample.shape
    assert T == 1
    H, dh = N_ATTN_HEADS, ATTN_HEAD_DIM
    past_len = page_table.shape[1] * PAGE_SIZE
    assert past_len % MOBA_BLOCK == 0
    l = 0

    gate_cols = 2 * N_M_HEADS
    w_in_p = jnp.pad(w_in[l], ((0, 0), (0, LANES - gate_cols))).astype(BF16)
    w_out_b = w_out[l].astype(BF16)
    wq_t = w_query[l].T.astype(BF16)
    u_b = expert_u[l].astype(BF16)
    n_exp = expert_v.shape[1]
    vt_b = jnp.transpose(expert_v[l].reshape(n_exp // _EXPERT_CHUNK, _EXPERT_CHUNK, D), (0, 2, 1)).astype(BF16)
    beta = jnp.concatenate([beta_attn[l], beta_mlstm[l]])[None, :]
    bg = b_gate[l]
    bg8 = jnp.broadcast_to(bg[:, None], (gate_cols, LANES))
    bg_row = jnp.pad(bg, (0, LANES - gate_cols))[None, :]
    cw, cb = conv_w[l], conv_b[l][None, :]
    table = rel_bias_table
    lg1, lb1, lg2, lb2 = ln1_g[l][None, :], ln1_b[l][None, :], ln2_g[l][None, :], ln2_b[l][None, :]

    mod = _ada(jnp.concatenate([c_prompt, c_sample], axis=0), w_ada[l], b_ada[l])
    sh1, sc1, g1, sh2, sc2, g2 = [mod[:, i * D:(i + 1) * D] for i in range(6)]
    sc1, sc2 = 1.0 + sc1, 1.0 + sc2
    pm = lambda t: t[:B][:, None, :]
    sm = lambda t: t[B:][None, :, :]

    xp2 = x_prompt.reshape(B * S, D)
    w_in_tb = w_in[l].T[:3 * ATTN_WIDTH].astype(BF16)
    aqt, ak, akt, avt, avb, mqk, mv, mo, gates = _inproj_prompt(x_prompt, pm(sc1), pm(sh1), w_in_p, w_in_tb)
    attn = _moba_prompt(aqt, ak.reshape(B, S, ATTN_WIDTH), avb, table, _bias_tiles(table))
    mqk3 = mqk.reshape(B, S, 2 * M_WIDTH)
    mem, c_p, n_p, m_p = _mlstm_prompt(mqk3, mv.reshape(B, S, M_WIDTH), mo.reshape(B, S, M_WIDTH),
                                       gates.reshape(B, S, LANES), cw, cb, bg8)
    x1, h2 = _outproj(attn, mem.reshape(B * S, M_WIDTH), xp2, beta, w_out_b,
                      pm(g1), pm(sc2), pm(sh2), lg1, lb1, 256, S)
    f = _peer(h2, wq_t, sub_keys[l], u_b, vt_b, 256, 512)
    y_prompt = _final(x1, f, pm(g2), lg2, lb2, 512, S).reshape(B, S, D)
    from_t = lambda t: jnp.transpose(t.reshape(B, H, dh, S), (0, 3, 1, 2))[None]
    k_prompt = from_t(akt)
    v_prompt = from_t(avt)
    conv_prompt = mqk3[:, S - (CONV_WIDTH - 1):, :][None]

    xs2 = x_sample.reshape(DB, D)
    saq, sak, sav, smqk, smv, smo, sgates = _inproj(xs2, sm(sc1), sm(sh1), w_in_p, DB, 1)
    cache_kt = jnp.transpose(cache_k[l], (0, 2, 3, 1))
    cache_vt = jnp.transpose(cache_v[l], (0, 2, 3, 1))
    psum = _page_sums(cache_kt)
    n_blocks = past_len // MOBA_BLOCK
    sel = _block_gate(page_table, psum, saq.reshape(DB, 1, H * dh), n_blocks)
    sel = sel[:, :, :MOBA_TOPK].reshape(DB, H * MOBA_TOPK)
    h3 = lambda t: t.reshape(DB, H, dh)
    s_attn = _moba_sample(page_table, sel, table, h3(saq), h3(sak), h3(sav), cache_kt, cache_vt, past_len)
    s_mem, c_s, n_s, m_s = _mlstm_step(
        smqk.reshape(DB, 1, 2 * M_WIDTH), state_conv[l], smv.reshape(DB, 1, M_WIDTH),
        smo.reshape(DB, 1, M_WIDTH), sgates.reshape(DB, 1, LANES), cw, cb, bg_row,
        state_C[l], state_n[l], state_m[l].reshape(DB, 1, N_M_HEADS))
    sx1, sh2_ = _outproj(s_attn.reshape(DB, H * dh), s_mem.reshape(DB, M_WIDTH), xs2, beta, w_out_b,
                         sm(g1), sm(sc2), sm(sh2), lg1, lb1, DB, 1)
    sf = _peer(_pad_rows(sh2_, LANES), wq_t, sub_keys[l], u_b, vt_b, LANES, LANES)[:DB]
    y_sample = _final(sx1, sf, sm(g2), lg2, lb2, DB, 1).reshape(DB, 1, D)
    conv_sample = jnp.concatenate([state_conv[l][:, 1:, :], smqk.reshape(DB, 1, 2 * M_WIDTH)], axis=1)[None]

    return (y_prompt, y_sample,
            k_prompt, v_prompt, c_p[None], n_p[None], m_p[:, :N_M_HEADS, 0][None], conv_prompt,
            sak.reshape(1, DB, 1, H, dh), sav.reshape(1, DB, 1, H, dh),
            c_s[None], n_s[None], m_s[:, 0, :N_M_HEADS][None], conv_sample)
```

```python
import functools
import math

import numpy as np
import jax
import jax.numpy as jnp
from jax import lax
from jax.experimental import pallas as pl
from jax.experimental.pallas import tpu as pltpu

F32 = jnp.float32
BF16 = jnp.bfloat16
NEG_INF = float("-inf")
HIGHEST = lax.Precision.HIGHEST

N_ATTN_HEADS = 8
ATTN_HEAD_DIM = 64
ATTN_WIDTH = N_ATTN_HEADS * ATTN_HEAD_DIM
MOBA_BLOCK = 256
MOBA_TOPK = 3
PAGE_SIZE = 128
N_BUCKETS = 32
MAX_DISTANCE = 128
N_M_HEADS = 4
M_HEAD_DIM = 128
M_WIDTH = N_M_HEADS * M_HEAD_DIM
CONV_WIDTH = 4
M_CHUNK = 128
N_KEYS = 128
PEER_HEADS = 8
PEER_KEY_DIM = 256
PEER_TOPK = 16
LN_EPS = 1e-5
DEPTH = 1
ALPHA = (2.0 * DEPTH) ** 0.25

LANES = 128
SUBLANES = 8
VMEM_LIMIT = 56 * 1024 * 1024


def _cparams(sem, flags=None):
    return pltpu.CompilerParams(dimension_semantics=sem, vmem_limit_bytes=VMEM_LIMIT, flags=flags)


def _bucket_thresholds():
    max_exact = N_BUCKETS // 2
    d = np.arange(0, MAX_DISTANCE + 1)
    far = max_exact + (np.log(np.maximum(d, 1) / max_exact) / math.log(MAX_DISTANCE / max_exact)
                       * (N_BUCKETS - max_exact)).astype(np.int64)
    bucket = np.where(d < max_exact, d, np.minimum(far, N_BUCKETS - 1))
    assert np.all(np.diff(bucket) >= 0) and bucket[-1] == N_BUCKETS - 1
    return tuple(int(np.argmax(bucket >= k)) for k in range(max_exact + 1, N_BUCKETS))


_BUCKET_THRESHOLDS = _bucket_thresholds()


def _t5_bucket(dist):
    max_exact = N_BUCKETS // 2
    far = jnp.full(dist.shape, max_exact, jnp.int32)
    for t in _BUCKET_THRESHOLDS:
        far = far + (dist >= t).astype(jnp.int32)
    return jnp.where(dist < max_exact, dist, far)


def _bias_from_bucket(bucket, tbl_ref, h):
    out = jnp.zeros(bucket.shape, F32)
    for j in range(N_BUCKETS):
        out = jnp.where(bucket == j, tbl_ref[j, h], out)
    return out


def _standardize(x):
    mu = jnp.mean(x, axis=-1, keepdims=True)
    xc = x - mu
    var = jnp.mean(xc * xc, axis=-1, keepdims=True)
    return xc * lax.rsqrt(var + LN_EPS)


def _sigmoid(x):
    return 1.0 / (1.0 + jnp.exp(-x))


def _log_sigmoid(x):
    return jnp.minimum(x, 0.0) - jnp.log1p(jnp.exp(-jnp.abs(x)))


def _gelu_tanh(x):
    c = math.sqrt(2.0 / math.pi)
    hx = 0.5 * x
    return hx + hx * jnp.tanh(x * (c + (c * 0.044715) * (x * x)))


def _dot_nt(a, b, **kw):
    return lax.dot_general(a, b, (((1,), (1,)), ((), ())), preferred_element_type=F32, **kw)


def _dot(a, b, **kw):
    return jnp.dot(a, b, preferred_element_type=F32, **kw)


def _ada_kernel(c_ref, w_ref, b_ref, o_ref):
    c = c_ref[...]
    s = c * _sigmoid(c)
    o_ref[...] = _dot(s, w_ref[...], precision=HIGHEST) + b_ref[...]


def _ada(c_all, w_ada, b_ada):
    n, d = c_all.shape
    n_out = w_ada.shape[1]
    tn = 1024
    return pl.pallas_call(
        _ada_kernel,
        grid=(n_out // tn,),
        in_specs=[pl.BlockSpec((n, d), lambda j: (0, 0)),
                  pl.BlockSpec((d, tn), lambda j: (0, j)),
                  pl.BlockSpec((1, tn), lambda j: (0, j))],
        out_specs=pl.BlockSpec((n, tn), lambda j: (0, j)),
        out_shape=jax.ShapeDtypeStruct((n, n_out), F32),
        compiler_params=_cparams(("parallel",)),
        name="ada_mod",
    )(c_all, w_ada, b_ada.reshape(1, n_out))


_PROJ_GROUPS = (ATTN_WIDTH, ATTN_WIDTH, ATTN_WIDTH, 2 * M_WIDTH, M_WIDTH, M_WIDTH, LANES)
_PROJ_OFFS = tuple(int(v) for v in np.cumsum((0,) + _PROJ_GROUPS))


def _inproj_kernel(x_ref, sc_ref, sh_ref, w_ref, *o_refs):
    h = _standardize(x_ref[...]) * sc_ref[0] + sh_ref[0]
    hb = h.astype(BF16)
    for g, o_ref in enumerate(o_refs):
        o_ref[...] = _dot(hb, w_ref[:, _PROJ_OFFS[g]:_PROJ_OFFS[g + 1]])


def _inproj(x2d, sc3, sh3, w_in_b, tm, rows_per_mod):
    r, d = x2d.shape
    m = sc3.shape[1]
    if m == 1:
        mod_map = lambda i: ((i * tm) // rows_per_mod, 0, 0)
    else:
        mod_map = lambda i: (i, 0, 0)
    return pl.pallas_call(
        _inproj_kernel,
        grid=(r // tm,),
        in_specs=[pl.BlockSpec((tm, d), lambda i: (i, 0)),
                  pl.BlockSpec((1, m, d), mod_map),
                  pl.BlockSpec((1, m, d), mod_map),
                  pl.BlockSpec(w_in_b.shape, lambda i: (0, 0))],
        out_specs=[pl.BlockSpec((tm, g), lambda i: (i, 0)) for g in _PROJ_GROUPS],
        out_shape=[jax.ShapeDtypeStruct((r, g), F32) for g in _PROJ_GROUPS],
        compiler_params=_cparams(("parallel",)),
        name="inproj",
    )(x2d, sc3, sh3, w_in_b)


def _inproj_prompt_kernel(x_ref, sc_ref, sh_ref, w_ref, wt_ref, qt_ref, k_ref, kt_ref, vt_ref, vb_ref,
                          mqk_ref, mv_ref, mo_ref, g_ref):
    h = _standardize(x_ref[...]) * sc_ref[0] + sh_ref[0]
    hb = h.astype(BF16)
    aw = ATTN_WIDTH
    k_ref[...] = _dot(hb, w_ref[:, _PROJ_OFFS[1]:_PROJ_OFFS[2]])
    for g, o_ref in ((3, mqk_ref), (4, mv_ref), (5, mo_ref), (6, g_ref)):
        o_ref[...] = _dot(hb, w_ref[:, _PROJ_OFFS[g]:_PROJ_OFFS[g + 1]])
    qt_ref[0] = _dot_nt(wt_ref[0:aw, :], hb)
    kt_ref[0] = _dot_nt(wt_ref[aw:2 * aw, :], hb)
    vt = _dot_nt(wt_ref[2 * aw:3 * aw, :], hb)
    vt_ref[0] = vt
    vb_ref[0, 0] = vt


def _inproj_prompt(x3, sc3, sh3, w_in_b, w_in_tb):
    b, s, d = x3.shape
    tm = MOBA_BLOCK
    nt = s // tm
    r = b * s
    aw = ATTN_WIDTH
    assert s % tm == 0
    row = lambda w: pl.BlockSpec((tm, w), lambda i: (i, 0))
    tr = pl.BlockSpec((1, aw, tm), lambda i: (i // nt, 0, i % nt))
    mod = pl.BlockSpec((1, 1, d), lambda i: (i // nt, 0, 0))
    return pl.pallas_call(
        _inproj_prompt_kernel,
        grid=(r // tm,),
        in_specs=[row(d), mod, mod,
                  pl.BlockSpec(w_in_b.shape, lambda i: (0, 0)),
                  pl.BlockSpec(w_in_tb.shape, lambda i: (0, 0))],
        out_specs=[tr, row(aw), tr, tr,
                   pl.BlockSpec((1, 1, aw, tm), lambda i: (i // nt, i % nt, 0, 0)),
                   row(2 * M_WIDTH), row(M_WIDTH), row(M_WIDTH), row(LANES)],
        out_shape=[jax.ShapeDtypeStruct((b, aw, s), F32), jax.ShapeDtypeStruct((r, aw), F32),
                   jax.ShapeDtypeStruct((b, aw, s), F32), jax.ShapeDtypeStruct((b, aw, s), F32),
                   jax.ShapeDtypeStruct((b, nt, aw, tm), F32),
                   jax.ShapeDtypeStruct((r, 2 * M_WIDTH), F32), jax.ShapeDtypeStruct((r, M_WIDTH), F32),
                   jax.ShapeDtypeStruct((r, M_WIDTH), F32), jax.ShapeDtypeStruct((r, LANES), F32)],
        compiler_params=_cparams(("parallel",)),
        name="inproj_prompt",
    )(x3.reshape(r, d), sc3, sh3, w_in_b, w_in_tb)


def _bias_tiles_kernel(tbl_ref, o_ref):
    h = pl.program_id(0)
    key = lax.broadcasted_iota(jnp.int32, (MOBA_BLOCK, MOBA_BLOCK), 0)
    qry = lax.broadcasted_iota(jnp.int32, (MOBA_BLOCK, MOBA_BLOCK), 1)
    for t in range(2):
        dist = jnp.maximum(qry - key + t * MOBA_BLOCK, 0)
        o_ref[0, t] = _bias_from_bucket(_t5_bucket(dist), tbl_ref, h)


def _bias_tiles(table):
    n_h = table.shape[1]
    return pl.pallas_call(
        _bias_tiles_kernel,
        grid=(n_h,),
        in_specs=[pl.BlockSpec(memory_space=pltpu.SMEM)],
        out_specs=pl.BlockSpec((1, 2, MOBA_BLOCK, MOBA_BLOCK), lambda h: (h, 0, 0, 0)),
        out_shape=jax.ShapeDtypeStruct((n_h, 2, MOBA_BLOCK, MOBA_BLOCK), F32),
        compiler_params=_cparams(("parallel",)),
        name="moba_bias_tiles",
    )(table)


_HEADS_PER_STEP = 4
_HEAD_GROUP_COLS = _HEADS_PER_STEP * ATTN_HEAD_DIM


def _moba_prompt_kernel(tbl_ref, qt_ref, k_ref, vb_ref, bias_ref, o_ref, selb_s, *, n_blocks):
    hp = pl.program_id(1)
    ob = pl.program_id(2)
    blk = MOBA_BLOCK
    dh = ATTN_HEAD_DIM
    gc = _HEAD_GROUP_COLS
    scale = dh ** -0.5
    heads = range(_HEADS_PER_STEP)

    kmean = jnp.concatenate(
        [jnp.sum(k_ref[0, n * blk:(n + 1) * blk, :], axis=0, keepdims=True) * (1.0 / blk)
         for n in range(n_blocks)], axis=0)
    sub = lax.broadcasted_iota(jnp.int32, (n_blocks, blk), 0)
    key = lax.broadcasted_iota(jnp.int32, (blk, blk), 0)
    qry = lax.broadcasted_iota(jnp.int32, (blk, blk), 1)
    start = pl.multiple_of(ob * blk, blk)
    fsub = lax.broadcasted_iota(jnp.int32, (gc, blk), 0)
    k_own = k_ref[0, pl.ds(start, blk), :].astype(BF16)
    c_fars = [tbl_ref[N_BUCKETS - 1, hp * _HEADS_PER_STEP + j] for j in heads]

    qzs = [jnp.where((fsub >= j * dh) & (fsub < (j + 1) * dh), qt_ref[0], 0.0) for j in heads]
    qzbs = [qz.astype(BF16) for qz in qzs]
    gates = [_dot(kmean, qz, precision=HIGHEST) for qz in qzs]
    qk_own = [_dot(k_own, qzb) for qzb in qzbs]
    for j in heads:
        g = jnp.where(sub < ob, gates[j], NEG_INF)
        sel = jnp.zeros((n_blocks, blk), F32)
        for _ in range(MOBA_TOPK):
            mx = jnp.max(g, axis=0, keepdims=True)
            idx = jnp.min(jnp.where(g == mx, sub, n_blocks), axis=0, keepdims=True)
            hit = sub == idx
            sel = jnp.where(hit, 1.0, sel)
            g = jnp.where(hit, NEG_INF, g)
        sel = jnp.where(sub < ob, sel, 0.0)
        for n in range(n_blocks):
            selb_s[j, n] = jnp.broadcast_to(sel[n:n + 1, :], (SUBLANES, blk))
    own = []
    for j in heads:
        s = jnp.where(key <= qry, qk_own[j] * scale + bias_ref[j, 0], NEG_INF)
        m0 = jnp.max(s, axis=0, keepdims=True)
        p = jnp.exp(s - m0)
        own.append((m0, jnp.sum(p, axis=0, keepdims=True), p.astype(BF16)))
    init = tuple((own[j][0], own[j][1], _dot(vb_ref[0, ob, j * dh:(j + 1) * dh, :].astype(BF16), own[j][2]))
                 for j in heads)

    def body(n, carry):
        st = pl.multiple_of(n * blk, blk)
        k_n = k_ref[0, pl.ds(st, blk), :].astype(BF16)
        qk = [_dot(k_n, qzbs[j]) for j in heads]
        stats = []
        for j in heads:
            m, l, _ = carry[j]
            bias = jnp.where(n == ob - 1, bias_ref[j, 1], c_fars[j])
            s = jnp.where(selb_s[j, n][0:1, :] > 0.5, qk[j] * scale + bias, NEG_INF)
            m_new = jnp.maximum(m, jnp.max(s, axis=0, keepdims=True))
            a = jnp.exp(m - m_new)
            p = jnp.exp(s - m_new)
            stats.append((m_new, a * l + jnp.sum(p, axis=0, keepdims=True), a, p.astype(BF16)))
        pv = [_dot(vb_ref[0, n, j * dh:(j + 1) * dh, :].astype(BF16), stats[j][3]) for j in heads]
        return tuple((stats[j][0], stats[j][1], stats[j][2] * carry[j][2] + pv[j]) for j in heads)

    final = lax.fori_loop(0, ob, body, init)
    for j in heads:
        _, l, acc = final[j]
        o_ref[0, j * dh:(j + 1) * dh, :] = acc / l


def _moba_prompt(qt, k, vb, table, bias_tiles):
    b, aw, s = qt.shape
    blk = MOBA_BLOCK
    n_blocks = s // blk
    hps = _HEADS_PER_STEP
    gc = _HEAD_GROUP_COLS
    assert s % blk == 0 and MOBA_TOPK <= n_blocks <= SUBLANES and aw % gc == 0 and gc % LANES == 0
    return pl.pallas_call(
        functools.partial(_moba_prompt_kernel, n_blocks=n_blocks),
        grid=(b, aw // gc, n_blocks),
        in_specs=[pl.BlockSpec(memory_space=pltpu.SMEM),
                  pl.BlockSpec((1, gc, blk), lambda i, g, j: (i, g, j)),
                  pl.BlockSpec((1, s, gc), lambda i, g, j: (i, 0, g)),
                  pl.BlockSpec((1, n_blocks, gc, blk), lambda i, g, j: (i, 0, g, 0)),
                  pl.BlockSpec((hps, 2, blk, blk), lambda i, g, j: (g, 0, 0, 0))],
        out_specs=pl.BlockSpec((1, gc, blk), lambda i, g, j: (i, g, j)),
        out_shape=jax.ShapeDtypeStruct((b, aw, s), F32),
        scratch_shapes=[pltpu.VMEM((hps, n_blocks, SUBLANES, blk), F32)],
        compiler_params=_cparams(("parallel", "parallel", "parallel")),
        name="moba_prompt",
    )(table, qt, k, vb, bias_tiles)


_CONV_PAD = SUBLANES


def _mlstm_prompt_kernel(q_ref, k_ref, v_ref, o_ref, g_ref, cw_ref, cb_ref, bg_ref,
                         mem_ref, c_out, n_out, m_out, xq_s, xk_s, c_s, n_s, m_s):
    ci = pl.program_id(1)
    L = M_CHUNK
    dh = M_HEAD_DIM
    pad = _CONV_PAD
    hist = CONV_WIDTH - 1

    @pl.when(ci == 0)
    def _():
        xq_s[0:pad, :] = jnp.zeros((pad, M_WIDTH), F32)
        xk_s[0:pad, :] = jnp.zeros((pad, M_WIDTH), F32)
        c_s[...] = jnp.zeros(c_s.shape, F32)
        n_s[...] = jnp.zeros(n_s.shape, F32)
        m_s[...] = jnp.zeros(m_s.shape, F32)

    xq_s[pad:pad + L, :] = q_ref[0]
    xk_s[pad:pad + L, :] = k_ref[0]

    def conv(x_s, col0):
        y = cb_ref[:, col0:col0 + M_WIDTH]
        for j in range(CONV_WIDTH):
            y = y + x_s[pad - hist + j:pad - hist + j + L, :] * cw_ref[j:j + 1, col0:col0 + M_WIDTH]
        return y * _sigmoid(y)

    qc = conv(xq_s, 0)
    kc = conv(xk_s, M_WIDTH) * (dh ** -0.5)
    tq = xq_s[pad + L - hist:pad + L, :]
    tk = xk_s[pad + L - hist:pad + L, :]
    xq_s[pad - hist:pad, :] = tq
    xk_s[pad - hist:pad, :] = tk

    heads = range(N_M_HEADS)
    hs = [slice(h * dh, (h + 1) * dh) for h in heads]
    qs = [qc[:, hs[h]] for h in heads]
    ks = [kc[:, hs[h]] for h in heads]
    vs = [v_ref[0, :, hs[h]] for h in heads]
    qbs = [q.astype(BF16) for q in qs]
    kbs = [k.astype(BF16) for k in ks]
    c_prevs = [c_s[h] for h in heads]
    n_prevs = [n_s[h:h + 1, :] for h in heads]
    m_prevs = [m_s[h:h + 1, 0:1] for h in heads]
    qk = [_dot_nt(qbs[h], kbs[h]) for h in heads]
    qc_prev = [_dot_nt(qbs[h], c_prevs[h].astype(BF16)) for h in heads]
    qn = [jnp.sum(qs[h] * n_prevs[h], axis=-1, keepdims=True) for h in heads]

    row = lax.broadcasted_iota(jnp.int32, (L, L), 0)
    col = lax.broadcasted_iota(jnp.int32, (L, L), 1)
    causal = col <= row
    pre = g_ref[0] + bg_ref[...]
    lane = lax.broadcasted_iota(jnp.int32, pre.shape, 1)
    is_f = (lane >= N_M_HEADS) & (lane < 2 * N_M_HEADS)
    cum = _dot(jnp.where(causal, 1.0, 0.0), jnp.where(is_f, _log_sigmoid(pre), 0.0), precision=HIGHEST)
    t8 = jnp.where(lane < N_M_HEADS, pre, cum)
    r8 = t8.T[0:2 * N_M_HEADS, :]
    i_rows = [r8[h:h + 1, :] for h in heads]
    b_rows = [r8[N_M_HEADS + h:N_M_HEADS + h + 1, :] for h in heads]
    i_cols = [t8[:, h:h + 1] for h in heads]
    b_cols = [t8[:, N_M_HEADS + h:N_M_HEADS + h + 1] for h in heads]

    intra = []
    for h in heads:
        d = jnp.where(causal, b_cols[h] - b_rows[h] + i_rows[h], NEG_INF)
        inter = b_cols[h] + m_prevs[h]
        m_t = jnp.maximum(inter, jnp.max(d, axis=-1, keepdims=True))
        w_inter = jnp.exp(inter - m_t)
        s = qk[h] * jnp.exp(d - m_t)
        intra.append((m_t, w_inter, s, jnp.sum(s, axis=-1, keepdims=True)))
    sv = [_dot(intra[h][2].astype(BF16), vs[h].astype(BF16)) for h in heads]

    carry = []
    for h in heads:
        b_last = b_rows[h][:, L - 1:L]
        g_row = b_last - b_rows[h] + i_rows[h]
        g_col = b_last - b_cols[h] + i_cols[h]
        m_new = jnp.maximum(b_last + m_prevs[h], jnp.max(g_row, axis=-1, keepdims=True))
        wc = jnp.exp(b_last + m_prevs[h] - m_new)
        ws = jnp.exp(g_col - m_new)
        carry.append((m_new, wc, ws, (ws * vs[h]).T.astype(BF16)))
    vk = [_dot(carry[h][3], kbs[h]) for h in heads]

    for h in heads:
        m_t, w_inter, _, s_sum = intra[h]
        num = w_inter * qc_prev[h] + sv[h]
        den = w_inter * qn[h] + s_sum
        hh = num / jnp.maximum(jnp.abs(den), jnp.exp(-m_t))
        mem_ref[0, :, hs[h]] = _sigmoid(o_ref[0, :, hs[h]]) * hh
    for h in heads:
        m_new, wc, ws, _ = carry[h]
        c_s[h] = wc * c_prevs[h] + vk[h]
        n_s[h:h + 1, :] = wc * n_prevs[h] + jnp.sum(ws * ks[h], axis=0, keepdims=True)
        m_s[h:h + 1, :] = jnp.broadcast_to(m_new, (1, LANES))

    @pl.when(ci == pl.num_programs(1) - 1)
    def _():
        c_out[0] = c_s[...]
        n_out[0] = n_s[0:N_M_HEADS, :]
        m_out[0] = m_s[...]


def _mlstm_prompt(mqk, mv, mo, gates, conv_w, conv_b, bg_row):
    b, s, _ = mv.shape
    L = M_CHUNK
    nc = s // L
    assert s % L == 0
    return pl.pallas_call(
        _mlstm_prompt_kernel,
        grid=(b, nc),
        in_specs=[pl.BlockSpec((1, L, M_WIDTH), lambda i, c: (i, c, 0)),
                  pl.BlockSpec((1, L, M_WIDTH), lambda i, c: (i, c, 1)),
                  pl.BlockSpec((1, L, M_WIDTH), lambda i, c: (i, c, 0)),
                  pl.BlockSpec((1, L, M_WIDTH), lambda i, c: (i, c, 0)),
                  pl.BlockSpec((1, L, LANES), lambda i, c: (i, c, 0)),
                  pl.BlockSpec((CONV_WIDTH, 2 * M_WIDTH), lambda i, c: (0, 0)),
                  pl.BlockSpec((1, 2 * M_WIDTH), lambda i, c: (0, 0)),
                  pl.BlockSpec((1, LANES), lambda i, c: (0, 0))],
        out_specs=[pl.BlockSpec((1, L, M_WIDTH), lambda i, c: (i, c, 0)),
                   pl.BlockSpec((1, N_M_HEADS, M_HEAD_DIM, M_HEAD_DIM), lambda i, c: (i, 0, 0, 0)),
                   pl.BlockSpec((1, N_M_HEADS, M_HEAD_DIM), lambda i, c: (i, 0, 0)),
                   pl.BlockSpec((1, SUBLANES, LANES), lambda i, c: (i, 0, 0))],
        out_shape=[jax.ShapeDtypeStruct((b, s, M_WIDTH), F32),
                   jax.ShapeDtypeStruct((b, N_M_HEADS, M_HEAD_DIM, M_HEAD_DIM), F32),
                   jax.ShapeDtypeStruct((b, N_M_HEADS, M_HEAD_DIM), F32),
                   jax.ShapeDtypeStruct((b, SUBLANES, LANES), F32)],
        scratch_shapes=[pltpu.VMEM((_CONV_PAD + L, M_WIDTH), F32),
                        pltpu.VMEM((_CONV_PAD + L, M_WIDTH), F32),
                        pltpu.VMEM((N_M_HEADS, M_HEAD_DIM, M_HEAD_DIM), F32),
                        pltpu.VMEM((SUBLANES, M_HEAD_DIM), F32),
                        pltpu.VMEM((SUBLANES, LANES), F32)],
        compiler_params=_cparams(("parallel", "arbitrary")),
        name="mlstm_prompt",
    )(mqk, mqk, mv, mo, gates, conv_w, conv_b, bg_row)


def _outproj_kernel(a_ref, m_ref, x_ref, beta_ref, w_ref, g1_ref, sc_ref, sh_ref, lg_ref, lb_ref,
                    x1_ref, h2_ref, *, attn_feature_major):
    attn = a_ref[0].T if attn_feature_major else a_ref[...]
    mixed = jnp.concatenate([attn, m_ref[...]], axis=-1) * beta_ref[...]
    y = _dot(mixed.astype(BF16), w_ref[...])
    z = ALPHA * x_ref[...] + g1_ref[0] * y
    x1 = _standardize(z) * lg_ref[...] + lb_ref[...]
    x1_ref[...] = x1
    h2_ref[...] = (_standardize(x1) * sc_ref[0] + sh_ref[0]).astype(h2_ref.dtype)


def _outproj(attn, mem, x2d, beta, w_out_b, g1, sc2, sh2, ln_g, ln_b, tm, rows_per_mod):
    r, d = x2d.shape
    m = g1.shape[1]
    if m == 1:
        mod_map = lambda i: ((i * tm) // rows_per_mod, 0, 0)
    else:
        mod_map = lambda i: (i, 0, 0)
    vec = pl.BlockSpec((1, d), lambda i: (0, 0))
    mod = pl.BlockSpec((1, m, d), mod_map)
    feature_major = attn.ndim == 3
    if feature_major:
        nt = rows_per_mod // tm
        assert rows_per_mod % tm == 0 and attn.shape[2] == rows_per_mod
        attn_spec = pl.BlockSpec((1, attn.shape[1], tm), lambda i: (i // nt, 0, i % nt))
    else:
        attn_spec = pl.BlockSpec((tm, attn.shape[1]), lambda i: (i, 0))
    return pl.pallas_call(
        functools.partial(_outproj_kernel, attn_feature_major=feature_major),
        grid=(r // tm,),
        in_specs=[attn_spec,
                  pl.BlockSpec((tm, mem.shape[1]), lambda i: (i, 0)),
                  pl.BlockSpec((tm, d), lambda i: (i, 0)),
                  vec,
                  pl.BlockSpec(w_out_b.shape, lambda i: (0, 0)),
                  mod, mod, mod, vec, vec],
        out_specs=[pl.BlockSpec((tm, d), lambda i: (i, 0)),
                   pl.BlockSpec((tm, d), lambda i: (i, 0))],
        out_shape=[jax.ShapeDtypeStruct((r, d), F32), jax.ShapeDtypeStruct((r, d), BF16)],
        compiler_params=_cparams(("parallel",)),
        name="outproj",
    )(attn, mem, x2d, beta, w_out_b, g1, sc2, sh2, ln_g, ln_b)


def _oddeven_merge_sort_pairs(n):
    pairs = []

    def merge(lo, m, r):
        step = r * 2
        if step < m:
            merge(lo, m, step)
            merge(lo + r, m, step)
            for i in range(lo + r, lo + m - r, step):
                pairs.append((i, i + r))
        else:
            pairs.append((lo, lo + r))

    def sort(lo, m):
        if m > 1:
            h = m // 2
            sort(lo, h)
            sort(lo + h, h)
            merge(lo, m, 1)

    sort(0, n)
    return tuple(pairs)


_SORT16 = _oddeven_merge_sort_pairs(PEER_TOPK)


def _vmax(a, b):
    if a is None:
        return b
    if b is None:
        return a
    return jnp.maximum(a, b)


def _cmpx(v, i, j):
    a, b = v[i], v[j]
    if b is None:
        return
    if a is None:
        v[i], v[j] = b, None
        return
    v[i], v[j] = jnp.maximum(a, b), jnp.minimum(a, b)


def _bitonic_to_desc(v):
    n = len(v)
    d = n // 2
    while d >= 1:
        for i in range(n):
            if (i & d) == 0:
                _cmpx(v, i, i + d)
        d //= 2
    return v


def _merge_top(x, y):
    n = len(x)
    return _bitonic_to_desc([_vmax(x[i], y[n - 1 - i]) for i in range(n)])


def _top16_desc(sc):
    groups = sc.shape[0] // SUBLANES
    assert groups == PEER_TOPK
    v = [sc[g * SUBLANES:(g + 1) * SUBLANES, :] for g in range(groups)]
    for i, j in _SORT16:
        _cmpx(v, i, j)
    shift = SUBLANES // 2
    while shift >= 1:
        partner = [pltpu.roll(a, shift, 0) for a in v]
        v = _merge_top(v, partner)
        shift //= 2
    return v


def _candidate_lists(a, b):
    k = PEER_TOPK
    lists = []
    for i in range(4):
        n = k // (i + 1)
        lists.append([a[i] + b[j] for j in range(n)])
    for j in range(3):
        n = k // (j + 1)
        col = [a[i] + b[j] for i in range(4, n)]
        if col:
            lists.append(col)
    return [l + [None] * (k - len(l)) for l in lists]


_RANK_STEP = 2.0


def _prefix_count(pred, vals):
    assert len(vals) == PEER_TOPK == 16
    sel = jnp.where
    c8 = pred(vals[7])
    c4 = pred(sel(c8, vals[11], vals[3]))
    c2 = pred(sel(c8, sel(c4, vals[13], vals[9]), sel(c4, vals[5], vals[1])))
    c1 = pred(sel(c8, sel(c4, sel(c2, vals[14], vals[12]), sel(c2, vals[10], vals[8])),
                  sel(c4, sel(c2, vals[6], vals[4]), sel(c2, vals[2], vals[0]))))
    n = (sel(c8, 8 * _RANK_STEP, 0.0) + sel(c4, 4 * _RANK_STEP, 0.0)
         + sel(c2, 2 * _RANK_STEP, 0.0) + sel(c1, _RANK_STEP, 0.0))
    return sel(pred(vals[15]), 16 * _RANK_STEP, n)


def _route_weights_kernel(sk_ref, wq_ref, o_ref):
    s = pl.program_id(0) % 2
    o_ref[...] = _dot_nt(sk_ref[s], wq_ref[...], precision=HIGHEST).astype(o_ref.dtype)


def _route_weights(sub_keys, w_query):
    d, width = w_query.shape
    kd = PEER_KEY_DIM // 2
    assert width == PEER_HEADS * 2 * kd and sub_keys.shape == (2, N_KEYS, kd)
    return pl.pallas_call(
        _route_weights_kernel,
        grid=(width // kd,),
        in_specs=[pl.BlockSpec(sub_keys.shape, lambda j: (0, 0, 0)),
                  pl.BlockSpec((d, kd), lambda j: (0, j))],
        out_specs=pl.BlockSpec((N_KEYS, d), lambda j: (j, 0)),
        out_shape=jax.ShapeDtypeStruct((PEER_HEADS * 2 * N_KEYS, d), BF16),
        compiler_params=_cparams(("parallel",)),
        name="peer_route_weights",
    )(sub_keys, w_query)


def _peer_route_kernel(h_ref, wr_ref, cnt_ref, rk_ref, a_ref, b_ref, sc_s, top_s, tz_s):
    tm = h_ref.shape[0]
    sc_s[...] = _dot_nt(wr_ref[...], h_ref[...])

    def scores(p, s):
        return sc_s[pl.ds(pl.multiple_of((2 * p + s) * N_KEYS, N_KEYS), N_KEYS), :]

    def head(p, carry):
        for s in range(2):
            srt = _top16_desc(scores(p, s))
            for r in range(PEER_TOPK):
                top_s[p, s, r] = srt[r]
        return carry

    lax.fori_loop(0, PEER_HEADS, head, 0)
    sub = lax.broadcasted_iota(jnp.int32, (SUBLANES, tm), 0)

    def on_sublanes(s, r):
        out = top_s[0, s, r]
        for p in range(1, PEER_HEADS):
            out = jnp.where(sub == p, top_s[p, s, r], out)
        return out

    top = [[on_sublanes(s, r) for r in range(PEER_TOPK)] for s in range(2)]
    lists = _candidate_lists(top[0], top[1])
    best = lists[0]
    for other in lists[1:]:
        best = _merge_top(best, other)
    z = jnp.ones_like(best[0])
    for r in range(1, PEER_TOPK):
        z = z + jnp.exp(best[r] - best[0])
    thr = best[PEER_TOPK - 1]
    for p in range(PEER_HEADS):
        tz_s[p, 0] = jnp.broadcast_to(thr[p:p + 1, :], (SUBLANES, tm))
        tz_s[p, 1] = jnp.broadcast_to(z[p:p + 1, :], (SUBLANES, tm))

    def emit(p, carry):
        s0 = scores(p, 0)
        s1 = scores(p, 1)
        t_row = tz_s[p, 0][0:1, :]
        z_row = tz_s[p, 1][0:1, :]
        b_top = [top_s[p, 1, r][0:1, :] for r in range(PEER_TOPK)]
        cnt_ref[p] = _prefix_count(lambda v: s0 + v >= t_row, b_top)
        rk_ref[p] = _prefix_count(lambda v: v > s1, b_top).astype(rk_ref.dtype)
        a_ref[p] = jnp.exp(s0 - top_s[p, 0, 0][0:1, :]) / z_row
        b_ref[p] = jnp.exp(s1 - top_s[p, 1, 0][0:1, :]).astype(b_ref.dtype)
        return carry

    lax.fori_loop(0, PEER_HEADS, emit, 0)


def _peer_route(h2, w_route, tm):
    r, d = h2.shape
    assert PEER_HEADS == SUBLANES and w_route.shape == (PEER_HEADS * 2 * N_KEYS, d)
    shp = (PEER_HEADS, N_KEYS, r)
    bspec = pl.BlockSpec((PEER_HEADS, N_KEYS, tm), lambda i: (0, 0, i))
    return pl.pallas_call(
        _peer_route_kernel,
        grid=(r // tm,),
        in_specs=[pl.BlockSpec((tm, d), lambda i: (i, 0)),
                  pl.BlockSpec(w_route.shape, lambda i: (0, 0))],
        out_specs=[bspec, bspec, bspec, bspec],
        out_shape=[jax.ShapeDtypeStruct(shp, F32), jax.ShapeDtypeStruct(shp, BF16),
                   jax.ShapeDtypeStruct(shp, F32), jax.ShapeDtypeStruct(shp, BF16)],
        scratch_shapes=[pltpu.VMEM((w_route.shape[0], tm), F32),
                        pltpu.VMEM((PEER_HEADS, 2, PEER_TOPK, SUBLANES, tm), F32),
                        pltpu.VMEM((PEER_HEADS, 2, SUBLANES, tm), F32)],
        compiler_params=_cparams(("parallel",)),
        name="peer_route",
    )(h2, w_route)


_EXPERT_CHUNK = SUBLANES * N_KEYS
_MIX_SUBTILE = 2 * SUBLANES


def _peer_mix_kernel(h_ref, u_ref, vt_ref, cnt_ref, rk_ref, a_ref, b_ref, o_ref, act_s, y_s, acc_s):
    c = pl.program_id(1)
    tm = h_ref.shape[0]

    @pl.when(c == 0)
    def _():
        acc_s[...] = jnp.zeros(acc_s.shape, F32)

    act_s[...] = _dot_nt(u_ref[...], h_ref[...])

    sub = _MIX_SUBTILE
    zero = jnp.zeros((sub, LANES), BF16)
    for ii in range(SUBLANES):
        for lc in range(tm // LANES):
            ls = slice(lc * LANES, (lc + 1) * LANES)
            cb = [jnp.broadcast_to(cnt_ref[p, ii:ii + 1, ls], (sub, LANES)).astype(BF16) for p in range(PEER_HEADS)]
            ab = [jnp.broadcast_to(a_ref[p, ii:ii + 1, ls], (sub, LANES)).astype(BF16) for p in range(PEER_HEADS)]
            for js in range(N_KEYS // sub):
                jr = slice(js * sub, (js + 1) * sub)
                terms = [jnp.maximum(jnp.minimum(ab[p] * b_ref[p, jr, ls], cb[p] - rk_ref[p, jr, ls]), zero)
                         for p in range(PEER_HEADS)]
                while len(terms) > 1:
                    terms = [terms[i] + terms[i + 1] for i in range(0, len(terms), 2)]
                w = terms[0]
                rs = slice(ii * N_KEYS + js * sub, ii * N_KEYS + (js + 1) * sub)
                y_s[rs, ls] = w * _gelu_tanh(act_s[rs, ls].astype(BF16))
    acc_s[...] += _dot(vt_ref[0], y_s[...])

    @pl.when(c == pl.num_programs(1) - 1)
    def _():
        o_ref[...] = acc_s[...].T


def _peer_mix(h2, u_b, vt_c, cnt, rk, a, b, tm):
    r, d = h2.shape
    n_exp = u_b.shape[0]
    ch = _EXPERT_CHUNK
    assert n_exp == N_KEYS * N_KEYS and vt_c.shape == (n_exp // ch, d, ch)
    row_blk = pl.BlockSpec((PEER_HEADS, SUBLANES, tm), lambda i, c: (0, c, i))
    all_blk = pl.BlockSpec((PEER_HEADS, N_KEYS, tm), lambda i, c: (0, 0, i))
    return pl.pallas_call(
        _peer_mix_kernel,
        grid=(r // tm, n_exp // ch),
        in_specs=[pl.BlockSpec((tm, d), lambda i, c: (i, 0)),
                  pl.BlockSpec((ch, d), lambda i, c: (c, 0)),
                  pl.BlockSpec((1, d, ch), lambda i, c: (c, 0, 0)),
                  row_blk, all_blk, row_blk, all_blk],
        out_specs=pl.BlockSpec((tm, d), lambda i, c: (i, 0)),
        out_shape=jax.ShapeDtypeStruct((r, d), F32),
        scratch_shapes=[pltpu.VMEM((ch, tm), F32), pltpu.VMEM((ch, tm), BF16), pltpu.VMEM((d, tm), F32)],
        compiler_params=_cparams(("parallel", "arbitrary")),
        name="peer_mix",
    )(h2, u_b, vt_c, cnt, rk, a, b)


def _peer(h2, w_route, u_b, vt_b, tm_route, tm_mix):
    cnt, rk, a, b = _peer_route(h2, w_route, tm_route)
    return _peer_mix(h2, u_b, vt_b, cnt, rk, a, b, tm_mix)


def _final_kernel(x_ref, f_ref, g2_ref, lg_ref, lb_ref, o_ref):
    z = ALPHA * x_ref[...] + g2_ref[0] * f_ref[...]
    o_ref[...] = _standardize(z) * lg_ref[...] + lb_ref[...]


def _final(x1, f, g2, ln_g, ln_b, tm, rows_per_mod):
    r, d = x1.shape
    m = g2.shape[1]
    if m == 1:
        mod_map = lambda i: ((i * tm) // rows_per_mod, 0, 0)
    else:
        mod_map = lambda i: (i, 0, 0)
    vec = pl.BlockSpec((1, d), lambda i: (0, 0))
    return pl.pallas_call(
        _final_kernel,
        grid=(r // tm,),
        in_specs=[pl.BlockSpec((tm, d), lambda i: (i, 0)),
                  pl.BlockSpec((tm, d), lambda i: (i, 0)),
                  pl.BlockSpec((1, m, d), mod_map), vec, vec],
        out_specs=pl.BlockSpec((tm, d), lambda i: (i, 0)),
        out_shape=jax.ShapeDtypeStruct((r, d), F32),
        compiler_params=_cparams(("parallel",)),
        name="final_norm",
    )(x1, f, g2, ln_g, ln_b)


_PAGES_PER_STEP = 64


def _page_sum_kernel(c_ref, o_ref):
    pp, n_h, dh, page = c_ref.shape
    width = n_h * dh
    lane = lax.broadcasted_iota(jnp.int32, (width, LANES), 1)
    t = jnp.zeros((width, LANES), F32)
    for pg in range(pp):
        col = jnp.sum(c_ref[pg].reshape(width, page), axis=-1, keepdims=True)
        t = jnp.where(lane == pg, col, t)
    o_ref[...] = t.T[0:pp, :]


def _page_sums(cache_t):
    n_phys, n_h, dh, page = cache_t.shape
    pp = _PAGES_PER_STEP
    assert n_phys % pp == 0 and pp <= LANES
    return pl.pallas_call(
        _page_sum_kernel,
        grid=(n_phys // pp,),
        in_specs=[pl.BlockSpec((pp, n_h, dh, page), lambda i: (i, 0, 0, 0))],
        out_specs=pl.BlockSpec((pp, n_h * dh), lambda i: (i, 0)),
        out_shape=jax.ShapeDtypeStruct((n_phys, n_h * dh), F32),
        compiler_params=_cparams(("parallel",)),
        name="page_sums",
    )(cache_t)


def _block_gate_kernel(pt_ref, ps_ref, q_ref, sel_ref, km_s, *, n_blocks):
    b = pl.program_id(0)
    ppb = MOBA_BLOCK // PAGE_SIZE
    width = ps_ref.shape[1]

    km_s[...] = jnp.zeros(km_s.shape, F32)

    def gather(n, carry):
        acc = jnp.zeros((1, width), F32)
        for j in range(ppb):
            acc = acc + ps_ref[pl.ds(pt_ref[b, n * ppb + j], 1), :]
        km_s[pl.ds(n, 1), :] = acc * (1.0 / MOBA_BLOCK)
        return carry

    lax.fori_loop(0, n_blocks, gather, 0)
    q = q_ref[0]
    sub = lax.broadcasted_iota(jnp.int32, (N_ATTN_HEADS, width), 0)
    lane_w = lax.broadcasted_iota(jnp.int32, (N_ATTN_HEADS, width), 1)
    qb = jnp.where(lane_w // ATTN_HEAD_DIM == sub, jnp.broadcast_to(q, (N_ATTN_HEADS, width)), 0.0)
    gate = _dot_nt(qb, km_s[...], precision=HIGHEST)
    lane = lax.broadcasted_iota(jnp.int32, gate.shape, 1)
    g = jnp.where(lane < n_blocks, gate, NEG_INF)
    out = jnp.zeros(gate.shape, jnp.int32)
    for k in range(MOBA_TOPK):
        mx = jnp.max(g, axis=-1, keepdims=True)
        idx = jnp.min(jnp.where(g == mx, lane, LANES), axis=-1, keepdims=True)
        out = jnp.where(lane == k, idx, out)
        g = jnp.where(lane == idx, NEG_INF, g)
    sel_ref[0] = out


def _block_gate(page_table, page_sum2d, q3, n_blocks):
    db = q3.shape[0]
    assert n_blocks <= LANES and n_blocks >= MOBA_TOPK
    grid_spec = pltpu.PrefetchScalarGridSpec(
        num_scalar_prefetch=1,
        grid=(db,),
        in_specs=[pl.BlockSpec(page_sum2d.shape, lambda i, pt: (0, 0)),
                  pl.BlockSpec((1, 1, q3.shape[2]), lambda i, pt: (i, 0, 0))],
        out_specs=pl.BlockSpec((1, N_ATTN_HEADS, LANES), lambda i, pt: (i, 0, 0)),
        scratch_shapes=[pltpu.VMEM((LANES, page_sum2d.shape[1]), F32)],
    )
    return pl.pallas_call(
        functools.partial(_block_gate_kernel, n_blocks=n_blocks),
        grid_spec=grid_spec,
        out_shape=jax.ShapeDtypeStruct((db, N_ATTN_HEADS, LANES), jnp.int32),
        compiler_params=_cparams(("arbitrary",)),
        name="block_gate",
    )(page_table, page_sum2d, q3)


_PAGES_PER_BLOCK = MOBA_BLOCK // PAGE_SIZE
_SEL_PAGES = MOBA_TOPK * _PAGES_PER_BLOCK


def _sample_page_copies(pt_ref, sel_ref, k_hbm, v_hbm, kbuf, vbuf, sem, bb, par):
    out = []
    for h in range(N_ATTN_HEADS):
        for kt in range(MOBA_TOPK):
            blk = sel_ref[bb, h * MOBA_TOPK + kt]
            for pp in range(_PAGES_PER_BLOCK):
                page = pt_ref[bb, blk * _PAGES_PER_BLOCK + pp]
                slot = h * _SEL_PAGES + kt * _PAGES_PER_BLOCK + pp
                out.append(pltpu.make_async_copy(k_hbm.at[page, h], kbuf.at[par, slot], sem.at[0, par]))
                out.append(pltpu.make_async_copy(v_hbm.at[page, h], vbuf.at[par, slot], sem.at[1, par]))
    return out


def _col_from_row(row):
    n = row.shape[1]
    r = lax.broadcasted_iota(jnp.int32, (n, n), 0)
    c = lax.broadcasted_iota(jnp.int32, (n, n), 1)
    return jnp.sum(jnp.where(r == c, jnp.broadcast_to(row, (n, n)), 0.0), axis=-1, keepdims=True)


def _row_from_col(col):
    n = col.shape[0]
    r = lax.broadcasted_iota(jnp.int32, (n, n), 0)
    c = lax.broadcasted_iota(jnp.int32, (n, n), 1)
    return jnp.sum(jnp.where(r == c, jnp.broadcast_to(col, (n, n)), 0.0), axis=0, keepdims=True)


def _moba_sample_kernel(pt_ref, sel_ref, tbl_ref, q_ref, kn_ref, vn_ref, k_hbm, v_hbm, o_ref,
                        kbuf, vbuf, bias_s, sem, *, past_len):
    b = pl.program_id(0)
    nb = pl.num_programs(0)
    par = b % 2
    scale = ATTN_HEAD_DIM ** -0.5
    copies = functools.partial(_sample_page_copies, pt_ref, sel_ref, k_hbm, v_hbm, kbuf, vbuf, sem)

    @pl.when(b == 0)
    def _():
        for c in copies(0, 0):
            c.start()

    @pl.when(b + 1 < nb)
    def _():
        for c in copies(b + 1, 1 - par):
            c.start()

    for c in copies(b, par):
        c.wait()

    sub = lax.broadcasted_iota(jnp.int32, (_PAGES_PER_BLOCK, PAGE_SIZE), 0)
    lane = lax.broadcasted_iota(jnp.int32, (_PAGES_PER_BLOCK, PAGE_SIZE), 1)
    for h in range(N_ATTN_HEADS):
        q = q_ref[0, h:h + 1, :]
        q_col = _col_from_row(q)
        for kt in range(MOBA_TOPK):
            pos0 = sel_ref[b, h * MOBA_TOPK + kt] * MOBA_BLOCK
            near = past_len - pos0 - (MOBA_BLOCK - 1) < MAX_DISTANCE
            rows = slice(kt * _PAGES_PER_BLOCK, (kt + 1) * _PAGES_PER_BLOCK)

            @pl.when(near)
            def _():
                dist = jnp.maximum(past_len - (pos0 + sub * PAGE_SIZE + lane), 0)
                bias_s[rows, :] = _bias_from_bucket(_t5_bucket(dist), tbl_ref, h)

            @pl.when(jnp.logical_not(near))
            def _():
                bias_s[rows, :] = jnp.full((_PAGES_PER_BLOCK, PAGE_SIZE), tbl_ref[N_BUCKETS - 1, h], F32)

        s = jnp.concatenate([jnp.sum(kbuf[par, h * _SEL_PAGES + j] * q_col, axis=0, keepdims=True)
                             for j in range(_SEL_PAGES)], axis=0) * scale + bias_s[0:_SEL_PAGES, :]
        s_new = jnp.sum(kn_ref[0, h:h + 1, :] * q, axis=-1, keepdims=True) * scale + tbl_ref[0, h]
        m = jnp.maximum(jnp.max(jnp.max(s, axis=-1, keepdims=True), axis=0, keepdims=True), s_new)
        p = jnp.exp(s - m)
        p_new = jnp.exp(s_new - m)
        den = jnp.sum(jnp.sum(p, axis=-1, keepdims=True), axis=0, keepdims=True) + p_new
        pv = vbuf[par, h * _SEL_PAGES] * p[0:1, :]
        for j in range(1, _SEL_PAGES):
            pv = pv + vbuf[par, h * _SEL_PAGES + j] * p[j:j + 1, :]
        num = _row_from_col(jnp.sum(pv, axis=-1, keepdims=True)) + p_new * vn_ref[0, h:h + 1, :]
        o_ref[0, h:h + 1, :] = num / den


def _moba_sample(page_table, sel, table, q3, k3, v3, cache_kt, cache_vt, past_len):
    db, n_h, dh = q3.shape
    assert cache_kt.shape[1:] == (n_h, dh, PAGE_SIZE) and PAGE_SIZE == LANES and _SEL_PAGES <= SUBLANES
    vec = pl.BlockSpec((1, n_h, dh), lambda i, pt, sl: (i, 0, 0))
    grid_spec = pltpu.PrefetchScalarGridSpec(
        num_scalar_prefetch=2,
        grid=(db,),
        in_specs=[pl.BlockSpec(memory_space=pltpu.SMEM), vec, vec, vec,
                  pl.BlockSpec(memory_space=pl.ANY), pl.BlockSpec(memory_space=pl.ANY)],
        out_specs=vec,
        scratch_shapes=[pltpu.VMEM((2, n_h * _SEL_PAGES, dh, PAGE_SIZE), F32),
                        pltpu.VMEM((2, n_h * _SEL_PAGES, dh, PAGE_SIZE), F32),
                        pltpu.VMEM((SUBLANES, PAGE_SIZE), F32),
                        pltpu.SemaphoreType.DMA((2, 2))],
    )
    return pl.pallas_call(
        functools.partial(_moba_sample_kernel, past_len=past_len),
        grid_spec=grid_spec,
        out_shape=jax.ShapeDtypeStruct((db, n_h, dh), F32),
        compiler_params=_cparams(("arbitrary",)),
        name="moba_sample",
    )(page_table, sel, table, q3, k3, v3, cache_kt, cache_vt)


def _mlstm_step_kernel(qk_ref, cs_ref, v_ref, o_ref, g_ref, cw_ref, cb_ref, bg_ref,
                       c_ref, n_ref, m_ref, mem_ref, c_out, n_out, m_out):
    dh = M_HEAD_DIM
    hist = CONV_WIDTH - 1
    y = cb_ref[...] + qk_ref[0] * cw_ref[hist:hist + 1, :]
    for j in range(hist):
        y = y + cs_ref[0, j:j + 1, :] * cw_ref[j:j + 1, :]
    y = y * _sigmoid(y)
    pre = g_ref[0] + bg_ref[...]
    row = lax.broadcasted_iota(jnp.int32, (dh, dh), 0)
    col = lax.broadcasted_iota(jnp.int32, (dh, dh), 1)
    lane = lax.broadcasted_iota(jnp.int32, (1, LANES), 1)
    m_all = jnp.zeros((1, LANES), F32)
    for h in range(N_M_HEADS):
        sl = slice(h * dh, (h + 1) * dh)
        q = y[:, sl]
        k = y[:, M_WIDTH + h * dh:M_WIDTH + (h + 1) * dh] * (dh ** -0.5)
        v = v_ref[0, :, sl]
        i_t = pre[:, h:h + 1]
        logf = _log_sigmoid(pre[:, N_M_HEADS + h:N_M_HEADS + h + 1])
        c_prev = c_ref[0, h]
        n_prev = n_ref[0, h:h + 1, :]
        m_prev = m_ref[0, :, h:h + 1]
        inter = logf + m_prev
        m_t = jnp.maximum(inter, i_t)
        w_inter = jnp.exp(inter - m_t)
        s = jnp.sum(q * k, axis=-1, keepdims=True) * jnp.exp(i_t - m_t)
        cq = _dot_nt(jnp.broadcast_to(q, (SUBLANES, dh)), c_prev, precision=HIGHEST)[0:1, :]
        num = w_inter * cq + s * v
        den = w_inter * jnp.sum(n_prev * q, axis=-1, keepdims=True) + s
        hh = num / jnp.maximum(jnp.abs(den), jnp.exp(-m_t))
        mem_ref[0, :, sl] = _sigmoid(o_ref[0, :, sl]) * hh
        wc = jnp.exp(inter - m_t)
        ws = jnp.exp(i_t - m_t)
        v_col = jnp.sum(jnp.where(row == col, jnp.broadcast_to(v, (dh, dh)), 0.0), axis=-1, keepdims=True)
        c_out[0, h] = wc * c_prev + (ws * v_col) * k
        n_out[0, h:h + 1, :] = wc * n_prev + ws * k
        m_all = jnp.where(lane == h, m_t, m_all)
    m_out[0] = m_all


def _mlstm_step(mqk, cstate, mv, mo, gates, conv_w, conv_b, bg_row, c0, n0, m0):
    db = mqk.shape[0]
    r3 = lambda w: pl.BlockSpec((1, 1, w), lambda i: (i, 0, 0))
    return pl.pallas_call(
        _mlstm_step_kernel,
        grid=(db,),
        in_specs=[r3(2 * M_WIDTH),
                  pl.BlockSpec((1, CONV_WIDTH - 1, 2 * M_WIDTH), lambda i: (i, 0, 0)),
                  r3(M_WIDTH), r3(M_WIDTH), r3(LANES),
                  pl.BlockSpec((CONV_WIDTH, 2 * M_WIDTH), lambda i: (0, 0)),
                  pl.BlockSpec((1, 2 * M_WIDTH), lambda i: (0, 0)),
                  pl.BlockSpec((1, LANES), lambda i: (0, 0)),
                  pl.BlockSpec((1, N_M_HEADS, M_HEAD_DIM, M_HEAD_DIM), lambda i: (i, 0, 0, 0)),
                  pl.BlockSpec((1, N_M_HEADS, M_HEAD_DIM), lambda i: (i, 0, 0)),
                  pl.BlockSpec((1, 1, N_M_HEADS), lambda i: (i, 0, 0))],
        out_specs=[r3(M_WIDTH),
                   pl.BlockSpec((1, N_M_HEADS, M_HEAD_DIM, M_HEAD_DIM), lambda i: (i, 0, 0, 0)),
                   pl.BlockSpec((1, N_M_HEADS, M_HEAD_DIM), lambda i: (i, 0, 0)),
                   r3(LANES)],
        out_shape=[jax.ShapeDtypeStruct((db, 1, M_WIDTH), F32),
                   jax.ShapeDtypeStruct((db, N_M_HEADS, M_HEAD_DIM, M_HEAD_DIM), F32),
                   jax.ShapeDtypeStruct((db, N_M_HEADS, M_HEAD_DIM), F32),
                   jax.ShapeDtypeStruct((db, 1, LANES), F32)],
        compiler_params=_cparams(("parallel",)),
        name="mlstm_step",
    )(mqk, cstate, mv, mo, gates, conv_w, conv_b, bg_row, c0, n0, m0)


def _pad_rows(x, mult):
    r = x.shape[0]
    rp = -(-r // mult) * mult
    return x if rp == r else jnp.pad(x, ((0, rp - r), (0, 0)))


def kernel(x_prompt, x_sample, cache_k, cache_v, page_table, state_C, state_n, state_m, state_conv,
           c_prompt, c_sample, rel_bias_table, w_ada, b_ada, w_in, b_gate, conv_w, conv_b,
           beta_attn, beta_mlstm, w_out, ln1_g, ln1_b, w_query, sub_keys, expert_u, expert_v,
           ln2_g, ln2_b):
    assert w_ada.shape[0] == DEPTH == 1
    B, S, D = x_prompt.shape
    DB, T, _ = x_sample.shape
    assert T == 1
    H, dh = N_ATTN_HEADS, ATTN_HEAD_DIM
    past_len = page_table.shape[1] * PAGE_SIZE
    assert past_len % MOBA_BLOCK == 0
    l = 0

    gate_cols = 2 * N_M_HEADS
    w_in_p = jnp.pad(w_in[l], ((0, 0), (0, LANES - gate_cols))).astype(BF16)
    w_out_b = w_out[l].astype(BF16)
    w_route = _route_weights(sub_keys[l], w_query[l])
    u_b = expert_u[l].astype(BF16)
    n_exp = expert_v.shape[1]
    vt_b = jnp.transpose(expert_v[l].reshape(n_exp // _EXPERT_CHUNK, _EXPERT_CHUNK, D), (0, 2, 1)).astype(BF16)
    beta = jnp.concatenate([beta_attn[l], beta_mlstm[l]])[None, :]
    bg = b_gate[l]
    bg_row = jnp.pad(bg, (0, LANES - gate_cols))[None, :]
    cw, cb = conv_w[l], conv_b[l][None, :]
    table = rel_bias_table
    lg1, lb1, lg2, lb2 = ln1_g[l][None, :], ln1_b[l][None, :], ln2_g[l][None, :], ln2_b[l][None, :]

    mod = _ada(jnp.concatenate([c_prompt, c_sample], axis=0), w_ada[l], b_ada[l])
    sh1, sc1, g1, sh2, sc2, g2 = [mod[:, i * D:(i + 1) * D] for i in range(6)]
    sc1, sc2 = 1.0 + sc1, 1.0 + sc2
    pm = lambda t: t[:B][:, None, :]
    sm = lambda t: t[B:][None, :, :]

    xp2 = x_prompt.reshape(B * S, D)
    w_in_tb = w_in[l].T[:3 * ATTN_WIDTH].astype(BF16)
    aqt, ak, akt, avt, avb, mqk, mv, mo, gates = _inproj_prompt(x_prompt, pm(sc1), pm(sh1), w_in_p, w_in_tb)
    attn = _moba_prompt(aqt, ak.reshape(B, S, ATTN_WIDTH), avb, table, _bias_tiles(table))
    mqk3 = mqk.reshape(B, S, 2 * M_WIDTH)
    mem, c_p, n_p, m_p = _mlstm_prompt(mqk3, mv.reshape(B, S, M_WIDTH), mo.reshape(B, S, M_WIDTH),
                                       gates.reshape(B, S, LANES), cw, cb, bg_row)
    x1, h2 = _outproj(attn, mem.reshape(B * S, M_WIDTH), xp2, beta, w_out_b,
                      pm(g1), pm(sc2), pm(sh2), lg1, lb1, 256, S)
    f = _peer(h2, w_route, u_b, vt_b, 256, 512)
    y_prompt = _final(x1, f, pm(g2), lg2, lb2, 512, S).reshape(B, S, D)
    from_t = lambda t: jnp.transpose(t.reshape(B, H, dh, S), (0, 3, 1, 2))[None]
    k_prompt = from_t(akt)
    v_prompt = from_t(avt)
    conv_prompt = mqk3[:, S - (CONV_WIDTH - 1):, :][None]

    xs2 = x_sample.reshape(DB, D)
    saq, sak, sav, smqk, smv, smo, sgates = _inproj(xs2, sm(sc1), sm(sh1), w_in_p, DB, 1)
    cache_kt = jnp.transpose(cache_k[l], (0, 2, 3, 1))
    cache_vt = jnp.transpose(cache_v[l], (0, 2, 3, 1))
    psum = _page_sums(cache_kt)
    n_blocks = past_len // MOBA_BLOCK
    sel = _block_gate(page_table, psum, saq.reshape(DB, 1, H * dh), n_blocks)
    sel = sel[:, :, :MOBA_TOPK].reshape(DB, H * MOBA_TOPK)
    h3 = lambda t: t.reshape(DB, H, dh)
    s_attn = _moba_sample(page_table, sel, table, h3(saq), h3(sak), h3(sav), cache_kt, cache_vt, past_len)
    s_mem, c_s, n_s, m_s = _mlstm_step(
        smqk.reshape(DB, 1, 2 * M_WIDTH), state_conv[l], smv.reshape(DB, 1, M_WIDTH),
        smo.reshape(DB, 1, M_WIDTH), sgates.reshape(DB, 1, LANES), cw, cb, bg_row,
        state_C[l], state_n[l], state_m[l].reshape(DB, 1, N_M_HEADS))
    sx1, sh2_ = _outproj(s_attn.reshape(DB, H * dh), s_mem.reshape(DB, M_WIDTH), xs2, beta, w_out_b,
                         sm(g1), sm(sc2), sm(sh2), lg1, lb1, DB, 1)
    sf = _peer(_pad_rows(sh2_, LANES), w_route, u_b, vt_b, LANES, LANES)[:DB]
    y_sample = _final(sx1, sf, sm(g2), lg2, lb2, DB, 1).reshape(DB, 1, D)
    conv_sample = jnp.concatenate([state_conv[l][:, 1:, :], smqk.reshape(DB, 1, 2 * M_WIDTH)], axis=1)[None]

    return (y_prompt, y_sample,
            k_prompt, v_prompt, c_p[None], n_p[None], m_p[:, :N_M_HEADS, 0][None], conv_prompt,
            sak.reshape(1, DB, 1, H, dh), sav.reshape(1, DB, 1, H, dh),
            c_s[None], n_s[None], m_s[:, 0, :N_M_HEADS][None], conv_sample)
```

```python
import functools
import math

import numpy as np
import jax
import jax.numpy as jnp
from jax import lax
from jax.experimental import pallas as pl
from jax.experimental.pallas import tpu as pltpu

F32 = jnp.float32
BF16 = jnp.bfloat16
NEG_INF = float("-inf")
HIGHEST = lax.Precision.HIGHEST

N_ATTN_HEADS = 8
ATTN_HEAD_DIM = 64
ATTN_WIDTH = N_ATTN_HEADS * ATTN_HEAD_DIM
MOBA_BLOCK = 256
MOBA_TOPK = 3
PAGE_SIZE = 128
N_BUCKETS = 32
MAX_DISTANCE = 128
N_M_HEADS = 4
M_HEAD_DIM = 128
M_WIDTH = N_M_HEADS * M_HEAD_DIM
CONV_WIDTH = 4
M_CHUNK = 128
N_KEYS = 128
PEER_HEADS = 8
PEER_KEY_DIM = 256
PEER_TOPK = 16
LN_EPS = 1e-5
DEPTH = 1
ALPHA = (2.0 * DEPTH) ** 0.25

LANES = 128
SUBLANES = 8
VMEM_LIMIT = 56 * 1024 * 1024


def _cparams(sem, flags=None):
    return pltpu.CompilerParams(dimension_semantics=sem, vmem_limit_bytes=VMEM_LIMIT, flags=flags)


def _bucket_thresholds():
    max_exact = N_BUCKETS // 2
    d = np.arange(0, MAX_DISTANCE + 1)
    far = max_exact + (np.log(np.maximum(d, 1) / max_exact) / math.log(MAX_DISTANCE / max_exact)
                       * (N_BUCKETS - max_exact)).astype(np.int64)
    bucket = np.where(d < max_exact, d, np.minimum(far, N_BUCKETS - 1))
    assert np.all(np.diff(bucket) >= 0) and bucket[-1] == N_BUCKETS - 1
    return tuple(int(np.argmax(bucket >= k)) for k in range(max_exact + 1, N_BUCKETS))


_BUCKET_THRESHOLDS = _bucket_thresholds()


def _t5_bucket(dist):
    max_exact = N_BUCKETS // 2
    far = jnp.full(dist.shape, max_exact, jnp.int32)
    for t in _BUCKET_THRESHOLDS:
        far = far + (dist >= t).astype(jnp.int32)
    return jnp.where(dist < max_exact, dist, far)


def _bias_from_bucket(bucket, tbl_ref, h):
    out = jnp.zeros(bucket.shape, F32)
    for j in range(N_BUCKETS):
        out = jnp.where(bucket == j, tbl_ref[j, h], out)
    return out


def _standardize(x):
    mu = jnp.mean(x, axis=-1, keepdims=True)
    xc = x - mu
    var = jnp.mean(xc * xc, axis=-1, keepdims=True)
    return xc * lax.rsqrt(var + LN_EPS)


def _sigmoid(x):
    return 1.0 / (1.0 + jnp.exp(-x))


def _log_sigmoid(x):
    return jnp.minimum(x, 0.0) - jnp.log1p(jnp.exp(-jnp.abs(x)))


def _gelu_tanh(x):
    c = math.sqrt(2.0 / math.pi)
    hx = 0.5 * x
    return hx + hx * jnp.tanh(x * (c + (c * 0.044715) * (x * x)))


def _dot_nt(a, b, **kw):
    return lax.dot_general(a, b, (((1,), (1,)), ((), ())), preferred_element_type=F32, **kw)


def _dot(a, b, **kw):
    return jnp.dot(a, b, preferred_element_type=F32, **kw)


def _ada_kernel(c_ref, w_ref, b_ref, o_ref):
    c = c_ref[...]
    s = c * _sigmoid(c)
    o_ref[...] = _dot(s, w_ref[...], precision=HIGHEST) + b_ref[...]


def _ada(c_all, w_ada, b_ada):
    n, d = c_all.shape
    n_out = w_ada.shape[1]
    tn = 1024
    return pl.pallas_call(
        _ada_kernel,
        grid=(n_out // tn,),
        in_specs=[pl.BlockSpec((n, d), lambda j: (0, 0)),
                  pl.BlockSpec((d, tn), lambda j: (0, j)),
                  pl.BlockSpec((1, tn), lambda j: (0, j))],
        out_specs=pl.BlockSpec((n, tn), lambda j: (0, j)),
        out_shape=jax.ShapeDtypeStruct((n, n_out), F32),
        compiler_params=_cparams(("parallel",)),
        name="ada_mod",
    )(c_all, w_ada, b_ada.reshape(1, n_out))


_PROJ_GROUPS = (ATTN_WIDTH, ATTN_WIDTH, ATTN_WIDTH, 2 * M_WIDTH, M_WIDTH, M_WIDTH, LANES)
_PROJ_OFFS = tuple(int(v) for v in np.cumsum((0,) + _PROJ_GROUPS))


def _inproj_kernel(x_ref, sc_ref, sh_ref, w_ref, *o_refs):
    h = _standardize(x_ref[...]) * sc_ref[0] + sh_ref[0]
    hb = h.astype(BF16)
    for g, o_ref in enumerate(o_refs):
        o_ref[...] = _dot(hb, w_ref[:, _PROJ_OFFS[g]:_PROJ_OFFS[g + 1]])


def _inproj(x2d, sc3, sh3, w_in_b, tm, rows_per_mod):
    r, d = x2d.shape
    m = sc3.shape[1]
    if m == 1:
        mod_map = lambda i: ((i * tm) // rows_per_mod, 0, 0)
    else:
        mod_map = lambda i: (i, 0, 0)
    return pl.pallas_call(
        _inproj_kernel,
        grid=(r // tm,),
        in_specs=[pl.BlockSpec((tm, d), lambda i: (i, 0)),
                  pl.BlockSpec((1, m, d), mod_map),
                  pl.BlockSpec((1, m, d), mod_map),
                  pl.BlockSpec(w_in_b.shape, lambda i: (0, 0))],
        out_specs=[pl.BlockSpec((tm, g), lambda i: (i, 0)) for g in _PROJ_GROUPS],
        out_shape=[jax.ShapeDtypeStruct((r, g), F32) for g in _PROJ_GROUPS],
        compiler_params=_cparams(("parallel",)),
        name="inproj",
    )(x2d, sc3, sh3, w_in_b)


def _inproj_prompt_kernel(x_ref, sc_ref, sh_ref, w_ref, wt_ref, qt_ref, k_ref, kt_ref, vt_ref, vb_ref,
                          mqk_ref, mv_ref, mo_ref, g_ref):
    h = _standardize(x_ref[...]) * sc_ref[0] + sh_ref[0]
    hb = h.astype(BF16)
    aw = ATTN_WIDTH
    k_ref[...] = _dot(hb, w_ref[:, _PROJ_OFFS[1]:_PROJ_OFFS[2]])
    for g, o_ref in ((3, mqk_ref), (4, mv_ref), (5, mo_ref), (6, g_ref)):
        o_ref[...] = _dot(hb, w_ref[:, _PROJ_OFFS[g]:_PROJ_OFFS[g + 1]])
    qt_ref[0] = _dot_nt(wt_ref[0:aw, :], hb)
    kt_ref[0] = _dot_nt(wt_ref[aw:2 * aw, :], hb)
    vt = _dot_nt(wt_ref[2 * aw:3 * aw, :], hb)
    vt_ref[0] = vt
    vb_ref[0, 0] = vt


def _inproj_prompt(x3, sc3, sh3, w_in_b, w_in_tb):
    b, s, d = x3.shape
    tm = MOBA_BLOCK
    nt = s // tm
    r = b * s
    aw = ATTN_WIDTH
    assert s % tm == 0
    row = lambda w: pl.BlockSpec((tm, w), lambda i: (i, 0))
    tr = pl.BlockSpec((1, aw, tm), lambda i: (i // nt, 0, i % nt))
    mod = pl.BlockSpec((1, 1, d), lambda i: (i // nt, 0, 0))
    return pl.pallas_call(
        _inproj_prompt_kernel,
        grid=(r // tm,),
        in_specs=[row(d), mod, mod,
                  pl.BlockSpec(w_in_b.shape, lambda i: (0, 0)),
                  pl.BlockSpec(w_in_tb.shape, lambda i: (0, 0))],
        out_specs=[tr, row(aw), tr, tr,
                   pl.BlockSpec((1, 1, aw, tm), lambda i: (i // nt, i % nt, 0, 0)),
                   row(2 * M_WIDTH), row(M_WIDTH), row(M_WIDTH), row(LANES)],
        out_shape=[jax.ShapeDtypeStruct((b, aw, s), F32), jax.ShapeDtypeStruct((r, aw), F32),
                   jax.ShapeDtypeStruct((b, aw, s), F32), jax.ShapeDtypeStruct((b, aw, s), F32),
                   jax.ShapeDtypeStruct((b, nt, aw, tm), F32),
                   jax.ShapeDtypeStruct((r, 2 * M_WIDTH), F32), jax.ShapeDtypeStruct((r, M_WIDTH), F32),
                   jax.ShapeDtypeStruct((r, M_WIDTH), F32), jax.ShapeDtypeStruct((r, LANES), F32)],
        compiler_params=_cparams(("parallel",)),
        name="inproj_prompt",
    )(x3.reshape(r, d), sc3, sh3, w_in_b, w_in_tb)


def _bias_tiles_kernel(tbl_ref, o_ref):
    h = pl.program_id(0)
    key = lax.broadcasted_iota(jnp.int32, (MOBA_BLOCK, MOBA_BLOCK), 0)
    qry = lax.broadcasted_iota(jnp.int32, (MOBA_BLOCK, MOBA_BLOCK), 1)
    for t in range(2):
        dist = jnp.maximum(qry - key + t * MOBA_BLOCK, 0)
        o_ref[0, t] = _bias_from_bucket(_t5_bucket(dist), tbl_ref, h)


def _bias_tiles(table):
    n_h = table.shape[1]
    return pl.pallas_call(
        _bias_tiles_kernel,
        grid=(n_h,),
        in_specs=[pl.BlockSpec(memory_space=pltpu.SMEM)],
        out_specs=pl.BlockSpec((1, 2, MOBA_BLOCK, MOBA_BLOCK), lambda h: (h, 0, 0, 0)),
        out_shape=jax.ShapeDtypeStruct((n_h, 2, MOBA_BLOCK, MOBA_BLOCK), F32),
        compiler_params=_cparams(("parallel",)),
        name="moba_bias_tiles",
    )(table)


_HEADS_PER_STEP = 4
_HEAD_GROUP_COLS = _HEADS_PER_STEP * ATTN_HEAD_DIM


def _moba_prompt_kernel(tbl_ref, qt_ref, k_ref, vb_ref, bias_ref, o_ref, selb_s, *, n_blocks):
    hp = pl.program_id(1)
    ob = pl.program_id(2)
    blk = MOBA_BLOCK
    dh = ATTN_HEAD_DIM
    gc = _HEAD_GROUP_COLS
    scale = dh ** -0.5
    heads = range(_HEADS_PER_STEP)

    kmean = jnp.concatenate(
        [jnp.sum(k_ref[0, n * blk:(n + 1) * blk, :], axis=0, keepdims=True) * (1.0 / blk)
         for n in range(n_blocks)], axis=0)
    sub = lax.broadcasted_iota(jnp.int32, (n_blocks, blk), 0)
    key = lax.broadcasted_iota(jnp.int32, (blk, blk), 0)
    qry = lax.broadcasted_iota(jnp.int32, (blk, blk), 1)
    start = pl.multiple_of(ob * blk, blk)
    fsub = lax.broadcasted_iota(jnp.int32, (gc, blk), 0)
    k_own = k_ref[0, pl.ds(start, blk), :].astype(BF16)
    c_fars = [tbl_ref[N_BUCKETS - 1, hp * _HEADS_PER_STEP + j] for j in heads]

    qzs = [jnp.where((fsub >= j * dh) & (fsub < (j + 1) * dh), qt_ref[0], 0.0) for j in heads]
    qzbs = [qz.astype(BF16) for qz in qzs]
    gates = [_dot(kmean, qz, precision=HIGHEST) for qz in qzs]
    qk_own = [_dot(k_own, qzb) for qzb in qzbs]
    for j in heads:
        g = jnp.where(sub < ob, gates[j], NEG_INF)
        sel = jnp.zeros((n_blocks, blk), F32)
        for _ in range(MOBA_TOPK):
            mx = jnp.max(g, axis=0, keepdims=True)
            idx = jnp.min(jnp.where(g == mx, sub, n_blocks), axis=0, keepdims=True)
            hit = sub == idx
            sel = jnp.where(hit, 1.0, sel)
            g = jnp.where(hit, NEG_INF, g)
        sel = jnp.where(sub < ob, sel, 0.0)
        for n in range(n_blocks):
            selb_s[j, n] = jnp.broadcast_to(sel[n:n + 1, :], (SUBLANES, blk))
    own = []
    for j in heads:
        s = jnp.where(key <= qry, qk_own[j] * scale + bias_ref[j, 0], NEG_INF)
        m0 = jnp.max(s, axis=0, keepdims=True)
        p = jnp.exp(s - m0)
        own.append((m0, jnp.sum(p, axis=0, keepdims=True), p.astype(BF16)))
    init = tuple((own[j][0], own[j][1], _dot(vb_ref[0, ob, j * dh:(j + 1) * dh, :].astype(BF16), own[j][2]))
                 for j in heads)

    def body(n, carry):
        st = pl.multiple_of(n * blk, blk)
        k_n = k_ref[0, pl.ds(st, blk), :].astype(BF16)
        qk = [_dot(k_n, qzbs[j]) for j in heads]
        stats = []
        for j in heads:
            m, l, _ = carry[j]
            bias = jnp.where(n == ob - 1, bias_ref[j, 1], c_fars[j])
            s = jnp.where(selb_s[j, n][0:1, :] > 0.5, qk[j] * scale + bias, NEG_INF)
            m_new = jnp.maximum(m, jnp.max(s, axis=0, keepdims=True))
            a = jnp.exp(m - m_new)
            p = jnp.exp(s - m_new)
            stats.append((m_new, a * l + jnp.sum(p, axis=0, keepdims=True), a, p.astype(BF16)))
        pv = [_dot(vb_ref[0, n, j * dh:(j + 1) * dh, :].astype(BF16), stats[j][3]) for j in heads]
        return tuple((stats[j][0], stats[j][1], stats[j][2] * carry[j][2] + pv[j]) for j in heads)

    final = lax.fori_loop(0, ob, body, init)
    for j in heads:
        _, l, acc = final[j]
        o_ref[0, j * dh:(j + 1) * dh, :] = acc / l


def _moba_prompt(qt, k, vb, table, bias_tiles):
    b, aw, s = qt.shape
    blk = MOBA_BLOCK
    n_blocks = s // blk
    hps = _HEADS_PER_STEP
    gc = _HEAD_GROUP_COLS
    assert s % blk == 0 and MOBA_TOPK <= n_blocks <= SUBLANES and aw % gc == 0 and gc % LANES == 0
    return pl.pallas_call(
        functools.partial(_moba_prompt_kernel, n_blocks=n_blocks),
        grid=(b, aw // gc, n_blocks),
        in_specs=[pl.BlockSpec(memory_space=pltpu.SMEM),
                  pl.BlockSpec((1, gc, blk), lambda i, g, j: (i, g, j)),
                  pl.BlockSpec((1, s, gc), lambda i, g, j: (i, 0, g)),
                  pl.BlockSpec((1, n_blocks, gc, blk), lambda i, g, j: (i, 0, g, 0)),
                  pl.BlockSpec((hps, 2, blk, blk), lambda i, g, j: (g, 0, 0, 0))],
        out_specs=pl.BlockSpec((1, gc, blk), lambda i, g, j: (i, g, j)),
        out_shape=jax.ShapeDtypeStruct((b, aw, s), F32),
        scratch_shapes=[pltpu.VMEM((hps, n_blocks, SUBLANES, blk), F32)],
        compiler_params=_cparams(("parallel", "parallel", "parallel")),
        name="moba_prompt",
    )(table, qt, k, vb, bias_tiles)


_CONV_PAD = SUBLANES


def _mlstm_prompt_kernel(q_ref, k_ref, v_ref, o_ref, g_ref, cw_ref, cb_ref, bg_ref,
                         mem_ref, c_out, n_out, m_out, xq_s, xk_s, c_s, n_s, m_s):
    ci = pl.program_id(1)
    L = M_CHUNK
    dh = M_HEAD_DIM
    pad = _CONV_PAD
    hist = CONV_WIDTH - 1

    @pl.when(ci == 0)
    def _():
        xq_s[0:pad, :] = jnp.zeros((pad, M_WIDTH), F32)
        xk_s[0:pad, :] = jnp.zeros((pad, M_WIDTH), F32)
        c_s[...] = jnp.zeros(c_s.shape, F32)
        n_s[...] = jnp.zeros(n_s.shape, F32)
        m_s[...] = jnp.zeros(m_s.shape, F32)

    xq_s[pad:pad + L, :] = q_ref[0]
    xk_s[pad:pad + L, :] = k_ref[0]

    def conv(x_s, col0):
        y = cb_ref[:, col0:col0 + M_WIDTH]
        for j in range(CONV_WIDTH):
            y = y + x_s[pad - hist + j:pad - hist + j + L, :] * cw_ref[j:j + 1, col0:col0 + M_WIDTH]
        return y * _sigmoid(y)

    qc = conv(xq_s, 0)
    kc = conv(xk_s, M_WIDTH) * (dh ** -0.5)
    tq = xq_s[pad + L - hist:pad + L, :]
    tk = xk_s[pad + L - hist:pad + L, :]
    xq_s[pad - hist:pad, :] = tq
    xk_s[pad - hist:pad, :] = tk

    heads = range(N_M_HEADS)
    hs = [slice(h * dh, (h + 1) * dh) for h in heads]
    qs = [qc[:, hs[h]] for h in heads]
    ks = [kc[:, hs[h]] for h in heads]
    vs = [v_ref[0, :, hs[h]] for h in heads]
    qbs = [q.astype(BF16) for q in qs]
    kbs = [k.astype(BF16) for k in ks]
    c_prevs = [c_s[h] for h in heads]
    n_prevs = [n_s[h:h + 1, :] for h in heads]
    m_prevs = [m_s[h:h + 1, 0:1] for h in heads]
    qk = [_dot_nt(qbs[h], kbs[h]) for h in heads]
    qc_prev = [_dot_nt(qbs[h], c_prevs[h].astype(BF16)) for h in heads]
    qn = [jnp.sum(qs[h] * n_prevs[h], axis=-1, keepdims=True) for h in heads]

    row = lax.broadcasted_iota(jnp.int32, (L, L), 0)
    col = lax.broadcasted_iota(jnp.int32, (L, L), 1)
    causal = col <= row
    pre = g_ref[0] + bg_ref[...]
    lane = lax.broadcasted_iota(jnp.int32, pre.shape, 1)
    is_f = (lane >= N_M_HEADS) & (lane < 2 * N_M_HEADS)
    cum = _dot(jnp.where(causal, 1.0, 0.0), jnp.where(is_f, _log_sigmoid(pre), 0.0), precision=HIGHEST)
    t8 = jnp.where(lane < N_M_HEADS, pre, cum)
    r8 = t8.T[0:2 * N_M_HEADS, :]
    i_rows = [r8[h:h + 1, :] for h in heads]
    b_rows = [r8[N_M_HEADS + h:N_M_HEADS + h + 1, :] for h in heads]
    i_cols = [t8[:, h:h + 1] for h in heads]
    b_cols = [t8[:, N_M_HEADS + h:N_M_HEADS + h + 1] for h in heads]

    intra = []
    for h in heads:
        d = jnp.where(causal, b_cols[h] - b_rows[h] + i_rows[h], NEG_INF)
        inter = b_cols[h] + m_prevs[h]
        m_t = jnp.maximum(inter, jnp.max(d, axis=-1, keepdims=True))
        w_inter = jnp.exp(inter - m_t)
        s = qk[h] * jnp.exp(d - m_t)
        intra.append((m_t, w_inter, s, jnp.sum(s, axis=-1, keepdims=True)))
    sv = [_dot(intra[h][2].astype(BF16), vs[h].astype(BF16)) for h in heads]

    carry = []
    for h in heads:
        b_last = b_rows[h][:, L - 1:L]
        g_row = b_last - b_rows[h] + i_rows[h]
        g_col = b_last - b_cols[h] + i_cols[h]
        m_new = jnp.maximum(b_last + m_prevs[h], jnp.max(g_row, axis=-1, keepdims=True))
        wc = jnp.exp(b_last + m_prevs[h] - m_new)
        ws = jnp.exp(g_col - m_new)
        carry.append((m_new, wc, ws, (ws * vs[h]).T.astype(BF16)))
    vk = [_dot(carry[h][3], kbs[h]) for h in heads]

    for h in heads:
        m_t, w_inter, _, s_sum = intra[h]
        num = w_inter * qc_prev[h] + sv[h]
        den = w_inter * qn[h] + s_sum
        hh = num / jnp.maximum(jnp.abs(den), jnp.exp(-m_t))
        mem_ref[0, :, hs[h]] = _sigmoid(o_ref[0, :, hs[h]]) * hh
    for h in heads:
        m_new, wc, ws, _ = carry[h]
        c_s[h] = wc * c_prevs[h] + vk[h]
        n_s[h:h + 1, :] = wc * n_prevs[h] + jnp.sum(ws * ks[h], axis=0, keepdims=True)
        m_s[h:h + 1, :] = jnp.broadcast_to(m_new, (1, LANES))

    @pl.when(ci == pl.num_programs(1) - 1)
    def _():
        c_out[0] = c_s[...]
        n_out[0] = n_s[0:N_M_HEADS, :]
        m_out[0] = m_s[...]


def _mlstm_prompt(mqk, mv, mo, gates, conv_w, conv_b, bg_row):
    b, s, _ = mv.shape
    L = M_CHUNK
    nc = s // L
    assert s % L == 0
    return pl.pallas_call(
        _mlstm_prompt_kernel,
        grid=(b, nc),
        in_specs=[pl.BlockSpec((1, L, M_WIDTH), lambda i, c: (i, c, 0)),
                  pl.BlockSpec((1, L, M_WIDTH), lambda i, c: (i, c, 1)),
                  pl.BlockSpec((1, L, M_WIDTH), lambda i, c: (i, c, 0)),
                  pl.BlockSpec((1, L, M_WIDTH), lambda i, c: (i, c, 0)),
                  pl.BlockSpec((1, L, LANES), lambda i, c: (i, c, 0)),
                  pl.BlockSpec((CONV_WIDTH, 2 * M_WIDTH), lambda i, c: (0, 0)),
                  pl.BlockSpec((1, 2 * M_WIDTH), lambda i, c: (0, 0)),
                  pl.BlockSpec((1, LANES), lambda i, c: (0, 0))],
        out_specs=[pl.BlockSpec((1, L, M_WIDTH), lambda i, c: (i, c, 0)),
                   pl.BlockSpec((1, N_M_HEADS, M_HEAD_DIM, M_HEAD_DIM), lambda i, c: (i, 0, 0, 0)),
                   pl.BlockSpec((1, N_M_HEADS, M_HEAD_DIM), lambda i, c: (i, 0, 0)),
                   pl.BlockSpec((1, SUBLANES, LANES), lambda i, c: (i, 0, 0))],
        out_shape=[jax.ShapeDtypeStruct((b, s, M_WIDTH), F32),
                   jax.ShapeDtypeStruct((b, N_M_HEADS, M_HEAD_DIM, M_HEAD_DIM), F32),
                   jax.ShapeDtypeStruct((b, N_M_HEADS, M_HEAD_DIM), F32),
                   jax.ShapeDtypeStruct((b, SUBLANES, LANES), F32)],
        scratch_shapes=[pltpu.VMEM((_CONV_PAD + L, M_WIDTH), F32),
                        pltpu.VMEM((_CONV_PAD + L, M_WIDTH), F32),
                        pltpu.VMEM((N_M_HEADS, M_HEAD_DIM, M_HEAD_DIM), F32),
                        pltpu.VMEM((SUBLANES, M_HEAD_DIM), F32),
                        pltpu.VMEM((SUBLANES, LANES), F32)],
        compiler_params=_cparams(("parallel", "arbitrary")),
        name="mlstm_prompt",
    )(mqk, mqk, mv, mo, gates, conv_w, conv_b, bg_row)


def _outproj_kernel(a_ref, m_ref, x_ref, beta_ref, w_ref, g1_ref, sc_ref, sh_ref, lg_ref, lb_ref,
                    x1_ref, h2_ref, *, attn_feature_major):
    attn = a_ref[0].T if attn_feature_major else a_ref[...]
    mixed = jnp.concatenate([attn, m_ref[...]], axis=-1) * beta_ref[...]
    y = _dot(mixed.astype(BF16), w_ref[...])
    z = ALPHA * x_ref[...] + g1_ref[0] * y
    x1 = _standardize(z) * lg_ref[...] + lb_ref[...]
    x1_ref[...] = x1
    h2_ref[...] = (_standardize(x1) * sc_ref[0] + sh_ref[0]).astype(h2_ref.dtype)


def _outproj(attn, mem, x2d, beta, w_out_b, g1, sc2, sh2, ln_g, ln_b, tm, rows_per_mod):
    r, d = x2d.shape
    m = g1.shape[1]
    if m == 1:
        mod_map = lambda i: ((i * tm) // rows_per_mod, 0, 0)
    else:
        mod_map = lambda i: (i, 0, 0)
    vec = pl.BlockSpec((1, d), lambda i: (0, 0))
    mod = pl.BlockSpec((1, m, d), mod_map)
    feature_major = attn.ndim == 3
    if feature_major:
        nt = rows_per_mod // tm
        assert rows_per_mod % tm == 0 and attn.shape[2] == rows_per_mod
        attn_spec = pl.BlockSpec((1, attn.shape[1], tm), lambda i: (i // nt, 0, i % nt))
    else:
        attn_spec = pl.BlockSpec((tm, attn.shape[1]), lambda i: (i, 0))
    return pl.pallas_call(
        functools.partial(_outproj_kernel, attn_feature_major=feature_major),
        grid=(r // tm,),
        in_specs=[attn_spec,
                  pl.BlockSpec((tm, mem.shape[1]), lambda i: (i, 0)),
                  pl.BlockSpec((tm, d), lambda i: (i, 0)),
                  vec,
                  pl.BlockSpec(w_out_b.shape, lambda i: (0, 0)),
                  mod, mod, mod, vec, vec],
        out_specs=[pl.BlockSpec((tm, d), lambda i: (i, 0)),
                   pl.BlockSpec((tm, d), lambda i: (i, 0))],
        out_shape=[jax.ShapeDtypeStruct((r, d), F32), jax.ShapeDtypeStruct((r, d), BF16)],
        compiler_params=_cparams(("parallel",)),
        name="outproj",
    )(attn, mem, x2d, beta, w_out_b, g1, sc2, sh2, ln_g, ln_b)


def _oddeven_merge_sort_pairs(n):
    pairs = []

    def merge(lo, m, r):
        step = r * 2
        if step < m:
            merge(lo, m, step)
            merge(lo + r, m, step)
            for i in range(lo + r, lo + m - r, step):
                pairs.append((i, i + r))
        else:
            pairs.append((lo, lo + r))

    def sort(lo, m):
        if m > 1:
            h = m // 2
            sort(lo, h)
            sort(lo + h, h)
            merge(lo, m, 1)

    sort(0, n)
    return tuple(pairs)


_SORT16 = _oddeven_merge_sort_pairs(PEER_TOPK)


def _vmax(a, b):
    if a is None:
        return b
    if b is None:
        return a
    return jnp.maximum(a, b)


def _cmpx(v, i, j):
    a, b = v[i], v[j]
    if b is None:
        return
    if a is None:
        v[i], v[j] = b, None
        return
    v[i], v[j] = jnp.maximum(a, b), jnp.minimum(a, b)


def _bitonic_to_desc(v):
    n = len(v)
    d = n // 2
    while d >= 1:
        for i in range(n):
            if (i & d) == 0:
                _cmpx(v, i, i + d)
        d //= 2
    return v


def _merge_top(x, y):
    n = len(x)
    return _bitonic_to_desc([_vmax(x[i], y[n - 1 - i]) for i in range(n)])


def _top16_desc(sc):
    groups = sc.shape[0] // SUBLANES
    assert groups == PEER_TOPK
    v = [sc[g * SUBLANES:(g + 1) * SUBLANES, :] for g in range(groups)]
    for i, j in _SORT16:
        _cmpx(v, i, j)
    shift = SUBLANES // 2
    while shift >= 1:
        partner = [pltpu.roll(a, shift, 0) for a in v]
        v = _merge_top(v, partner)
        shift //= 2
    return v


def _candidate_lists(a, b):
    k = PEER_TOPK
    lists = []
    for i in range(4):
        n = k // (i + 1)
        lists.append([a[i] + b[j] for j in range(n)])
    for j in range(3):
        n = k // (j + 1)
        col = [a[i] + b[j] for i in range(4, n)]
        if col:
            lists.append(col)
    return [l + [None] * (k - len(l)) for l in lists]


_RANK_STEP = 2.0


def _prefix_count(pred, vals):
    assert len(vals) == PEER_TOPK == 16
    sel = jnp.where
    c8 = pred(vals[7])
    c4 = pred(sel(c8, vals[11], vals[3]))
    c2 = pred(sel(c8, sel(c4, vals[13], vals[9]), sel(c4, vals[5], vals[1])))
    c1 = pred(sel(c8, sel(c4, sel(c2, vals[14], vals[12]), sel(c2, vals[10], vals[8])),
                  sel(c4, sel(c2, vals[6], vals[4]), sel(c2, vals[2], vals[0]))))
    n = (sel(c8, 8 * _RANK_STEP, 0.0) + sel(c4, 4 * _RANK_STEP, 0.0)
         + sel(c2, 2 * _RANK_STEP, 0.0) + sel(c1, _RANK_STEP, 0.0))
    return sel(pred(vals[15]), 16 * _RANK_STEP, n)


def _route_weights_kernel(sk_ref, wq_ref, o_ref):
    s = pl.program_id(0) % 2
    o_ref[...] = _dot_nt(sk_ref[s], wq_ref[...], precision=HIGHEST).astype(o_ref.dtype)


def _route_weights(sub_keys, w_query):
    d, width = w_query.shape
    kd = PEER_KEY_DIM // 2
    assert width == PEER_HEADS * 2 * kd and sub_keys.shape == (2, N_KEYS, kd)
    return pl.pallas_call(
        _route_weights_kernel,
        grid=(width // kd,),
        in_specs=[pl.BlockSpec(sub_keys.shape, lambda j: (0, 0, 0)),
                  pl.BlockSpec((d, kd), lambda j: (0, j))],
        out_specs=pl.BlockSpec((N_KEYS, d), lambda j: (j, 0)),
        out_shape=jax.ShapeDtypeStruct((PEER_HEADS * 2 * N_KEYS, d), BF16),
        compiler_params=_cparams(("parallel",)),
        name="peer_route_weights",
    )(sub_keys, w_query)


def _peer_route_kernel(h_ref, wr_ref, cnt_ref, rk_ref, a_ref, b_ref, sc_s, top_s, tz_s):
    tm = h_ref.shape[0]
    sc_s[...] = _dot_nt(wr_ref[...], h_ref[...])

    def scores(p, s):
        return sc_s[pl.ds(pl.multiple_of((2 * p + s) * N_KEYS, N_KEYS), N_KEYS), :]

    def head(p, carry):
        for s in range(2):
            srt = _top16_desc(scores(p, s))
            for r in range(PEER_TOPK):
                top_s[p, s, r] = srt[r]
        return carry

    lax.fori_loop(0, PEER_HEADS, head, 0)
    sub = lax.broadcasted_iota(jnp.int32, (SUBLANES, tm), 0)

    def on_sublanes(s, r):
        out = top_s[0, s, r]
        for p in range(1, PEER_HEADS):
            out = jnp.where(sub == p, top_s[p, s, r], out)
        return out

    top = [[on_sublanes(s, r) for r in range(PEER_TOPK)] for s in range(2)]
    lists = _candidate_lists(top[0], top[1])
    best = lists[0]
    for other in lists[1:]:
        best = _merge_top(best, other)
    z = jnp.ones_like(best[0])
    for r in range(1, PEER_TOPK):
        z = z + jnp.exp(best[r] - best[0])
    thr = best[PEER_TOPK - 1]
    for p in range(PEER_HEADS):
        tz_s[p, 0] = jnp.broadcast_to(thr[p:p + 1, :], (SUBLANES, tm))
        tz_s[p, 1] = jnp.broadcast_to(z[p:p + 1, :], (SUBLANES, tm))

    def emit(p, carry):
        s0 = scores(p, 0)
        s1 = scores(p, 1)
        t_row = tz_s[p, 0][0:1, :]
        z_row = tz_s[p, 1][0:1, :]
        b_top = [top_s[p, 1, r][0:1, :] for r in range(PEER_TOPK)]
        cnt_ref[p] = _prefix_count(lambda v: s0 + v >= t_row, b_top)
        rk_ref[p] = _prefix_count(lambda v: v > s1, b_top).astype(rk_ref.dtype)
        a_ref[p] = jnp.exp(s0 - top_s[p, 0, 0][0:1, :]) / z_row
        b_ref[p] = jnp.exp(s1 - top_s[p, 1, 0][0:1, :]).astype(b_ref.dtype)
        return carry

    lax.fori_loop(0, PEER_HEADS, emit, 0)


def _peer_route(h2, w_route, tm):
    r, d = h2.shape
    assert PEER_HEADS == SUBLANES and w_route.shape == (PEER_HEADS * 2 * N_KEYS, d)
    shp = (PEER_HEADS, N_KEYS, r)
    bspec = pl.BlockSpec((PEER_HEADS, N_KEYS, tm), lambda i: (0, 0, i))
    return pl.pallas_call(
        _peer_route_kernel,
        grid=(r // tm,),
        in_specs=[pl.BlockSpec((tm, d), lambda i: (i, 0)),
                  pl.BlockSpec(w_route.shape, lambda i: (0, 0))],
        out_specs=[bspec, bspec, bspec, bspec],
        out_shape=[jax.ShapeDtypeStruct(shp, F32), jax.ShapeDtypeStruct(shp, BF16),
                   jax.ShapeDtypeStruct(shp, F32), jax.ShapeDtypeStruct(shp, BF16)],
        scratch_shapes=[pltpu.VMEM((w_route.shape[0], tm), F32),
                        pltpu.VMEM((PEER_HEADS, 2, PEER_TOPK, SUBLANES, tm), F32),
                        pltpu.VMEM((PEER_HEADS, 2, SUBLANES, tm), F32)],
        compiler_params=_cparams(("parallel",)),
        name="peer_route",
    )(h2, w_route)


_EXPERT_CHUNK = SUBLANES * N_KEYS
_MIX_SUBTILE = 2 * SUBLANES


def _peer_mix_kernel(h_ref, u_ref, vt_ref, cnt_ref, rk_ref, a_ref, b_ref, o_ref, act_s, y_s, acc_s):
    c = pl.program_id(1)
    tm = h_ref.shape[0]

    @pl.when(c == 0)
    def _():
        acc_s[...] = jnp.zeros(acc_s.shape, F32)

    act_s[...] = _dot_nt(u_ref[...], h_ref[...])

    sub = _MIX_SUBTILE
    zero = jnp.zeros((sub, LANES), BF16)
    for ii in range(SUBLANES):
        for lc in range(tm // LANES):
            ls = slice(lc * LANES, (lc + 1) * LANES)
            cb = [jnp.broadcast_to(cnt_ref[p, ii:ii + 1, ls], (sub, LANES)).astype(BF16) for p in range(PEER_HEADS)]
            ab = [jnp.broadcast_to(a_ref[p, ii:ii + 1, ls], (sub, LANES)).astype(BF16) for p in range(PEER_HEADS)]
            for js in range(N_KEYS // sub):
                jr = slice(js * sub, (js + 1) * sub)
                terms = [jnp.maximum(jnp.minimum(ab[p] * b_ref[p, jr, ls], cb[p] - rk_ref[p, jr, ls]), zero)
                         for p in range(PEER_HEADS)]
                while len(terms) > 1:
                    terms = [terms[i] + terms[i + 1] for i in range(0, len(terms), 2)]
                w = terms[0]
                rs = slice(ii * N_KEYS + js * sub, ii * N_KEYS + (js + 1) * sub)
                y_s[rs, ls] = w * _gelu_tanh(act_s[rs, ls].astype(BF16))
    acc_s[...] += _dot(vt_ref[0], y_s[...])

    @pl.when(c == pl.num_programs(1) - 1)
    def _():
        o_ref[...] = acc_s[...].T


def _peer_mix(h2, u_b, vt_c, cnt, rk, a, b, tm):
    r, d = h2.shape
    n_exp = u_b.shape[0]
    ch = _EXPERT_CHUNK
    assert n_exp == N_KEYS * N_KEYS and vt_c.shape == (n_exp // ch, d, ch)
    row_blk = pl.BlockSpec((PEER_HEADS, SUBLANES, tm), lambda i, c: (0, c, i))
    all_blk = pl.BlockSpec((PEER_HEADS, N_KEYS, tm), lambda i, c: (0, 0, i))
    return pl.pallas_call(
        _peer_mix_kernel,
        grid=(r // tm, n_exp // ch),
        in_specs=[pl.BlockSpec((tm, d), lambda i, c: (i, 0)),
                  pl.BlockSpec((ch, d), lambda i, c: (c, 0)),
                  pl.BlockSpec((1, d, ch), lambda i, c: (c, 0, 0)),
                  row_blk, all_blk, row_blk, all_blk],
        out_specs=pl.BlockSpec((tm, d), lambda i, c: (i, 0)),
        out_shape=jax.ShapeDtypeStruct((r, d), F32),
        scratch_shapes=[pltpu.VMEM((ch, tm), F32), pltpu.VMEM((ch, tm), BF16), pltpu.VMEM((d, tm), F32)],
        compiler_params=_cparams(("parallel", "arbitrary")),
        name="peer_mix",
    )(h2, u_b, vt_c, cnt, rk, a, b)


def _peer(h2, w_route, u_b, vt_b, tm_route, tm_mix):
    cnt, rk, a, b = _peer_route(h2, w_route, tm_route)
    return _peer_mix(h2, u_b, vt_b, cnt, rk, a, b, tm_mix)


def _final_kernel(x_ref, f_ref, g2_ref, lg_ref, lb_ref, o_ref):
    z = ALPHA * x_ref[...] + g2_ref[0] * f_ref[...]
    o_ref[...] = _standardize(z) * lg_ref[...] + lb_ref[...]


def _final(x1, f, g2, ln_g, ln_b, tm, rows_per_mod):
    r, d = x1.shape
    m = g2.shape[1]
    if m == 1:
        mod_map = lambda i: ((i * tm) // rows_per_mod, 0, 0)
    else:
        mod_map = lambda i: (i, 0, 0)
    vec = pl.BlockSpec((1, d), lambda i: (0, 0))
    return pl.pallas_call(
        _final_kernel,
        grid=(r // tm,),
        in_specs=[pl.BlockSpec((tm, d), lambda i: (i, 0)),
                  pl.BlockSpec((tm, d), lambda i: (i, 0)),
                  pl.BlockSpec((1, m, d), mod_map), vec, vec],
        out_specs=pl.BlockSpec((tm, d), lambda i: (i, 0)),
        out_shape=jax.ShapeDtypeStruct((r, d), F32),
        compiler_params=_cparams(("parallel",)),
        name="final_norm",
    )(x1, f, g2, ln_g, ln_b)


_PAGES_PER_STEP = 64


def _page_copies(pt_ref, k_hbm, buf, sem, step, half):
    pp = buf.shape[1]
    per_tok = pt_ref.shape[1]
    tok = (step * pp) // per_tok
    first = (step * pp) % per_tok
    return [pltpu.make_async_copy(k_hbm.at[pt_ref[tok, first + k]], buf.at[half, k], sem.at[half])
            for k in range(pp)]


def _page_sum_kernel(pt_ref, k_hbm, o_ref, buf, sem):
    i = pl.program_id(0)
    n = pl.num_programs(0)
    half = i % 2
    _, pp, n_h, dh, page = buf.shape
    width = n_h * dh
    copies = functools.partial(_page_copies, pt_ref, k_hbm, buf, sem)

    @pl.when(i == 0)
    def _():
        for c in copies(0, 0):
            c.start()

    @pl.when(i + 1 < n)
    def _():
        for c in copies(i + 1, 1 - half):
            c.start()

    for c in copies(i, half):
        c.wait()

    lane = lax.broadcasted_iota(jnp.int32, (width, LANES), 1)
    t = jnp.zeros((width, LANES), F32)
    for pg in range(pp):
        col = jnp.sum(buf[half, pg].reshape(width, page), axis=-1, keepdims=True)
        t = jnp.where(lane == pg, col, t)
    o_ref[...] = t.T[0:pp, :]


def _page_sums(page_table, cache_t):
    n_phys, n_h, dh, page = cache_t.shape
    db, per_tok = page_table.shape
    pp = _PAGES_PER_STEP
    assert per_tok % pp == 0 and pp <= LANES
    grid_spec = pltpu.PrefetchScalarGridSpec(
        num_scalar_prefetch=1,
        grid=(db * per_tok // pp,),
        in_specs=[pl.BlockSpec(memory_space=pl.ANY)],
        out_specs=pl.BlockSpec((pp, n_h * dh), lambda i, pt: (i, 0)),
        scratch_shapes=[pltpu.VMEM((2, pp, n_h, dh, page), F32), pltpu.SemaphoreType.DMA((2,))],
    )
    return pl.pallas_call(
        _page_sum_kernel,
        grid_spec=grid_spec,
        out_shape=jax.ShapeDtypeStruct((db * per_tok, n_h * dh), F32),
        compiler_params=_cparams(("arbitrary",)),
        name="page_sums",
    )(page_table, cache_t)


def _block_gate_kernel(ps_ref, q_ref, sel_ref, km_s, *, n_blocks):
    ppb = MOBA_BLOCK // PAGE_SIZE
    width = ps_ref.shape[1]

    km_s[...] = jnp.zeros(km_s.shape, F32)

    def gather(n, carry):
        acc = jnp.zeros((1, width), F32)
        for j in range(ppb):
            acc = acc + ps_ref[pl.ds(n * ppb + j, 1), :]
        km_s[pl.ds(n, 1), :] = acc * (1.0 / MOBA_BLOCK)
        return carry

    lax.fori_loop(0, n_blocks, gather, 0)
    q = q_ref[0]
    sub = lax.broadcasted_iota(jnp.int32, (N_ATTN_HEADS, width), 0)
    lane_w = lax.broadcasted_iota(jnp.int32, (N_ATTN_HEADS, width), 1)
    qb = jnp.where(lane_w // ATTN_HEAD_DIM == sub, jnp.broadcast_to(q, (N_ATTN_HEADS, width)), 0.0)
    gate = _dot_nt(qb, km_s[...], precision=HIGHEST)
    lane = lax.broadcasted_iota(jnp.int32, gate.shape, 1)
    g = jnp.where(lane < n_blocks, gate, NEG_INF)
    out = jnp.zeros(gate.shape, jnp.int32)
    for k in range(MOBA_TOPK):
        mx = jnp.max(g, axis=-1, keepdims=True)
        idx = jnp.min(jnp.where(g == mx, lane, LANES), axis=-1, keepdims=True)
        out = jnp.where(lane == k, idx, out)
        g = jnp.where(lane == idx, NEG_INF, g)
    sel_ref[0] = out


def _block_gate(page_sums, q3, n_blocks):
    db = q3.shape[0]
    per_tok = page_sums.shape[0] // db
    assert MOBA_TOPK <= n_blocks <= LANES and per_tok * PAGE_SIZE == n_blocks * MOBA_BLOCK
    return pl.pallas_call(
        functools.partial(_block_gate_kernel, n_blocks=n_blocks),
        grid=(db,),
        in_specs=[pl.BlockSpec((per_tok, page_sums.shape[1]), lambda i: (i, 0)),
                  pl.BlockSpec((1, 1, q3.shape[2]), lambda i: (i, 0, 0))],
        out_specs=pl.BlockSpec((1, N_ATTN_HEADS, LANES), lambda i: (i, 0, 0)),
        scratch_shapes=[pltpu.VMEM((LANES, page_sums.shape[1]), F32)],
        out_shape=jax.ShapeDtypeStruct((db, N_ATTN_HEADS, LANES), jnp.int32),
        compiler_params=_cparams(("parallel",)),
        name="block_gate",
    )(page_sums, q3)


_PAGES_PER_BLOCK = MOBA_BLOCK // PAGE_SIZE
_SEL_PAGES = MOBA_TOPK * _PAGES_PER_BLOCK


def _sample_page_copies(pt_ref, sel_ref, k_hbm, v_hbm, kbuf, vbuf, sem, bb, par):
    out = []
    for h in range(N_ATTN_HEADS):
        for kt in range(MOBA_TOPK):
            blk = sel_ref[bb, h * MOBA_TOPK + kt]
            for pp in range(_PAGES_PER_BLOCK):
                page = pt_ref[bb, blk * _PAGES_PER_BLOCK + pp]
                slot = h * _SEL_PAGES + kt * _PAGES_PER_BLOCK + pp
                out.append(pltpu.make_async_copy(k_hbm.at[page, h], kbuf.at[par, slot], sem.at[0, par]))
                out.append(pltpu.make_async_copy(v_hbm.at[page, h], vbuf.at[par, slot], sem.at[1, par]))
    return out


def _col_from_row(row):
    n = row.shape[1]
    r = lax.broadcasted_iota(jnp.int32, (n, n), 0)
    c = lax.broadcasted_iota(jnp.int32, (n, n), 1)
    return jnp.sum(jnp.where(r == c, jnp.broadcast_to(row, (n, n)), 0.0), axis=-1, keepdims=True)


def _row_from_col(col):
    n = col.shape[0]
    r = lax.broadcasted_iota(jnp.int32, (n, n), 0)
    c = lax.broadcasted_iota(jnp.int32, (n, n), 1)
    return jnp.sum(jnp.where(r == c, jnp.broadcast_to(col, (n, n)), 0.0), axis=0, keepdims=True)


def _moba_sample_kernel(pt_ref, sel_ref, tbl_ref, q_ref, kn_ref, vn_ref, k_hbm, v_hbm, o_ref,
                        kbuf, vbuf, bias_s, sem, *, past_len):
    b = pl.program_id(0)
    nb = pl.num_programs(0)
    par = b % 2
    scale = ATTN_HEAD_DIM ** -0.5
    copies = functools.partial(_sample_page_copies, pt_ref, sel_ref, k_hbm, v_hbm, kbuf, vbuf, sem)

    @pl.when(b == 0)
    def _():
        for c in copies(0, 0):
            c.start()

    @pl.when(b + 1 < nb)
    def _():
        for c in copies(b + 1, 1 - par):
            c.start()

    for c in copies(b, par):
        c.wait()

    sub = lax.broadcasted_iota(jnp.int32, (_PAGES_PER_BLOCK, PAGE_SIZE), 0)
    lane = lax.broadcasted_iota(jnp.int32, (_PAGES_PER_BLOCK, PAGE_SIZE), 1)
    for h in range(N_ATTN_HEADS):
        q = q_ref[0, h:h + 1, :]
        q_col = _col_from_row(q)
        for kt in range(MOBA_TOPK):
            pos0 = sel_ref[b, h * MOBA_TOPK + kt] * MOBA_BLOCK
            near = past_len - pos0 - (MOBA_BLOCK - 1) < MAX_DISTANCE
            rows = slice(kt * _PAGES_PER_BLOCK, (kt + 1) * _PAGES_PER_BLOCK)

            @pl.when(near)
            def _():
                dist = jnp.maximum(past_len - (pos0 + sub * PAGE_SIZE + lane), 0)
                bias_s[rows, :] = _bias_from_bucket(_t5_bucket(dist), tbl_ref, h)

            @pl.when(jnp.logical_not(near))
            def _():
                bias_s[rows, :] = jnp.full((_PAGES_PER_BLOCK, PAGE_SIZE), tbl_ref[N_BUCKETS - 1, h], F32)

        s = jnp.concatenate([jnp.sum(kbuf[par, h * _SEL_PAGES + j] * q_col, axis=0, keepdims=True)
                             for j in range(_SEL_PAGES)], axis=0) * scale + bias_s[0:_SEL_PAGES, :]
        s_new = jnp.sum(kn_ref[0, h:h + 1, :] * q, axis=-1, keepdims=True) * scale + tbl_ref[0, h]
        m = jnp.maximum(jnp.max(jnp.max(s, axis=-1, keepdims=True), axis=0, keepdims=True), s_new)
        p = jnp.exp(s - m)
        p_new = jnp.exp(s_new - m)
        den = jnp.sum(jnp.sum(p, axis=-1, keepdims=True), axis=0, keepdims=True) + p_new
        pv = vbuf[par, h * _SEL_PAGES] * p[0:1, :]
        for j in range(1, _SEL_PAGES):
            pv = pv + vbuf[par, h * _SEL_PAGES + j] * p[j:j + 1, :]
        num = _row_from_col(jnp.sum(pv, axis=-1, keepdims=True)) + p_new * vn_ref[0, h:h + 1, :]
        o_ref[0, h:h + 1, :] = num / den


def _moba_sample(page_table, sel, table, q3, k3, v3, cache_kt, cache_vt, past_len):
    db, n_h, dh = q3.shape
    assert cache_kt.shape[1:] == (n_h, dh, PAGE_SIZE) and PAGE_SIZE == LANES and _SEL_PAGES <= SUBLANES
    vec = pl.BlockSpec((1, n_h, dh), lambda i, pt, sl: (i, 0, 0))
    grid_spec = pltpu.PrefetchScalarGridSpec(
        num_scalar_prefetch=2,
        grid=(db,),
        in_specs=[pl.BlockSpec(memory_space=pltpu.SMEM), vec, vec, vec,
                  pl.BlockSpec(memory_space=pl.ANY), pl.BlockSpec(memory_space=pl.ANY)],
        out_specs=vec,
        scratch_shapes=[pltpu.VMEM((2, n_h * _SEL_PAGES, dh, PAGE_SIZE), F32),
                        pltpu.VMEM((2, n_h * _SEL_PAGES, dh, PAGE_SIZE), F32),
                        pltpu.VMEM((SUBLANES, PAGE_SIZE), F32),
                        pltpu.SemaphoreType.DMA((2, 2))],
    )
    return pl.pallas_call(
        functools.partial(_moba_sample_kernel, past_len=past_len),
        grid_spec=grid_spec,
        out_shape=jax.ShapeDtypeStruct((db, n_h, dh), F32),
        compiler_params=_cparams(("arbitrary",)),
        name="moba_sample",
    )(page_table, sel, table, q3, k3, v3, cache_kt, cache_vt)


def _mlstm_step_kernel(qk_ref, cs_ref, v_ref, o_ref, g_ref, cw_ref, cb_ref, bg_ref,
                       c_ref, n_ref, m_ref, mem_ref, c_out, n_out, m_out):
    dh = M_HEAD_DIM
    hist = CONV_WIDTH - 1
    y = cb_ref[...] + qk_ref[0] * cw_ref[hist:hist + 1, :]
    for j in range(hist):
        y = y + cs_ref[0, j:j + 1, :] * cw_ref[j:j + 1, :]
    y = y * _sigmoid(y)
    pre = g_ref[0] + bg_ref[...]
    row = lax.broadcasted_iota(jnp.int32, (dh, dh), 0)
    col = lax.broadcasted_iota(jnp.int32, (dh, dh), 1)
    lane = lax.broadcasted_iota(jnp.int32, (1, LANES), 1)
    m_all = jnp.zeros((1, LANES), F32)
    for h in range(N_M_HEADS):
        sl = slice(h * dh, (h + 1) * dh)
        q = y[:, sl]
        k = y[:, M_WIDTH + h * dh:M_WIDTH + (h + 1) * dh] * (dh ** -0.5)
        v = v_ref[0, :, sl]
        i_t = pre[:, h:h + 1]
        logf = _log_sigmoid(pre[:, N_M_HEADS + h:N_M_HEADS + h + 1])
        c_prev = c_ref[0, h]
        n_prev = n_ref[0, h:h + 1, :]
        m_prev = m_ref[0, :, h:h + 1]
        inter = logf + m_prev
        m_t = jnp.maximum(inter, i_t)
        w_inter = jnp.exp(inter - m_t)
        s = jnp.sum(q * k, axis=-1, keepdims=True) * jnp.exp(i_t - m_t)
        cq = _dot_nt(jnp.broadcast_to(q, (SUBLANES, dh)), c_prev, precision=HIGHEST)[0:1, :]
        num = w_inter * cq + s * v
        den = w_inter * jnp.sum(n_prev * q, axis=-1, keepdims=True) + s
        hh = num / jnp.maximum(jnp.abs(den), jnp.exp(-m_t))
        mem_ref[0, :, sl] = _sigmoid(o_ref[0, :, sl]) * hh
        wc = jnp.exp(inter - m_t)
        ws = jnp.exp(i_t - m_t)
        v_col = jnp.sum(jnp.where(row == col, jnp.broadcast_to(v, (dh, dh)), 0.0), axis=-1, keepdims=True)
        c_out[0, h] = wc * c_prev + (ws * v_col) * k
        n_out[0, h:h + 1, :] = wc * n_prev + ws * k
        m_all = jnp.where(lane == h, m_t, m_all)
    m_out[0] = m_all


def _mlstm_step(mqk, cstate, mv, mo, gates, conv_w, conv_b, bg_row, c0, n0, m0):
    db = mqk.shape[0]
    r3 = lambda w: pl.BlockSpec((1, 1, w), lambda i: (i, 0, 0))
    return pl.pallas_call(
        _mlstm_step_kernel,
        grid=(db,),
        in_specs=[r3(2 * M_WIDTH),
                  pl.BlockSpec((1, CONV_WIDTH - 1, 2 * M_WIDTH), lambda i: (i, 0, 0)),
                  r3(M_WIDTH), r3(M_WIDTH), r3(LANES),
                  pl.BlockSpec((CONV_WIDTH, 2 * M_WIDTH), lambda i: (0, 0)),
                  pl.BlockSpec((1, 2 * M_WIDTH), lambda i: (0, 0)),
                  pl.BlockSpec((1, LANES), lambda i: (0, 0)),
                  pl.BlockSpec((1, N_M_HEADS, M_HEAD_DIM, M_HEAD_DIM), lambda i: (i, 0, 0, 0)),
                  pl.BlockSpec((1, N_M_HEADS, M_HEAD_DIM), lambda i: (i, 0, 0)),
                  pl.BlockSpec((1, 1, N_M_HEADS), lambda i: (i, 0, 0))],
        out_specs=[r3(M_WIDTH),
                   pl.BlockSpec((1, N_M_HEADS, M_HEAD_DIM, M_HEAD_DIM), lambda i: (i, 0, 0, 0)),
                   pl.BlockSpec((1, N_M_HEADS, M_HEAD_DIM), lambda i: (i, 0, 0)),
                   r3(LANES)],
        out_shape=[jax.ShapeDtypeStruct((db, 1, M_WIDTH), F32),
                   jax.ShapeDtypeStruct((db, N_M_HEADS, M_HEAD_DIM, M_HEAD_DIM), F32),
                   jax.ShapeDtypeStruct((db, N_M_HEADS, M_HEAD_DIM), F32),
                   jax.ShapeDtypeStruct((db, 1, LANES), F32)],
        compiler_params=_cparams(("parallel",)),
        name="mlstm_step",
    )(mqk, cstate, mv, mo, gates, conv_w, conv_b, bg_row, c0, n0, m0)


def _pad_rows(x, mult):
    r = x.shape[0]
    rp = -(-r // mult) * mult
    return x if rp == r else jnp.pad(x, ((0, rp - r), (0, 0)))


def kernel(x_prompt, x_sample, cache_k, cache_v, page_table, state_C, state_n, state_m, state_conv,
           c_prompt, c_sample, rel_bias_table, w_ada, b_ada, w_in, b_gate, conv_w, conv_b,
           beta_attn, beta_mlstm, w_out, ln1_g, ln1_b, w_query, sub_keys, expert_u, expert_v,
           ln2_g, ln2_b):
    assert w_ada.shape[0] == DEPTH == 1
    B, S, D = x_prompt.shape
    DB, T, _ = x_sample.shape
    assert T == 1
    H, dh = N_ATTN_HEADS, ATTN_HEAD_DIM
    past_len = page_table.shape[1] * PAGE_SIZE
    assert past_len % MOBA_BLOCK == 0
    l = 0

    gate_cols = 2 * N_M_HEADS
    w_in_p = jnp.pad(w_in[l], ((0, 0), (0, LANES - gate_cols))).astype(BF16)
    w_out_b = w_out[l].astype(BF16)
    w_route = _route_weights(sub_keys[l], w_query[l])
    u_b = expert_u[l].astype(BF16)
    n_exp = expert_v.shape[1]
    vt_b = jnp.transpose(expert_v[l].reshape(n_exp // _EXPERT_CHUNK, _EXPERT_CHUNK, D), (0, 2, 1)).astype(BF16)
    beta = jnp.concatenate([beta_attn[l], beta_mlstm[l]])[None, :]
    bg = b_gate[l]
    bg_row = jnp.pad(bg, (0, LANES - gate_cols))[None, :]
    cw, cb = conv_w[l], conv_b[l][None, :]
    table = rel_bias_table
    lg1, lb1, lg2, lb2 = ln1_g[l][None, :], ln1_b[l][None, :], ln2_g[l][None, :], ln2_b[l][None, :]

    mod = _ada(jnp.concatenate([c_prompt, c_sample], axis=0), w_ada[l], b_ada[l])
    sh1, sc1, g1, sh2, sc2, g2 = [mod[:, i * D:(i + 1) * D] for i in range(6)]
    sc1, sc2 = 1.0 + sc1, 1.0 + sc2
    pm = lambda t: t[:B][:, None, :]
    sm = lambda t: t[B:][None, :, :]

    xp2 = x_prompt.reshape(B * S, D)
    w_in_tb = w_in[l].T[:3 * ATTN_WIDTH].astype(BF16)
    aqt, ak, akt, avt, avb, mqk, mv, mo, gates = _inproj_prompt(x_prompt, pm(sc1), pm(sh1), w_in_p, w_in_tb)
    attn = _moba_prompt(aqt, ak.reshape(B, S, ATTN_WIDTH), avb, table, _bias_tiles(table))
    mqk3 = mqk.reshape(B, S, 2 * M_WIDTH)
    mem, c_p, n_p, m_p = _mlstm_prompt(mqk3, mv.reshape(B, S, M_WIDTH), mo.reshape(B, S, M_WIDTH),
                                       gates.reshape(B, S, LANES), cw, cb, bg_row)
    x1, h2 = _outproj(attn, mem.reshape(B * S, M_WIDTH), xp2, beta, w_out_b,
                      pm(g1), pm(sc2), pm(sh2), lg1, lb1, 256, S)
    f = _peer(h2, w_route, u_b, vt_b, 256, 512)
    y_prompt = _final(x1, f, pm(g2), lg2, lb2, 512, S).reshape(B, S, D)
    from_t = lambda t: jnp.transpose(t.reshape(B, H, dh, S), (0, 3, 1, 2))[None]
    k_prompt = from_t(akt)
    v_prompt = from_t(avt)
    conv_prompt = mqk3[:, S - (CONV_WIDTH - 1):, :][None]

    xs2 = x_sample.reshape(DB, D)
    saq, sak, sav, smqk, smv, smo, sgates = _inproj(xs2, sm(sc1), sm(sh1), w_in_p, DB, 1)
    cache_kt = jnp.transpose(cache_k[l], (0, 2, 3, 1))
    cache_vt = jnp.transpose(cache_v[l], (0, 2, 3, 1))
    psum = _page_sums(page_table, cache_kt)
    n_blocks = past_len // MOBA_BLOCK
    sel = _block_gate(psum, saq.reshape(DB, 1, H * dh), n_blocks)
    sel = sel[:, :, :MOBA_TOPK].reshape(DB, H * MOBA_TOPK)
    h3 = lambda t: t.reshape(DB, H, dh)
    s_attn = _moba_sample(page_table, sel, table, h3(saq), h3(sak), h3(sav), cache_kt, cache_vt, past_len)
    s_mem, c_s, n_s, m_s = _mlstm_step(
        smqk.reshape(DB, 1, 2 * M_WIDTH), state_conv[l], smv.reshape(DB, 1, M_WIDTH),
        smo.reshape(DB, 1, M_WIDTH), sgates.reshape(DB, 1, LANES), cw, cb, bg_row,
        state_C[l], state_n[l], state_m[l].reshape(DB, 1, N_M_HEADS))
    sx1, sh2_ = _outproj(s_attn.reshape(DB, H * dh), s_mem.reshape(DB, M_WIDTH), xs2, beta, w_out_b,
                         sm(g1), sm(sc2), sm(sh2), lg1, lb1, DB, 1)
    sf = _peer(_pad_rows(sh2_, LANES), w_route, u_b, vt_b, LANES, LANES)[:DB]
    y_sample = _final(sx1, sf, sm(g2), lg2, lb2, DB, 1).reshape(DB, 1, D)
    conv_sample = jnp.concatenate([state_conv[l][:, 1:, :], smqk.reshape(DB, 1, 2 * M_WIDTH)], axis=1)[None]

    return (y_prompt, y_sample,
            k_prompt, v_prompt, c_p[None], n_p[None], m_p[:, :N_M_HEADS, 0][None], conv_prompt,
            sak.reshape(1, DB, 1, H, dh), sav.reshape(1, DB, 1, H, dh),
            c_s[None], n_s[None], m_s[:, 0, :N_M_HEADS][None], conv_sample)
```

```python
import functools
import math

import numpy as np
import jax
import jax.numpy as jnp
from jax import lax
from jax.experimental import pallas as pl
from jax.experimental.pallas import tpu as pltpu

F32 = jnp.float32
BF16 = jnp.bfloat16
NEG_INF = float("-inf")
HIGHEST = lax.Precision.HIGHEST

N_ATTN_HEADS = 8
ATTN_HEAD_DIM = 64
ATTN_WIDTH = N_ATTN_HEADS * ATTN_HEAD_DIM
MOBA_BLOCK = 256
MOBA_TOPK = 3
PAGE_SIZE = 128
N_BUCKETS = 32
MAX_DISTANCE = 128
N_M_HEADS = 4
M_HEAD_DIM = 128
M_WIDTH = N_M_HEADS * M_HEAD_DIM
CONV_WIDTH = 4
M_CHUNK = 128
N_KEYS = 128
PEER_HEADS = 8
PEER_KEY_DIM = 256
PEER_TOPK = 16
LN_EPS = 1e-5
DEPTH = 1
ALPHA = (2.0 * DEPTH) ** 0.25

LANES = 128
SUBLANES = 8
VMEM_LIMIT = 56 * 1024 * 1024


def _cparams(sem, flags=None):
    return pltpu.CompilerParams(dimension_semantics=sem, vmem_limit_bytes=VMEM_LIMIT, flags=flags)


def _bucket_thresholds():
    max_exact = N_BUCKETS // 2
    d = np.arange(0, MAX_DISTANCE + 1)
    far = max_exact + (np.log(np.maximum(d, 1) / max_exact) / math.log(MAX_DISTANCE / max_exact)
                       * (N_BUCKETS - max_exact)).astype(np.int64)
    bucket = np.where(d < max_exact, d, np.minimum(far, N_BUCKETS - 1))
    assert np.all(np.diff(bucket) >= 0) and bucket[-1] == N_BUCKETS - 1
    return tuple(int(np.argmax(bucket >= k)) for k in range(max_exact + 1, N_BUCKETS))


_BUCKET_THRESHOLDS = _bucket_thresholds()


def _t5_bucket(dist):
    max_exact = N_BUCKETS // 2
    far = jnp.full(dist.shape, max_exact, jnp.int32)
    for t in _BUCKET_THRESHOLDS:
        far = far + (dist >= t).astype(jnp.int32)
    return jnp.where(dist < max_exact, dist, far)


def _bias_from_bucket(bucket, tbl_ref, h):
    out = jnp.zeros(bucket.shape, F32)
    for j in range(N_BUCKETS):
        out = jnp.where(bucket == j, tbl_ref[j, h], out)
    return out


def _standardize(x):
    mu = jnp.mean(x, axis=-1, keepdims=True)
    xc = x - mu
    var = jnp.mean(xc * xc, axis=-1, keepdims=True)
    return xc * lax.rsqrt(var + LN_EPS)


def _sigmoid(x):
    return 1.0 / (1.0 + jnp.exp(-x))


def _log_sigmoid(x):
    return jnp.minimum(x, 0.0) - jnp.log1p(jnp.exp(-jnp.abs(x)))


def _gelu_tanh(x):
    c = math.sqrt(2.0 / math.pi)
    hx = 0.5 * x
    return hx + hx * jnp.tanh(x * (c + (c * 0.044715) * (x * x)))


def _dot_nt(a, b, **kw):
    return lax.dot_general(a, b, (((1,), (1,)), ((), ())), preferred_element_type=F32, **kw)


def _dot(a, b, **kw):
    return jnp.dot(a, b, preferred_element_type=F32, **kw)


def _ada_kernel(c_ref, w_ref, b_ref, o_ref):
    c = c_ref[...]
    s = c * _sigmoid(c)
    o_ref[...] = _dot(s, w_ref[...], precision=HIGHEST) + b_ref[...]


def _ada(c_all, w_ada, b_ada):
    n, d = c_all.shape
    n_out = w_ada.shape[1]
    tn = 1024
    return pl.pallas_call(
        _ada_kernel,
        grid=(n_out // tn,),
        in_specs=[pl.BlockSpec((n, d), lambda j: (0, 0)),
                  pl.BlockSpec((d, tn), lambda j: (0, j)),
                  pl.BlockSpec((1, tn), lambda j: (0, j))],
        out_specs=pl.BlockSpec((n, tn), lambda j: (0, j)),
        out_shape=jax.ShapeDtypeStruct((n, n_out), F32),
        compiler_params=_cparams(("parallel",)),
        name="ada_mod",
    )(c_all, w_ada, b_ada.reshape(1, n_out))


_PROJ_GROUPS = (ATTN_WIDTH, ATTN_WIDTH, ATTN_WIDTH, 2 * M_WIDTH, M_WIDTH, M_WIDTH, LANES)
_PROJ_OFFS = tuple(int(v) for v in np.cumsum((0,) + _PROJ_GROUPS))


def _inproj_kernel(x_ref, sc_ref, sh_ref, w_ref, *o_refs):
    h = _standardize(x_ref[...]) * sc_ref[0] + sh_ref[0]
    hb = h.astype(BF16)
    for g, o_ref in enumerate(o_refs):
        o_ref[...] = _dot(hb, w_ref[:, _PROJ_OFFS[g]:_PROJ_OFFS[g + 1]])


def _inproj(x2d, sc3, sh3, w_in_b, tm, rows_per_mod):
    r, d = x2d.shape
    m = sc3.shape[1]
    if m == 1:
        mod_map = lambda i: ((i * tm) // rows_per_mod, 0, 0)
    else:
        mod_map = lambda i: (i, 0, 0)
    return pl.pallas_call(
        _inproj_kernel,
        grid=(r // tm,),
        in_specs=[pl.BlockSpec((tm, d), lambda i: (i, 0)),
                  pl.BlockSpec((1, m, d), mod_map),
                  pl.BlockSpec((1, m, d), mod_map),
                  pl.BlockSpec(w_in_b.shape, lambda i: (0, 0))],
        out_specs=[pl.BlockSpec((tm, g), lambda i: (i, 0)) for g in _PROJ_GROUPS],
        out_shape=[jax.ShapeDtypeStruct((r, g), F32) for g in _PROJ_GROUPS],
        compiler_params=_cparams(("parallel",)),
        name="inproj",
    )(x2d, sc3, sh3, w_in_b)


def _inproj_prompt_kernel(x_ref, sc_ref, sh_ref, w_ref, wt_ref, qt_ref, k_ref, kt_ref, vt_ref, vb_ref,
                          mqk_ref, mv_ref, mo_ref, g_ref):
    h = _standardize(x_ref[...]) * sc_ref[0] + sh_ref[0]
    hb = h.astype(BF16)
    aw = ATTN_WIDTH
    k_ref[...] = _dot(hb, w_ref[:, _PROJ_OFFS[1]:_PROJ_OFFS[2]])
    for g, o_ref in ((3, mqk_ref), (4, mv_ref), (5, mo_ref), (6, g_ref)):
        o_ref[...] = _dot(hb, w_ref[:, _PROJ_OFFS[g]:_PROJ_OFFS[g + 1]])
    qt_ref[0] = _dot_nt(wt_ref[0:aw, :], hb)
    kt_ref[0] = _dot_nt(wt_ref[aw:2 * aw, :], hb)
    vt = _dot_nt(wt_ref[2 * aw:3 * aw, :], hb)
    vt_ref[0] = vt
    vb_ref[0, 0] = vt


def _inproj_prompt(x3, sc3, sh3, w_in_b, w_in_tb):
    b, s, d = x3.shape
    tm = MOBA_BLOCK
    nt = s // tm
    r = b * s
    aw = ATTN_WIDTH
    assert s % tm == 0
    row = lambda w: pl.BlockSpec((tm, w), lambda i: (i, 0))
    tr = pl.BlockSpec((1, aw, tm), lambda i: (i // nt, 0, i % nt))
    mod = pl.BlockSpec((1, 1, d), lambda i: (i // nt, 0, 0))
    return pl.pallas_call(
        _inproj_prompt_kernel,
        grid=(r // tm,),
        in_specs=[row(d), mod, mod,
                  pl.BlockSpec(w_in_b.shape, lambda i: (0, 0)),
                  pl.BlockSpec(w_in_tb.shape, lambda i: (0, 0))],
        out_specs=[tr, row(aw), tr, tr,
                   pl.BlockSpec((1, 1, aw, tm), lambda i: (i // nt, i % nt, 0, 0)),
                   row(2 * M_WIDTH), row(M_WIDTH), row(M_WIDTH), row(LANES)],
        out_shape=[jax.ShapeDtypeStruct((b, aw, s), F32), jax.ShapeDtypeStruct((r, aw), F32),
                   jax.ShapeDtypeStruct((b, aw, s), F32), jax.ShapeDtypeStruct((b, aw, s), F32),
                   jax.ShapeDtypeStruct((b, nt, aw, tm), F32),
                   jax.ShapeDtypeStruct((r, 2 * M_WIDTH), F32), jax.ShapeDtypeStruct((r, M_WIDTH), F32),
                   jax.ShapeDtypeStruct((r, M_WIDTH), F32), jax.ShapeDtypeStruct((r, LANES), F32)],
        compiler_params=_cparams(("parallel",)),
        name="inproj_prompt",
    )(x3.reshape(r, d), sc3, sh3, w_in_b, w_in_tb)


def _bias_tiles_kernel(tbl_ref, o_ref):
    h = pl.program_id(0)
    key = lax.broadcasted_iota(jnp.int32, (MOBA_BLOCK, MOBA_BLOCK), 0)
    qry = lax.broadcasted_iota(jnp.int32, (MOBA_BLOCK, MOBA_BLOCK), 1)
    for t in range(2):
        dist = jnp.maximum(qry - key + t * MOBA_BLOCK, 0)
        o_ref[0, t] = _bias_from_bucket(_t5_bucket(dist), tbl_ref, h)


def _bias_tiles(table):
    n_h = table.shape[1]
    return pl.pallas_call(
        _bias_tiles_kernel,
        grid=(n_h,),
        in_specs=[pl.BlockSpec(memory_space=pltpu.SMEM)],
        out_specs=pl.BlockSpec((1, 2, MOBA_BLOCK, MOBA_BLOCK), lambda h: (h, 0, 0, 0)),
        out_shape=jax.ShapeDtypeStruct((n_h, 2, MOBA_BLOCK, MOBA_BLOCK), F32),
        compiler_params=_cparams(("parallel",)),
        name="moba_bias_tiles",
    )(table)


_HEADS_PER_STEP = 4
_HEAD_GROUP_COLS = _HEADS_PER_STEP * ATTN_HEAD_DIM


def _moba_prompt_kernel(tbl_ref, qt_ref, k_ref, vb_ref, bias_ref, o_ref, selb_s, *, n_blocks):
    hp = pl.program_id(1)
    ob = pl.program_id(2)
    blk = MOBA_BLOCK
    dh = ATTN_HEAD_DIM
    gc = _HEAD_GROUP_COLS
    scale = dh ** -0.5
    heads = range(_HEADS_PER_STEP)

    kmean = jnp.concatenate(
        [jnp.sum(k_ref[0, n * blk:(n + 1) * blk, :], axis=0, keepdims=True) * (1.0 / blk)
         for n in range(n_blocks)], axis=0)
    sub = lax.broadcasted_iota(jnp.int32, (n_blocks, blk), 0)
    key = lax.broadcasted_iota(jnp.int32, (blk, blk), 0)
    qry = lax.broadcasted_iota(jnp.int32, (blk, blk), 1)
    start = pl.multiple_of(ob * blk, blk)
    fsub = lax.broadcasted_iota(jnp.int32, (gc, blk), 0)
    k_own = k_ref[0, pl.ds(start, blk), :].astype(BF16)
    c_fars = [tbl_ref[N_BUCKETS - 1, hp * _HEADS_PER_STEP + j] for j in heads]

    qzs = [jnp.where((fsub >= j * dh) & (fsub < (j + 1) * dh), qt_ref[0], 0.0) for j in heads]
    qzbs = [qz.astype(BF16) for qz in qzs]
    gates = [_dot(kmean, qz, precision=HIGHEST) for qz in qzs]
    qk_own = [_dot(k_own, qzb) for qzb in qzbs]
    for j in heads:
        g = jnp.where(sub < ob, gates[j], NEG_INF)
        sel = jnp.zeros((n_blocks, blk), F32)
        for _ in range(MOBA_TOPK):
            mx = jnp.max(g, axis=0, keepdims=True)
            idx = jnp.min(jnp.where(g == mx, sub, n_blocks), axis=0, keepdims=True)
            hit = sub == idx
            sel = jnp.where(hit, 1.0, sel)
            g = jnp.where(hit, NEG_INF, g)
        sel = jnp.where(sub < ob, sel, 0.0)
        for n in range(n_blocks):
            selb_s[j, n] = jnp.broadcast_to(sel[n:n + 1, :], (SUBLANES, blk))
    own = []
    for j in heads:
        s = jnp.where(key <= qry, qk_own[j] * scale + bias_ref[j, 0], NEG_INF)
        m0 = jnp.max(s, axis=0, keepdims=True)
        p = jnp.exp(s - m0)
        own.append((m0, jnp.sum(p, axis=0, keepdims=True), p.astype(BF16)))
    init = tuple((own[j][0], own[j][1], _dot(vb_ref[0, ob, j * dh:(j + 1) * dh, :].astype(BF16), own[j][2]))
                 for j in heads)

    def body(n, carry):
        st = pl.multiple_of(n * blk, blk)
        k_n = k_ref[0, pl.ds(st, blk), :].astype(BF16)
        qk = [_dot(k_n, qzbs[j]) for j in heads]
        stats = []
        for j in heads:
            m, l, _ = carry[j]
            bias = jnp.where(n == ob - 1, bias_ref[j, 1], c_fars[j])
            s = jnp.where(selb_s[j, n][0:1, :] > 0.5, qk[j] * scale + bias, NEG_INF)
            m_new = jnp.maximum(m, jnp.max(s, axis=0, keepdims=True))
            a = jnp.exp(m - m_new)
            p = jnp.exp(s - m_new)
            stats.append((m_new, a * l + jnp.sum(p, axis=0, keepdims=True), a, p.astype(BF16)))
        pv = [_dot(vb_ref[0, n, j * dh:(j + 1) * dh, :].astype(BF16), stats[j][3]) for j in heads]
        return tuple((stats[j][0], stats[j][1], stats[j][2] * carry[j][2] + pv[j]) for j in heads)

    final = lax.fori_loop(0, ob, body, init)
    for j in heads:
        _, l, acc = final[j]
        o_ref[0, j * dh:(j + 1) * dh, :] = acc / l


def _moba_prompt(qt, k, vb, table, bias_tiles):
    b, aw, s = qt.shape
    blk = MOBA_BLOCK
    n_blocks = s // blk
    hps = _HEADS_PER_STEP
    gc = _HEAD_GROUP_COLS
    assert s % blk == 0 and MOBA_TOPK <= n_blocks <= SUBLANES and aw % gc == 0 and gc % LANES == 0
    return pl.pallas_call(
        functools.partial(_moba_prompt_kernel, n_blocks=n_blocks),
        grid=(b, aw // gc, n_blocks),
        in_specs=[pl.BlockSpec(memory_space=pltpu.SMEM),
                  pl.BlockSpec((1, gc, blk), lambda i, g, j: (i, g, j)),
                  pl.BlockSpec((1, s, gc), lambda i, g, j: (i, 0, g)),
                  pl.BlockSpec((1, n_blocks, gc, blk), lambda i, g, j: (i, 0, g, 0)),
                  pl.BlockSpec((hps, 2, blk, blk), lambda i, g, j: (g, 0, 0, 0))],
        out_specs=pl.BlockSpec((1, gc, blk), lambda i, g, j: (i, g, j)),
        out_shape=jax.ShapeDtypeStruct((b, aw, s), F32),
        scratch_shapes=[pltpu.VMEM((hps, n_blocks, SUBLANES, blk), F32)],
        compiler_params=_cparams(("parallel", "parallel", "parallel")),
        name="moba_prompt",
    )(table, qt, k, vb, bias_tiles)


_CONV_PAD = SUBLANES


def _mlstm_prompt_kernel(q_ref, k_ref, v_ref, o_ref, g_ref, cw_ref, cb_ref, bg_ref,
                         mem_ref, c_out, n_out, m_out, xq_s, xk_s, c_s, n_s, m_s):
    ci = pl.program_id(1)
    L = M_CHUNK
    dh = M_HEAD_DIM
    pad = _CONV_PAD
    hist = CONV_WIDTH - 1

    @pl.when(ci == 0)
    def _():
        xq_s[0:pad, :] = jnp.zeros((pad, M_WIDTH), F32)
        xk_s[0:pad, :] = jnp.zeros((pad, M_WIDTH), F32)
        c_s[...] = jnp.zeros(c_s.shape, F32)
        n_s[...] = jnp.zeros(n_s.shape, F32)
        m_s[...] = jnp.zeros(m_s.shape, F32)

    xq_s[pad:pad + L, :] = q_ref[0]
    xk_s[pad:pad + L, :] = k_ref[0]

    def conv(x_s, col0):
        y = cb_ref[:, col0:col0 + M_WIDTH]
        for j in range(CONV_WIDTH):
            y = y + x_s[pad - hist + j:pad - hist + j + L, :] * cw_ref[j:j + 1, col0:col0 + M_WIDTH]
        return y * _sigmoid(y)

    qc = conv(xq_s, 0)
    kc = conv(xk_s, M_WIDTH) * (dh ** -0.5)
    tq = xq_s[pad + L - hist:pad + L, :]
    tk = xk_s[pad + L - hist:pad + L, :]
    xq_s[pad - hist:pad, :] = tq
    xk_s[pad - hist:pad, :] = tk

    heads = range(N_M_HEADS)
    hs = [slice(h * dh, (h + 1) * dh) for h in heads]
    qs = [qc[:, hs[h]] for h in heads]
    ks = [kc[:, hs[h]] for h in heads]
    vs = [v_ref[0, :, hs[h]] for h in heads]
    qbs = [q.astype(BF16) for q in qs]
    kbs = [k.astype(BF16) for k in ks]
    c_prevs = [c_s[h] for h in heads]
    n_prevs = [n_s[h:h + 1, :] for h in heads]
    m_prevs = [m_s[h:h + 1, 0:1] for h in heads]
    qk = [_dot_nt(qbs[h], kbs[h]) for h in heads]
    qc_prev = [_dot_nt(qbs[h], c_prevs[h].astype(BF16)) for h in heads]
    qn = [jnp.sum(qs[h] * n_prevs[h], axis=-1, keepdims=True) for h in heads]

    row = lax.broadcasted_iota(jnp.int32, (L, L), 0)
    col = lax.broadcasted_iota(jnp.int32, (L, L), 1)
    causal = col <= row
    pre = g_ref[0] + bg_ref[...]
    lane = lax.broadcasted_iota(jnp.int32, pre.shape, 1)
    is_f = (lane >= N_M_HEADS) & (lane < 2 * N_M_HEADS)
    cum = _dot(jnp.where(causal, 1.0, 0.0), jnp.where(is_f, _log_sigmoid(pre), 0.0), precision=HIGHEST)
    t8 = jnp.where(lane < N_M_HEADS, pre, cum)
    r8 = t8.T[0:2 * N_M_HEADS, :]
    i_rows = [r8[h:h + 1, :] for h in heads]
    b_rows = [r8[N_M_HEADS + h:N_M_HEADS + h + 1, :] for h in heads]
    i_cols = [t8[:, h:h + 1] for h in heads]
    b_cols = [t8[:, N_M_HEADS + h:N_M_HEADS + h + 1] for h in heads]

    intra = []
    for h in heads:
        d = jnp.where(causal, b_cols[h] - b_rows[h] + i_rows[h], NEG_INF)
        inter = b_cols[h] + m_prevs[h]
        m_t = jnp.maximum(inter, jnp.max(d, axis=-1, keepdims=True))
        w_inter = jnp.exp(inter - m_t)
        s = qk[h] * jnp.exp(d - m_t)
        intra.append((m_t, w_inter, s, jnp.sum(s, axis=-1, keepdims=True)))
    sv = [_dot(intra[h][2].astype(BF16), vs[h].astype(BF16)) for h in heads]

    carry = []
    for h in heads:
        b_last = b_rows[h][:, L - 1:L]
        g_row = b_last - b_rows[h] + i_rows[h]
        g_col = b_last - b_cols[h] + i_cols[h]
        m_new = jnp.maximum(b_last + m_prevs[h], jnp.max(g_row, axis=-1, keepdims=True))
        wc = jnp.exp(b_last + m_prevs[h] - m_new)
        ws = jnp.exp(g_col - m_new)
        carry.append((m_new, wc, ws, (ws * vs[h]).T.astype(BF16)))
    vk = [_dot(carry[h][3], kbs[h]) for h in heads]

    for h in heads:
        m_t, w_inter, _, s_sum = intra[h]
        num = w_inter * qc_prev[h] + sv[h]
        den = w_inter * qn[h] + s_sum
        hh = num / jnp.maximum(jnp.abs(den), jnp.exp(-m_t))
        mem_ref[0, :, hs[h]] = _sigmoid(o_ref[0, :, hs[h]]) * hh
    for h in heads:
        m_new, wc, ws, _ = carry[h]
        c_s[h] = wc * c_prevs[h] + vk[h]
        n_s[h:h + 1, :] = wc * n_prevs[h] + jnp.sum(ws * ks[h], axis=0, keepdims=True)
        m_s[h:h + 1, :] = jnp.broadcast_to(m_new, (1, LANES))

    @pl.when(ci == pl.num_programs(1) - 1)
    def _():
        c_out[0] = c_s[...]
        n_out[0] = n_s[0:N_M_HEADS, :]
        m_out[0] = m_s[...]


def _mlstm_prompt(mqk, mv, mo, gates, conv_w, conv_b, bg_row):
    b, s, _ = mv.shape
    L = M_CHUNK
    nc = s // L
    assert s % L == 0
    return pl.pallas_call(
        _mlstm_prompt_kernel,
        grid=(b, nc),
        in_specs=[pl.BlockSpec((1, L, M_WIDTH), lambda i, c: (i, c, 0)),
                  pl.BlockSpec((1, L, M_WIDTH), lambda i, c: (i, c, 1)),
                  pl.BlockSpec((1, L, M_WIDTH), lambda i, c: (i, c, 0)),
                  pl.BlockSpec((1, L, M_WIDTH), lambda i, c: (i, c, 0)),
                  pl.BlockSpec((1, L, LANES), lambda i, c: (i, c, 0)),
                  pl.BlockSpec((CONV_WIDTH, 2 * M_WIDTH), lambda i, c: (0, 0)),
                  pl.BlockSpec((1, 2 * M_WIDTH), lambda i, c: (0, 0)),
                  pl.BlockSpec((1, LANES), lambda i, c: (0, 0))],
        out_specs=[pl.BlockSpec((1, L, M_WIDTH), lambda i, c: (i, c, 0)),
                   pl.BlockSpec((1, N_M_HEADS, M_HEAD_DIM, M_HEAD_DIM), lambda i, c: (i, 0, 0, 0)),
                   pl.BlockSpec((1, N_M_HEADS, M_HEAD_DIM), lambda i, c: (i, 0, 0)),
                   pl.BlockSpec((1, SUBLANES, LANES), lambda i, c: (i, 0, 0))],
        out_shape=[jax.ShapeDtypeStruct((b, s, M_WIDTH), F32),
                   jax.ShapeDtypeStruct((b, N_M_HEADS, M_HEAD_DIM, M_HEAD_DIM), F32),
                   jax.ShapeDtypeStruct((b, N_M_HEADS, M_HEAD_DIM), F32),
                   jax.ShapeDtypeStruct((b, SUBLANES, LANES), F32)],
        scratch_shapes=[pltpu.VMEM((_CONV_PAD + L, M_WIDTH), F32),
                        pltpu.VMEM((_CONV_PAD + L, M_WIDTH), F32),
                        pltpu.VMEM((N_M_HEADS, M_HEAD_DIM, M_HEAD_DIM), F32),
                        pltpu.VMEM((SUBLANES, M_HEAD_DIM), F32),
                        pltpu.VMEM((SUBLANES, LANES), F32)],
        compiler_params=_cparams(("parallel", "arbitrary")),
        name="mlstm_prompt",
    )(mqk, mqk, mv, mo, gates, conv_w, conv_b, bg_row)


def _outproj_kernel(a_ref, m_ref, x_ref, beta_ref, w_ref, g1_ref, sc_ref, sh_ref, lg_ref, lb_ref,
                    x1_ref, h2_ref, *, attn_feature_major):
    attn = a_ref[0].T if attn_feature_major else a_ref[...]
    mixed = jnp.concatenate([attn, m_ref[...]], axis=-1) * beta_ref[...]
    y = _dot(mixed.astype(BF16), w_ref[...])
    z = ALPHA * x_ref[...] + g1_ref[0] * y
    x1 = _standardize(z) * lg_ref[...] + lb_ref[...]
    x1_ref[...] = x1
    h2_ref[...] = (_standardize(x1) * sc_ref[0] + sh_ref[0]).astype(h2_ref.dtype)


def _outproj(attn, mem, x2d, beta, w_out_b, g1, sc2, sh2, ln_g, ln_b, tm, rows_per_mod):
    r, d = x2d.shape
    m = g1.shape[1]
    if m == 1:
        mod_map = lambda i: ((i * tm) // rows_per_mod, 0, 0)
    else:
        mod_map = lambda i: (i, 0, 0)
    vec = pl.BlockSpec((1, d), lambda i: (0, 0))
    mod = pl.BlockSpec((1, m, d), mod_map)
    feature_major = attn.ndim == 3
    if feature_major:
        nt = rows_per_mod // tm
        assert rows_per_mod % tm == 0 and attn.shape[2] == rows_per_mod
        attn_spec = pl.BlockSpec((1, attn.shape[1], tm), lambda i: (i // nt, 0, i % nt))
    else:
        attn_spec = pl.BlockSpec((tm, attn.shape[1]), lambda i: (i, 0))
    return pl.pallas_call(
        functools.partial(_outproj_kernel, attn_feature_major=feature_major),
        grid=(r // tm,),
        in_specs=[attn_spec,
                  pl.BlockSpec((tm, mem.shape[1]), lambda i: (i, 0)),
                  pl.BlockSpec((tm, d), lambda i: (i, 0)),
                  vec,
                  pl.BlockSpec(w_out_b.shape, lambda i: (0, 0)),
                  mod, mod, mod, vec, vec],
        out_specs=[pl.BlockSpec((tm, d), lambda i: (i, 0)),
                   pl.BlockSpec((tm, d), lambda i: (i, 0))],
        out_shape=[jax.ShapeDtypeStruct((r, d), F32), jax.ShapeDtypeStruct((r, d), BF16)],
        compiler_params=_cparams(("parallel",)),
        name="outproj",
    )(attn, mem, x2d, beta, w_out_b, g1, sc2, sh2, ln_g, ln_b)


def _oddeven_merge_sort_pairs(n):
    pairs = []

    def merge(lo, m, r):
        step = r * 2
        if step < m:
            merge(lo, m, step)
            merge(lo + r, m, step)
            for i in range(lo + r, lo + m - r, step):
                pairs.append((i, i + r))
        else:
            pairs.append((lo, lo + r))

    def sort(lo, m):
        if m > 1:
            h = m // 2
            sort(lo, h)
            sort(lo + h, h)
            merge(lo, m, 1)

    sort(0, n)
    return tuple(pairs)


_SORT16 = _oddeven_merge_sort_pairs(PEER_TOPK)


def _vmax(a, b):
    if a is None:
        return b
    if b is None:
        return a
    return jnp.maximum(a, b)


def _cmpx(v, i, j):
    a, b = v[i], v[j]
    if b is None:
        return
    if a is None:
        v[i], v[j] = b, None
        return
    v[i], v[j] = jnp.maximum(a, b), jnp.minimum(a, b)


def _bitonic_to_desc(v):
    n = len(v)
    d = n // 2
    while d >= 1:
        for i in range(n):
            if (i & d) == 0:
                _cmpx(v, i, i + d)
        d //= 2
    return v


def _merge_top(x, y):
    n = len(x)
    return _bitonic_to_desc([_vmax(x[i], y[n - 1 - i]) for i in range(n)])


def _top16_desc(sc):
    groups = sc.shape[0] // SUBLANES
    assert groups == PEER_TOPK
    v = [sc[g * SUBLANES:(g + 1) * SUBLANES, :] for g in range(groups)]
    for i, j in _SORT16:
        _cmpx(v, i, j)
    shift = SUBLANES // 2
    while shift >= 1:
        partner = [pltpu.roll(a, shift, 0) for a in v]
        v = _merge_top(v, partner)
        shift //= 2
    return v


def _candidate_lists(a, b):
    k = PEER_TOPK
    lists = []
    for i in range(4):
        n = k // (i + 1)
        lists.append([a[i] + b[j] for j in range(n)])
    for j in range(3):
        n = k // (j + 1)
        col = [a[i] + b[j] for i in range(4, n)]
        if col:
            lists.append(col)
    return [l + [None] * (k - len(l)) for l in lists]


_RANK_STEP = 2.0


def _prefix_count(pred, vals):
    assert len(vals) == PEER_TOPK == 16
    sel = jnp.where
    c8 = pred(vals[7])
    c4 = pred(sel(c8, vals[11], vals[3]))
    c2 = pred(sel(c8, sel(c4, vals[13], vals[9]), sel(c4, vals[5], vals[1])))
    c1 = pred(sel(c8, sel(c4, sel(c2, vals[14], vals[12]), sel(c2, vals[10], vals[8])),
                  sel(c4, sel(c2, vals[6], vals[4]), sel(c2, vals[2], vals[0]))))
    n = (sel(c8, 8 * _RANK_STEP, 0.0) + sel(c4, 4 * _RANK_STEP, 0.0)
         + sel(c2, 2 * _RANK_STEP, 0.0) + sel(c1, _RANK_STEP, 0.0))
    return sel(pred(vals[15]), 16 * _RANK_STEP, n)


def _route_weights_kernel(sk_ref, wq_ref, o_ref):
    s = pl.program_id(0) % 2
    o_ref[...] = _dot_nt(sk_ref[s], wq_ref[...], precision=HIGHEST).astype(o_ref.dtype)


def _route_weights(sub_keys, w_query):
    d, width = w_query.shape
    kd = PEER_KEY_DIM // 2
    assert width == PEER_HEADS * 2 * kd and sub_keys.shape == (2, N_KEYS, kd)
    return pl.pallas_call(
        _route_weights_kernel,
        grid=(width // kd,),
        in_specs=[pl.BlockSpec(sub_keys.shape, lambda j: (0, 0, 0)),
                  pl.BlockSpec((d, kd), lambda j: (0, j))],
        out_specs=pl.BlockSpec((N_KEYS, d), lambda j: (j, 0)),
        out_shape=jax.ShapeDtypeStruct((PEER_HEADS * 2 * N_KEYS, d), BF16),
        compiler_params=_cparams(("parallel",)),
        name="peer_route_weights",
    )(sub_keys, w_query)


def _peer_route_kernel(h_ref, wr_ref, cnt_ref, rk_ref, a_ref, b_ref, sc_s, top_s, tz_s):
    tm = h_ref.shape[0]
    sc_s[...] = _dot_nt(wr_ref[...], h_ref[...])

    def scores(p, s):
        return sc_s[pl.ds(pl.multiple_of((2 * p + s) * N_KEYS, N_KEYS), N_KEYS), :]

    def head(p, carry):
        for s in range(2):
            srt = _top16_desc(scores(p, s))
            for r in range(PEER_TOPK):
                top_s[p, s, r] = srt[r]
        return carry

    lax.fori_loop(0, PEER_HEADS, head, 0)
    sub = lax.broadcasted_iota(jnp.int32, (SUBLANES, tm), 0)

    def on_sublanes(s, r):
        out = top_s[0, s, r]
        for p in range(1, PEER_HEADS):
            out = jnp.where(sub == p, top_s[p, s, r], out)
        return out

    top = [[on_sublanes(s, r) for r in range(PEER_TOPK)] for s in range(2)]
    lists = _candidate_lists(top[0], top[1])
    best = lists[0]
    for other in lists[1:]:
        best = _merge_top(best, other)
    z = jnp.ones_like(best[0])
    for r in range(1, PEER_TOPK):
        z = z + jnp.exp(best[r] - best[0])
    thr = best[PEER_TOPK - 1]
    for p in range(PEER_HEADS):
        tz_s[p, 0] = jnp.broadcast_to(thr[p:p + 1, :], (SUBLANES, tm))
        tz_s[p, 1] = jnp.broadcast_to(z[p:p + 1, :], (SUBLANES, tm))

    def emit(p, carry):
        s0 = scores(p, 0)
        s1 = scores(p, 1)
        t_row = tz_s[p, 0][0:1, :]
        z_row = tz_s[p, 1][0:1, :]
        b_top = [top_s[p, 1, r][0:1, :] for r in range(PEER_TOPK)]
        cnt_ref[p] = _prefix_count(lambda v: s0 + v >= t_row, b_top)
        rk_ref[p] = _prefix_count(lambda v: v > s1, b_top).astype(rk_ref.dtype)
        a_ref[p] = jnp.exp(s0 - top_s[p, 0, 0][0:1, :]) / z_row
        b_ref[p] = jnp.exp(s1 - top_s[p, 1, 0][0:1, :]).astype(b_ref.dtype)
        return carry

    lax.fori_loop(0, PEER_HEADS, emit, 0)


def _peer_route(h2, w_route, tm):
    r, d = h2.shape
    assert PEER_HEADS == SUBLANES and w_route.shape == (PEER_HEADS * 2 * N_KEYS, d)
    shp = (PEER_HEADS, N_KEYS, r)
    bspec = pl.BlockSpec((PEER_HEADS, N_KEYS, tm), lambda i: (0, 0, i))
    return pl.pallas_call(
        _peer_route_kernel,
        grid=(r // tm,),
        in_specs=[pl.BlockSpec((tm, d), lambda i: (i, 0)),
                  pl.BlockSpec(w_route.shape, lambda i: (0, 0))],
        out_specs=[bspec, bspec, bspec, bspec],
        out_shape=[jax.ShapeDtypeStruct(shp, F32), jax.ShapeDtypeStruct(shp, BF16),
                   jax.ShapeDtypeStruct(shp, F32), jax.ShapeDtypeStruct(shp, BF16)],
        scratch_shapes=[pltpu.VMEM((w_route.shape[0], tm), F32),
                        pltpu.VMEM((PEER_HEADS, 2, PEER_TOPK, SUBLANES, tm), F32),
                        pltpu.VMEM((PEER_HEADS, 2, SUBLANES, tm), F32)],
        compiler_params=_cparams(("parallel",)),
        name="peer_route",
    )(h2, w_route)


_EXPERT_CHUNK = SUBLANES * N_KEYS
_MIX_SUBTILE = 2 * SUBLANES


def _peer_mix_kernel(h_ref, u_ref, vt_ref, cnt_ref, rk_ref, a_ref, b_ref, o_ref, act_s, y_s, acc_s):
    c = pl.program_id(1)
    tm = h_ref.shape[0]

    @pl.when(c == 0)
    def _():
        acc_s[...] = jnp.zeros(acc_s.shape, F32)

    act_s[...] = _dot_nt(u_ref[...], h_ref[...])

    sub = _MIX_SUBTILE
    zero = jnp.zeros((sub, LANES), BF16)
    for ii in range(SUBLANES):
        for lc in range(tm // LANES):
            ls = slice(lc * LANES, (lc + 1) * LANES)
            cb = [jnp.broadcast_to(cnt_ref[p, ii:ii + 1, ls], (sub, LANES)).astype(BF16) for p in range(PEER_HEADS)]
            ab = [jnp.broadcast_to(a_ref[p, ii:ii + 1, ls], (sub, LANES)).astype(BF16) for p in range(PEER_HEADS)]
            for js in range(N_KEYS // sub):
                jr = slice(js * sub, (js + 1) * sub)
                terms = [jnp.maximum(jnp.minimum(ab[p] * b_ref[p, jr, ls], cb[p] - rk_ref[p, jr, ls]), zero)
                         for p in range(PEER_HEADS)]
                while len(terms) > 1:
                    terms = [terms[i] + terms[i + 1] for i in range(0, len(terms), 2)]
                w = terms[0]
                rs = slice(ii * N_KEYS + js * sub, ii * N_KEYS + (js + 1) * sub)
                y_s[rs, ls] = w * _gelu_tanh(act_s[rs, ls].astype(BF16))
    acc_s[...] += _dot(vt_ref[0], y_s[...])

    @pl.when(c == pl.num_programs(1) - 1)
    def _():
        o_ref[...] = acc_s[...].T


def _peer_mix(h2, u_b, vt_c, cnt, rk, a, b, tm):
    r, d = h2.shape
    n_exp = u_b.shape[0]
    ch = _EXPERT_CHUNK
    assert n_exp == N_KEYS * N_KEYS and vt_c.shape == (n_exp // ch, d, ch)
    row_blk = pl.BlockSpec((PEER_HEADS, SUBLANES, tm), lambda i, c: (0, c, i))
    all_blk = pl.BlockSpec((PEER_HEADS, N_KEYS, tm), lambda i, c: (0, 0, i))
    return pl.pallas_call(
        _peer_mix_kernel,
        grid=(r // tm, n_exp // ch),
        in_specs=[pl.BlockSpec((tm, d), lambda i, c: (i, 0)),
                  pl.BlockSpec((ch, d), lambda i, c: (c, 0)),
                  pl.BlockSpec((1, d, ch), lambda i, c: (c, 0, 0)),
                  row_blk, all_blk, row_blk, all_blk],
        out_specs=pl.BlockSpec((tm, d), lambda i, c: (i, 0)),
        out_shape=jax.ShapeDtypeStruct((r, d), F32),
        scratch_shapes=[pltpu.VMEM((ch, tm), F32), pltpu.VMEM((ch, tm), BF16), pltpu.VMEM((d, tm), F32)],
        compiler_params=_cparams(("parallel", "arbitrary")),
        name="peer_mix",
    )(h2, u_b, vt_c, cnt, rk, a, b)


def _peer(h2, w_route, u_b, vt_b, tm_route, tm_mix):
    cnt, rk, a, b = _peer_route(h2, w_route, tm_route)
    return _peer_mix(h2, u_b, vt_b, cnt, rk, a, b, tm_mix)


def _final_kernel(x_ref, f_ref, g2_ref, lg_ref, lb_ref, o_ref):
    z = ALPHA * x_ref[...] + g2_ref[0] * f_ref[...]
    o_ref[...] = _standardize(z) * lg_ref[...] + lb_ref[...]


def _final(x1, f, g2, ln_g, ln_b, tm, rows_per_mod):
    r, d = x1.shape
    m = g2.shape[1]
    if m == 1:
        mod_map = lambda i: ((i * tm) // rows_per_mod, 0, 0)
    else:
        mod_map = lambda i: (i, 0, 0)
    vec = pl.BlockSpec((1, d), lambda i: (0, 0))
    return pl.pallas_call(
        _final_kernel,
        grid=(r // tm,),
        in_specs=[pl.BlockSpec((tm, d), lambda i: (i, 0)),
                  pl.BlockSpec((tm, d), lambda i: (i, 0)),
                  pl.BlockSpec((1, m, d), mod_map), vec, vec],
        out_specs=pl.BlockSpec((tm, d), lambda i: (i, 0)),
        out_shape=jax.ShapeDtypeStruct((r, d), F32),
        compiler_params=_cparams(("parallel",)),
        name="final_norm",
    )(x1, f, g2, ln_g, ln_b)


_PAGES_PER_STEP = 64


def _page_copies(pt_ref, k_hbm, buf, sem, step, half):
    pp = buf.shape[1]
    per_tok = pt_ref.shape[1]
    tok = (step * pp) // per_tok
    first = (step * pp) % per_tok
    return [pltpu.make_async_copy(k_hbm.at[pt_ref[tok, first + k]], buf.at[half, k], sem.at[half])
            for k in range(pp)]


def _page_sum_kernel(pt_ref, k_hbm, o_ref, buf, sem):
    i = pl.program_id(0)
    n = pl.num_programs(0)
    half = i % 2
    _, pp, n_h, dh, page = buf.shape
    width = n_h * dh
    copies = functools.partial(_page_copies, pt_ref, k_hbm, buf, sem)

    @pl.when(i == 0)
    def _():
        for c in copies(0, 0):
            c.start()

    @pl.when(i + 1 < n)
    def _():
        for c in copies(i + 1, 1 - half):
            c.start()

    for c in copies(i, half):
        c.wait()

    lane = lax.broadcasted_iota(jnp.int32, (width, LANES), 1)
    t = jnp.zeros((width, LANES), F32)
    for pg in range(pp):
        col = jnp.sum(buf[half, pg].reshape(width, page), axis=-1, keepdims=True)
        t = jnp.where(lane == pg, col, t)
    o_ref[...] = t.T[0:pp, :]


def _page_sums(page_table, cache_t):
    n_phys, n_h, dh, page = cache_t.shape
    db, per_tok = page_table.shape
    pp = _PAGES_PER_STEP
    assert per_tok % pp == 0 and pp <= LANES
    grid_spec = pltpu.PrefetchScalarGridSpec(
        num_scalar_prefetch=1,
        grid=(db * per_tok // pp,),
        in_specs=[pl.BlockSpec(memory_space=pl.ANY)],
        out_specs=pl.BlockSpec((pp, n_h * dh), lambda i, pt: (i, 0)),
        scratch_shapes=[pltpu.VMEM((2, pp, n_h, dh, page), F32), pltpu.SemaphoreType.DMA((2,))],
    )
    return pl.pallas_call(
        _page_sum_kernel,
        grid_spec=grid_spec,
        out_shape=jax.ShapeDtypeStruct((db * per_tok, n_h * dh), F32),
        compiler_params=_cparams(("arbitrary",)),
        name="page_sums",
    )(page_table, cache_t)


def _block_gate_kernel(ps_ref, q_ref, sel_ref, km_s, *, n_blocks):
    ppb = MOBA_BLOCK // PAGE_SIZE
    width = ps_ref.shape[1]

    km_s[...] = jnp.zeros(km_s.shape, F32)

    def gather(n, carry):
        acc = jnp.zeros((1, width), F32)
        for j in range(ppb):
            acc = acc + ps_ref[pl.ds(n * ppb + j, 1), :]
        km_s[pl.ds(n, 1), :] = acc * (1.0 / MOBA_BLOCK)
        return carry

    lax.fori_loop(0, n_blocks, gather, 0)
    q = q_ref[0]
    sub = lax.broadcasted_iota(jnp.int32, (N_ATTN_HEADS, width), 0)
    lane_w = lax.broadcasted_iota(jnp.int32, (N_ATTN_HEADS, width), 1)
    qb = jnp.where(lane_w // ATTN_HEAD_DIM == sub, jnp.broadcast_to(q, (N_ATTN_HEADS, width)), 0.0)
    gate = _dot_nt(qb, km_s[...], precision=HIGHEST)
    lane = lax.broadcasted_iota(jnp.int32, gate.shape, 1)
    g = jnp.where(lane < n_blocks, gate, NEG_INF)
    out = jnp.zeros(gate.shape, jnp.int32)
    for k in range(MOBA_TOPK):
        mx = jnp.max(g, axis=-1, keepdims=True)
        idx = jnp.min(jnp.where(g == mx, lane, LANES), axis=-1, keepdims=True)
        out = jnp.where(lane == k, idx, out)
        g = jnp.where(lane == idx, NEG_INF, g)
    sel_ref[0] = out


def _block_gate(page_sums, q3, n_blocks):
    db = q3.shape[0]
    per_tok = page_sums.shape[0] // db
    assert MOBA_TOPK <= n_blocks <= LANES and per_tok * PAGE_SIZE == n_blocks * MOBA_BLOCK
    return pl.pallas_call(
        functools.partial(_block_gate_kernel, n_blocks=n_blocks),
        grid=(db,),
        in_specs=[pl.BlockSpec((per_tok, page_sums.shape[1]), lambda i: (i, 0)),
                  pl.BlockSpec((1, 1, q3.shape[2]), lambda i: (i, 0, 0))],
        out_specs=pl.BlockSpec((1, N_ATTN_HEADS, LANES), lambda i: (i, 0, 0)),
        scratch_shapes=[pltpu.VMEM((LANES, page_sums.shape[1]), F32)],
        out_shape=jax.ShapeDtypeStruct((db, N_ATTN_HEADS, LANES), jnp.int32),
        compiler_params=_cparams(("parallel",)),
        name="block_gate",
    )(page_sums, q3)


_PAGES_PER_BLOCK = MOBA_BLOCK // PAGE_SIZE
_SEL_PAGES = MOBA_TOPK * _PAGES_PER_BLOCK


def _sample_page_copies(pt_ref, sel_ref, k_hbm, v_hbm, kbuf, vbuf, sem, bb, par):
    out = []
    for h in range(N_ATTN_HEADS):
        for kt in range(MOBA_TOPK):
            blk = sel_ref[bb, h * MOBA_TOPK + kt]
            for pp in range(_PAGES_PER_BLOCK):
                page = pt_ref[bb, blk * _PAGES_PER_BLOCK + pp]
                slot = h * _SEL_PAGES + kt * _PAGES_PER_BLOCK + pp
                out.append(pltpu.make_async_copy(k_hbm.at[page, h], kbuf.at[par, slot], sem.at[0, par]))
                out.append(pltpu.make_async_copy(v_hbm.at[page, h], vbuf.at[par, slot], sem.at[1, par]))
    return out


def _col_from_row(row):
    n = row.shape[1]
    r = lax.broadcasted_iota(jnp.int32, (n, n), 0)
    c = lax.broadcasted_iota(jnp.int32, (n, n), 1)
    return jnp.sum(jnp.where(r == c, jnp.broadcast_to(row, (n, n)), 0.0), axis=-1, keepdims=True)


def _row_from_col(col):
    n = col.shape[0]
    r = lax.broadcasted_iota(jnp.int32, (n, n), 0)
    c = lax.broadcasted_iota(jnp.int32, (n, n), 1)
    return jnp.sum(jnp.where(r == c, jnp.broadcast_to(col, (n, n)), 0.0), axis=0, keepdims=True)


def _moba_sample_kernel(pt_ref, sel_ref, tbl_ref, q_ref, kn_ref, vn_ref, k_hbm, v_hbm, o_ref,
                        kbuf, vbuf, bias_s, sem, *, past_len):
    b = pl.program_id(0)
    nb = pl.num_programs(0)
    par = b % 2
    scale = ATTN_HEAD_DIM ** -0.5
    copies = functools.partial(_sample_page_copies, pt_ref, sel_ref, k_hbm, v_hbm, kbuf, vbuf, sem)

    @pl.when(b == 0)
    def _():
        for n, c in enumerate(copies(0, 0)):
            c.start(priority=n % 2)

    @pl.when(b + 1 < nb)
    def _():
        for n, c in enumerate(copies(b + 1, 1 - par)):
            c.start(priority=n % 2)

    for c in copies(b, par):
        c.wait()

    sub = lax.broadcasted_iota(jnp.int32, (_PAGES_PER_BLOCK, PAGE_SIZE), 0)
    lane = lax.broadcasted_iota(jnp.int32, (_PAGES_PER_BLOCK, PAGE_SIZE), 1)
    for h in range(N_ATTN_HEADS):
        q = q_ref[0, h:h + 1, :]
        q_col = _col_from_row(q)
        for kt in range(MOBA_TOPK):
            pos0 = sel_ref[b, h * MOBA_TOPK + kt] * MOBA_BLOCK
            near = past_len - pos0 - (MOBA_BLOCK - 1) < MAX_DISTANCE
            rows = slice(kt * _PAGES_PER_BLOCK, (kt + 1) * _PAGES_PER_BLOCK)

            @pl.when(near)
            def _():
                dist = jnp.maximum(past_len - (pos0 + sub * PAGE_SIZE + lane), 0)
                bias_s[rows, :] = _bias_from_bucket(_t5_bucket(dist), tbl_ref, h)

            @pl.when(jnp.logical_not(near))
            def _():
                bias_s[rows, :] = jnp.full((_PAGES_PER_BLOCK, PAGE_SIZE), tbl_ref[N_BUCKETS - 1, h], F32)

        s = jnp.concatenate([jnp.sum(kbuf[par, h * _SEL_PAGES + j] * q_col, axis=0, keepdims=True)
                             for j in range(_SEL_PAGES)], axis=0) * scale + bias_s[0:_SEL_PAGES, :]
        s_new = jnp.sum(kn_ref[0, h:h + 1, :] * q, axis=-1, keepdims=True) * scale + tbl_ref[0, h]
        m = jnp.maximum(jnp.max(jnp.max(s, axis=-1, keepdims=True), axis=0, keepdims=True), s_new)
        p = jnp.exp(s - m)
        p_new = jnp.exp(s_new - m)
        den = jnp.sum(jnp.sum(p, axis=-1, keepdims=True), axis=0, keepdims=True) + p_new
        pv = vbuf[par, h * _SEL_PAGES] * p[0:1, :]
        for j in range(1, _SEL_PAGES):
            pv = pv + vbuf[par, h * _SEL_PAGES + j] * p[j:j + 1, :]
        num = _row_from_col(jnp.sum(pv, axis=-1, keepdims=True)) + p_new * vn_ref[0, h:h + 1, :]
        o_ref[0, h:h + 1, :] = num / den


def _moba_sample(page_table, sel, table, q3, k3, v3, cache_kt, cache_vt, past_len):
    db, n_h, dh = q3.shape
    assert cache_kt.shape[1:] == (n_h, dh, PAGE_SIZE) and PAGE_SIZE == LANES and _SEL_PAGES <= SUBLANES
    vec = pl.BlockSpec((1, n_h, dh), lambda i, pt, sl: (i, 0, 0))
    grid_spec = pltpu.PrefetchScalarGridSpec(
        num_scalar_prefetch=2,
        grid=(db,),
        in_specs=[pl.BlockSpec(memory_space=pltpu.SMEM), vec, vec, vec,
                  pl.BlockSpec(memory_space=pl.ANY), pl.BlockSpec(memory_space=pl.ANY)],
        out_specs=vec,
        scratch_shapes=[pltpu.VMEM((2, n_h * _SEL_PAGES, dh, PAGE_SIZE), F32),
                        pltpu.VMEM((2, n_h * _SEL_PAGES, dh, PAGE_SIZE), F32),
                        pltpu.VMEM((SUBLANES, PAGE_SIZE), F32),
                        pltpu.SemaphoreType.DMA((2, 2))],
    )
    return pl.pallas_call(
        functools.partial(_moba_sample_kernel, past_len=past_len),
        grid_spec=grid_spec,
        out_shape=jax.ShapeDtypeStruct((db, n_h, dh), F32),
        compiler_params=_cparams(("arbitrary",)),
        name="moba_sample",
    )(page_table, sel, table, q3, k3, v3, cache_kt, cache_vt)


def _mlstm_step_kernel(qk_ref, cs_ref, v_ref, o_ref, g_ref, cw_ref, cb_ref, bg_ref,
                       c_ref, n_ref, m_ref, mem_ref, c_out, n_out, m_out):
    dh = M_HEAD_DIM
    hist = CONV_WIDTH - 1
    y = cb_ref[...] + qk_ref[0] * cw_ref[hist:hist + 1, :]
    for j in range(hist):
        y = y + cs_ref[0, j:j + 1, :] * cw_ref[j:j + 1, :]
    y = y * _sigmoid(y)
    pre = g_ref[0] + bg_ref[...]
    row = lax.broadcasted_iota(jnp.int32, (dh, dh), 0)
    col = lax.broadcasted_iota(jnp.int32, (dh, dh), 1)
    lane = lax.broadcasted_iota(jnp.int32, (1, LANES), 1)
    m_all = jnp.zeros((1, LANES), F32)
    for h in range(N_M_HEADS):
        sl = slice(h * dh, (h + 1) * dh)
        q = y[:, sl]
        k = y[:, M_WIDTH + h * dh:M_WIDTH + (h + 1) * dh] * (dh ** -0.5)
        v = v_ref[0, :, sl]
        i_t = pre[:, h:h + 1]
        logf = _log_sigmoid(pre[:, N_M_HEADS + h:N_M_HEADS + h + 1])
        c_prev = c_ref[0, h]
        n_prev = n_ref[0, h:h + 1, :]
        m_prev = m_ref[0, :, h:h + 1]
        inter = logf + m_prev
        m_t = jnp.maximum(inter, i_t)
        w_inter = jnp.exp(inter - m_t)
        s = jnp.sum(q * k, axis=-1, keepdims=True) * jnp.exp(i_t - m_t)
        cq = _dot_nt(jnp.broadcast_to(q, (SUBLANES, dh)), c_prev, precision=HIGHEST)[0:1, :]
        num = w_inter * cq + s * v
        den = w_inter * jnp.sum(n_prev * q, axis=-1, keepdims=True) + s
        hh = num / jnp.maximum(jnp.abs(den), jnp.exp(-m_t))
        mem_ref[0, :, sl] = _sigmoid(o_ref[0, :, sl]) * hh
        wc = jnp.exp(inter - m_t)
        ws = jnp.exp(i_t - m_t)
        v_col = jnp.sum(jnp.where(row == col, jnp.broadcast_to(v, (dh, dh)), 0.0), axis=-1, keepdims=True)
        c_out[0, h] = wc * c_prev + (ws * v_col) * k
        n_out[0, h:h + 1, :] = wc * n_prev + ws * k
        m_all = jnp.where(lane == h, m_t, m_all)
    m_out[0] = m_all


def _mlstm_step(mqk, cstate, mv, mo, gates, conv_w, conv_b, bg_row, c0, n0, m0):
    db = mqk.shape[0]
    r3 = lambda w: pl.BlockSpec((1, 1, w), lambda i: (i, 0, 0))
    return pl.pallas_call(
        _mlstm_step_kernel,
        grid=(db,),
        in_specs=[r3(2 * M_WIDTH),
                  pl.BlockSpec((1, CONV_WIDTH - 1, 2 * M_WIDTH), lambda i: (i, 0, 0)),
                  r3(M_WIDTH), r3(M_WIDTH), r3(LANES),
                  pl.BlockSpec((CONV_WIDTH, 2 * M_WIDTH), lambda i: (0, 0)),
                  pl.BlockSpec((1, 2 * M_WIDTH), lambda i: (0, 0)),
                  pl.BlockSpec((1, LANES), lambda i: (0, 0)),
                  pl.BlockSpec((1, N_M_HEADS, M_HEAD_DIM, M_HEAD_DIM), lambda i: (i, 0, 0, 0)),
                  pl.BlockSpec((1, N_M_HEADS, M_HEAD_DIM), lambda i: (i, 0, 0)),
                  pl.BlockSpec((1, 1, N_M_HEADS), lambda i: (i, 0, 0))],
        out_specs=[r3(M_WIDTH),
                   pl.BlockSpec((1, N_M_HEADS, M_HEAD_DIM, M_HEAD_DIM), lambda i: (i, 0, 0, 0)),
                   pl.BlockSpec((1, N_M_HEADS, M_HEAD_DIM), lambda i: (i, 0, 0)),
                   r3(LANES)],
        out_shape=[jax.ShapeDtypeStruct((db, 1, M_WIDTH), F32),
                   jax.ShapeDtypeStruct((db, N_M_HEADS, M_HEAD_DIM, M_HEAD_DIM), F32),
                   jax.ShapeDtypeStruct((db, N_M_HEADS, M_HEAD_DIM), F32),
                   jax.ShapeDtypeStruct((db, 1, LANES), F32)],
        compiler_params=_cparams(("parallel",)),
        name="mlstm_step",
    )(mqk, cstate, mv, mo, gates, conv_w, conv_b, bg_row, c0, n0, m0)


def _pad_rows(x, mult):
    r = x.shape[0]
    rp = -(-r // mult) * mult
    return x if rp == r else jnp.pad(x, ((0, rp - r), (0, 0)))


def kernel(x_prompt, x_sample, cache_k, cache_v, page_table, state_C, state_n, state_m, state_conv,
           c_prompt, c_sample, rel_bias_table, w_ada, b_ada, w_in, b_gate, conv_w, conv_b,
           beta_attn, beta_mlstm, w_out, ln1_g, ln1_b, w_query, sub_keys, expert_u, expert_v,
           ln2_g, ln2_b):
    assert w_ada.shape[0] == DEPTH == 1
    B, S, D = x_prompt.shape
    DB, T, _ = x_sample.shape
    assert T == 1
    H, dh = N_ATTN_HEADS, ATTN_HEAD_DIM
    past_len = page_table.shape[1] * PAGE_SIZE
    assert past_len % MOBA_BLOCK == 0
    l = 0

    gate_cols = 2 * N_M_HEADS
    w_in_p = jnp.pad(w_in[l], ((0, 0), (0, LANES - gate_cols))).astype(BF16)
    w_out_b = w_out[l].astype(BF16)
    w_route = _route_weights(sub_keys[l], w_query[l])
    u_b = expert_u[l].astype(BF16)
    n_exp = expert_v.shape[1]
    vt_b = jnp.transpose(expert_v[l].reshape(n_exp // _EXPERT_CHUNK, _EXPERT_CHUNK, D), (0, 2, 1)).astype(BF16)
    beta = jnp.concatenate([beta_attn[l], beta_mlstm[l]])[None, :]
    bg = b_gate[l]
    bg_row = jnp.pad(bg, (0, LANES - gate_cols))[None, :]
    cw, cb = conv_w[l], conv_b[l][None, :]
    table = rel_bias_table
    lg1, lb1, lg2, lb2 = ln1_g[l][None, :], ln1_b[l][None, :], ln2_g[l][None, :], ln2_b[l][None, :]

    mod = _ada(jnp.concatenate([c_prompt, c_sample], axis=0), w_ada[l], b_ada[l])
    sh1, sc1, g1, sh2, sc2, g2 = [mod[:, i * D:(i + 1) * D] for i in range(6)]
    sc1, sc2 = 1.0 + sc1, 1.0 + sc2
    pm = lambda t: t[:B][:, None, :]
    sm = lambda t: t[B:][None, :, :]

    xp2 = x_prompt.reshape(B * S, D)
    w_in_tb = w_in[l].T[:3 * ATTN_WIDTH].astype(BF16)
    aqt, ak, akt, avt, avb, mqk, mv, mo, gates = _inproj_prompt(x_prompt, pm(sc1), pm(sh1), w_in_p, w_in_tb)
    attn = _moba_prompt(aqt, ak.reshape(B, S, ATTN_WIDTH), avb, table, _bias_tiles(table))
    mqk3 = mqk.reshape(B, S, 2 * M_WIDTH)
    mem, c_p, n_p, m_p = _mlstm_prompt(mqk3, mv.reshape(B, S, M_WIDTH), mo.reshape(B, S, M_WIDTH),
                                       gates.reshape(B, S, LANES), cw, cb, bg_row)
    x1, h2 = _outproj(attn, mem.reshape(B * S, M_WIDTH), xp2, beta, w_out_b,
                      pm(g1), pm(sc2), pm(sh2), lg1, lb1, 256, S)
    f = _peer(h2, w_route, u_b, vt_b, 256, 512)
    y_prompt = _final(x1, f, pm(g2), lg2, lb2, 512, S).reshape(B, S, D)
    from_t = lambda t: jnp.transpose(t.reshape(B, H, dh, S), (0, 3, 1, 2))[None]
    k_prompt = from_t(akt)
    v_prompt = from_t(avt)
    conv_prompt = mqk3[:, S - (CONV_WIDTH - 1):, :][None]

    xs2 = x_sample.reshape(DB, D)
    saq, sak, sav, smqk, smv, smo, sgates = _inproj(xs2, sm(sc1), sm(sh1), w_in_p, DB, 1)
    cache_kt = jnp.transpose(cache_k[l], (0, 2, 3, 1))
    cache_vt = jnp.transpose(cache_v[l], (0, 2, 3, 1))
    psum = _page_sums(page_table, cache_kt)
    n_blocks = past_len // MOBA_BLOCK
    sel = _block_gate(psum, saq.reshape(DB, 1, H * dh), n_blocks)
    sel = sel[:, :, :MOBA_TOPK].reshape(DB, H * MOBA_TOPK)
    h3 = lambda t: t.reshape(DB, H, dh)
    s_attn = _moba_sample(page_table, sel, table, h3(saq), h3(sak), h3(sav), cache_kt, cache_vt, past_len)
    s_mem, c_s, n_s, m_s = _mlstm_step(
        smqk.reshape(DB, 1, 2 * M_WIDTH), state_conv[l], smv.reshape(DB, 1, M_WIDTH),
        smo.reshape(DB, 1, M_WIDTH), sgates.reshape(DB, 1, LANES), cw, cb, bg_row,
        state_C[l], state_n[l], state_m[l].reshape(DB, 1, N_M_HEADS))
    sx1, sh2_ = _outproj(s_attn.reshape(DB, H * dh), s_mem.reshape(DB, M_WIDTH), xs2, beta, w_out_b,
                         sm(g1), sm(sc2), sm(sh2), lg1, lb1, DB, 1)
    sf = _peer(_pad_rows(sh2_, LANES), w_route, u_b, vt_b, LANES, LANES)[:DB]
    y_sample = _final(sx1, sf, sm(g2), lg2, lb2, DB, 1).reshape(DB, 1, D)
    conv_sample = jnp.concatenate([state_conv[l][:, 1:, :], smqk.reshape(DB, 1, 2 * M_WIDTH)], axis=1)[None]

    return (y_prompt, y_sample,
            k_prompt, v_prompt, c_p[None], n_p[None], m_p[:, :N_M_HEADS, 0][None], conv_prompt,
            sak.reshape(1, DB, 1, H, dh), sav.reshape(1, DB, 1, H, dh),
            c_s[None], n_s[None], m_s[:, 0, :N_M_HEADS][None], conv_sample)
```
